```python
import jax, jax.numpy as jnp
from jax import lax
import numpy as np

D_MODEL = 1024
BATCH = 8
SEQ = 4096
DEPTH = 2
DEC_BATCH = 32
DEC_SEQ = 1
PAST_LEN = 16384
PAGE_SIZE = 128

MIX_WIDTH = D_MODEL
HEAD_DIM = 64
NSA_WIDTH = MIX_WIDTH // 2
NSA_HEADS = NSA_WIDTH // HEAD_DIM
NSA_KV_HEADS = 2
HEADS_PER_KV = NSA_HEADS // NSA_KV_HEADS
KV_WIDTH = NSA_KV_HEADS * HEAD_DIM
N_BRANCH = 3
CMP_STRIDE = 16
CMP_BLOCK = 2 * CMP_STRIDE
CMP_HIDDEN = HEAD_DIM
SEL_BLOCK = 64
SEL_TOPN = 16
WINDOW = 512
ROPE_THETA = 500000.0
ROPE_DIM = HEAD_DIM // 4
Q_BLOCK = 64
RWKV_WIDTH = MIX_WIDTH - NSA_WIDTH
RWKV_HEAD = 64
RWKV_HEADS = RWKV_WIDTH // RWKV_HEAD
DECAY_LORA = 64
AAA_LORA = 64
GATE_LORA = 128
NSA_PROJ = NSA_WIDTH + 6 * KV_WIDTH + N_BRANCH * NSA_HEADS
RWKV_PROJ = 3 * RWKV_WIDTH + DECAY_LORA + AAA_LORA + GATE_LORA
IN_PROJ = NSA_PROJ + RWKV_PROJ
D_FF = 2816
CONV_WIDTH = 3
NORM_EPS = 1e-6
LNX_EPS = 64e-5
NEG = -1e30
FORCED_SCORE = 1e9

kernel_name = "hybrid_nsa_rwkv7_convffn_step"


def rms_norm(x, g):
    xf = x.astype(jnp.float32)
    y = xf * lax.rsqrt(jnp.mean(xf * xf, axis=-1, keepdims=True) + NORM_EPS)
    return (y * g.astype(jnp.float32)).astype(x.dtype)


def rope(x, pos):
    half = ROPE_DIM // 2
    inv_freq = ROPE_THETA ** (-jnp.arange(half, dtype=jnp.float32) / half)
    ang = pos.astype(jnp.float32)[:, None] * inv_freq[None, :]
    cos = jnp.cos(ang)[None, :, None, :]
    sin = jnp.sin(ang)[None, :, None, :]
    xf = x.astype(jnp.float32)
    x1, x2 = xf[..., :half], xf[..., half:ROPE_DIM]
    out = jnp.concatenate([x1 * cos - x2 * sin, x2 * cos + x1 * sin, xf[..., ROPE_DIM:]], axis=-1)
    return out.astype(x.dtype)


def masked_softmax(s, mask):
    s = jnp.where(mask, s.astype(jnp.float32), NEG)
    m = jnp.max(s, axis=-1, keepdims=True)
    e = jnp.where(mask, jnp.exp(s - m), 0.0)
    return e / jnp.maximum(jnp.sum(e, axis=-1, keepdims=True), 1e-30)


def nsa_project(proj, pos):
    B, T, _ = proj.shape
    q = rope(proj[..., :NSA_WIDTH].reshape(B, T, NSA_HEADS, HEAD_DIM), pos)
    kv = proj[..., NSA_WIDTH:NSA_WIDTH + 6 * KV_WIDTH].reshape(B, T, N_BRANCH, 2, NSA_KV_HEADS, HEAD_DIM)
    k = rope(kv[:, :, :, 0].reshape(B, T, N_BRANCH * NSA_KV_HEADS, HEAD_DIM), pos)
    k = k.reshape(B, T, N_BRANCH, NSA_KV_HEADS, HEAD_DIM)
    kv = jnp.stack([k, kv[:, :, :, 1]], axis=3)
    kv_rows = kv[:, :, :2].reshape(B, T, 4, NSA_KV_HEADS, HEAD_DIM)
    win_rows = kv[:, :, 2]
    gates = jax.nn.sigmoid(proj[..., NSA_WIDTH + 6 * KV_WIDTH:].astype(jnp.float32))
    return q, kv_rows, win_rows, gates.reshape(B, T, NSA_HEADS, N_BRANCH)


def compress(rows, pe, w1, w2):
    B, L = rows.shape[:2]
    n_chunk = L // CMP_STRIDE
    ch = rows[:, :n_chunk * CMP_STRIDE].reshape(B, n_chunk, CMP_STRIDE, NSA_KV_HEADS, HEAD_DIM)
    blocks = jnp.concatenate([ch[:, :-1], ch[:, 1:]], axis=2) + pe[None, None, :, None, :]
    h = jax.nn.gelu(jnp.einsum('bnlgd,lde->bnge', blocks, w1))
    return jnp.einsum('bnge,ef->bgnf', h, w2)


def cmp_to_sel(n_cmp, n_sel):
    c0 = jnp.arange(n_cmp) * CMP_STRIDE
    s0 = jnp.arange(n_sel) * SEL_BLOCK
    hit = (c0[:, None] < s0[None, :] + SEL_BLOCK) & (c0[:, None] + CMP_BLOCK > s0[None, :])
    return hit.astype(jnp.float32)


def cmp_sel_attend(q, qpos, kc, vc, ksb, vsb):
    B, Tq = q.shape[:2]
    scale = HEAD_DIM ** -0.5
    qg = q.reshape(B, Tq, NSA_KV_HEADS, HEADS_PER_KV, HEAD_DIM).transpose(0, 2, 1, 3, 4)
    n_cmp, n_sel = kc.shape[2], ksb.shape[2]
    cmp_end = jnp.arange(n_cmp, dtype=jnp.int32) * CMP_STRIDE + CMP_BLOCK - 1
    cmask = (cmp_end[None, :] <= qpos[:, None])[None, None, :, None, :]
    pc = masked_softmax(jnp.einsum('bgqhd,bgnd->bgqhn', qg, kc) * scale, cmask)
    o_cmp = jnp.einsum('bgqhn,bgnd->bgqhd', pc.astype(vc.dtype), vc)
    imp = jnp.einsum('bgqhn,ns->bgqs', pc, cmp_to_sel(n_cmp, n_sel))
    blk = jnp.arange(n_sel, dtype=jnp.int32)[None, :]
    cur = (qpos // SEL_BLOCK)[:, None]
    forced = (blk == 0) | (blk == cur) | (blk == cur - 1)
    valid = blk * SEL_BLOCK <= qpos[:, None]
    score = jnp.where(valid, jnp.where(forced, FORCED_SCORE, imp), NEG)
    top_val, top_idx = lax.top_k(score, min(SEL_TOPN, n_sel))
    chosen = top_val > 0.5 * NEG
    gather = jax.vmap(jax.vmap(lambda blocks, idx: blocks[idx]))
    ks = gather(ksb, top_idx)
    vs = gather(vsb, top_idx)
    tok = top_idx[..., None] * SEL_BLOCK + jnp.arange(SEL_BLOCK, dtype=jnp.int32)
    smask = chosen[..., None] & (tok <= qpos[None, None, :, None, None])
    n_top = top_idx.shape[-1]
    ss = (jnp.einsum('bgqhd,bgqnld->bgqhnl', qg, ks) * scale).reshape(B, NSA_KV_HEADS, Tq, HEADS_PER_KV, n_top * SEL_BLOCK)
    ps = masked_softmax(ss, smask.reshape(B, NSA_KV_HEADS, Tq, 1, n_top * SEL_BLOCK))
    ps = ps.reshape(B, NSA_KV_HEADS, Tq, HEADS_PER_KV, n_top, SEL_BLOCK)
    o_sel = jnp.einsum('bgqhnl,bgqnld->bgqhd', ps.astype(vs.dtype), vs)
    to_heads = lambda o: o.transpose(0, 2, 1, 3, 4).reshape(B, Tq, NSA_HEADS, HEAD_DIM)
    return to_heads(o_cmp), to_heads(o_sel)


def window_attend(q, qpos, kw, vw, kpos):
    B, Tq = q.shape[:2]
    qg = q.reshape(B, Tq, NSA_KV_HEADS, HEADS_PER_KV, HEAD_DIM)
    s = jnp.einsum('bqghd,bkgd->bgqhk', qg, kw) * (HEAD_DIM ** -0.5)
    rel = qpos[:, None] - kpos[None, :]
    mask = ((rel >= 0) & (rel < WINDOW) & (kpos[None, :] >= 0))[None, None, :, None, :]
    p = masked_softmax(s, mask)
    o = jnp.einsum('bgqhk,bkgd->bqghd', p.astype(vw.dtype), vw)
    return o.reshape(B, Tq, NSA_HEADS, HEAD_DIM)


def combine_branches(o_cmp, o_sel, o_win, g):
    return g[..., 0:1] * o_cmp + g[..., 1:2] * o_sel + g[..., 2:3] * o_win


def nsa_prompt(q, kv_rows, win_rows, gates, lp):
    B, T = q.shape[:2]
    kc = compress(kv_rows[:, :, 0], lp["cmp_pe"][0], lp["cmp_w1"][0], lp["cmp_w2"][0])
    vc = compress(kv_rows[:, :, 1], lp["cmp_pe"][1], lp["cmp_w1"][1], lp["cmp_w2"][1])
    n_sel = T // SEL_BLOCK
    to_blocks = lambda r: r.reshape(B, n_sel, SEL_BLOCK, NSA_KV_HEADS, HEAD_DIM).transpose(0, 3, 1, 2, 4)
    ksb, vsb = to_blocks(kv_rows[:, :, 2]), to_blocks(kv_rows[:, :, 3])
    pad = ((0, 0), (WINDOW, 0), (0, 0), (0, 0))
    kw_pad = jnp.pad(win_rows[:, :, 0], pad)
    vw_pad = jnp.pad(win_rows[:, :, 1], pad)

    def one_block(i):
        qs = i * Q_BLOCK
        qb = lax.dynamic_slice_in_dim(q, qs, Q_BLOCK, axis=1)
        gb = lax.dynamic_slice_in_dim(gates, qs, Q_BLOCK, axis=1)
        qpos = qs + jnp.arange(Q_BLOCK, dtype=jnp.int32)
        o_cmp, o_sel = cmp_sel_attend(qb, qpos, kc, vc, ksb, vsb)
        kw = lax.dynamic_slice_in_dim(kw_pad, qs, Q_BLOCK + WINDOW, axis=1)
        vw = lax.dynamic_slice_in_dim(vw_pad, qs, Q_BLOCK + WINDOW, axis=1)
        kpos = qs - WINDOW + jnp.arange(Q_BLOCK + WINDOW, dtype=jnp.int32)
        o_win = window_attend(qb, qpos, kw, vw, kpos)
        return combine_branches(o_cmp, o_sel, o_win, gb).astype(q.dtype)

    o = lax.map(one_block, jnp.arange(T // Q_BLOCK, dtype=jnp.int32))
    o = jnp.moveaxis(o, 0, 1).reshape(B, T, NSA_WIDTH)
    return o, win_rows[:, T - min(WINDOW, T):]


def nsa_sample(q, kv_rows, win_rows, gates, lp, past_kv, win_buf):
    B, Tn = q.shape[:2]
    past_len = past_kv.shape[1]
    full = lambda c: jnp.concatenate([past_kv[:, :, c].astype(kv_rows.dtype), kv_rows[:, :, c]], axis=1)
    kc = compress(full(0), lp["cmp_pe"][0], lp["cmp_w1"][0], lp["cmp_w2"][0])
    vc = compress(full(1), lp["cmp_pe"][1], lp["cmp_w1"][1], lp["cmp_w2"][1])
    L = past_len + Tn
    n_sel = -(-L // SEL_BLOCK)
    pad = ((0, 0), (0, n_sel * SEL_BLOCK - L), (0, 0), (0, 0))
    to_blocks = lambda r: jnp.pad(r, pad).reshape(B, n_sel, SEL_BLOCK, NSA_KV_HEADS, HEAD_DIM).transpose(0, 3, 1, 2, 4)
    ksb, vsb = to_blocks(full(2)), to_blocks(full(3))
    qpos = past_len + jnp.arange(Tn, dtype=jnp.int32)
    o_cmp, o_sel = cmp_sel_attend(q, qpos, kc, vc, ksb, vsb)
    wb = win_buf.shape[1]
    win_all = jnp.concatenate([win_buf.astype(win_rows.dtype), win_rows], axis=1)
    kpos = past_len - wb + jnp.arange(wb + Tn, dtype=jnp.int32)
    o_win = window_attend(q, qpos, win_all[:, :, 0], win_all[:, :, 1], kpos)
    o = combine_branches(o_cmp, o_sel, o_win, gates).astype(q.dtype)
    return o.reshape(B, Tn, NSA_WIDTH), win_all[:, Tn:]


def wkv_scan(r, decay, k, v, a, b, s0):
    def step(s, inp):
        r_t, w_t, k_t, v_t, a_t, b_t = inp
        sa = jnp.einsum('bhvk,bhk->bhv', s, a_t)
        s = s * w_t[:, :, None, :] + sa[..., None] * b_t[:, :, None, :] + v_t[..., None] * k_t[:, :, None, :]
        return s, jnp.einsum('bhvk,bhk->bhv', s, r_t)
    xs = tuple(jnp.moveaxis(t, 1, 0) for t in (r, decay, k, v, a, b))
    s, y = lax.scan(step, s0, xs)
    return jnp.moveaxis(y, 0, 1), s


def rwkv_time_mix(proj, prev, s0, lp):
    B, T, _ = proj.shape
    f32 = jnp.float32
    p = proj.astype(f32)
    shifted = jnp.concatenate([prev.astype(f32)[:, None], p[:, :-1]], axis=1)
    xs = p + (shifted - p) * lp["rwkv_mu"].astype(f32)
    o1, o2, o3 = RWKV_WIDTH, 2 * RWKV_WIDTH, 3 * RWKV_WIDTH
    o4, o5 = o3 + DECAY_LORA, o3 + DECAY_LORA + AAA_LORA
    r, k, v = xs[..., :o1], xs[..., o1:o2], xs[..., o2:o3]
    wl, al, gl = xs[..., o3:o4], xs[..., o4:o5], xs[..., o5:]
    w = -jax.nn.softplus(-(lp["rwkv_w0"].astype(f32) + jnp.tanh(wl) @ lp["rwkv_w_up"].astype(f32))) - 0.5
    decay = jnp.exp(-jnp.exp(w))
    a = jax.nn.sigmoid(lp["rwkv_a0"].astype(f32) + al @ lp["rwkv_a_up"].astype(f32))
    g = jax.nn.sigmoid(gl) @ lp["rwkv_g_up"].astype(f32)
    heads = lambda t: t.reshape(B, T, RWKV_HEADS, RWKV_HEAD)
    kk = heads(k * lp["rwkv_k_k"].astype(f32))
    kk = kk / jnp.maximum(jnp.sqrt(jnp.sum(kk * kk, axis=-1, keepdims=True)), 1e-12)
    k = k * (1.0 + (a - 1.0) * lp["rwkv_k_a"].astype(f32))
    rh, kh, vh, ah = heads(r), heads(k), heads(v), heads(a)
    y, s = wkv_scan(rh, heads(decay), kh, vh, -kk, kk * ah, s0.astype(f32))
    mu = jnp.mean(y, axis=-1, keepdims=True)
    var = jnp.mean(jnp.square(y - mu), axis=-1, keepdims=True)
    lnx_w = lp["rwkv_lnx_w"].astype(f32).reshape(RWKV_HEADS, RWKV_HEAD)
    lnx_b = lp["rwkv_lnx_b"].astype(f32).reshape(RWKV_HEADS, RWKV_HEAD)
    y = (y - mu) * lax.rsqrt(var + LNX_EPS) * lnx_w + lnx_b
    y = y + jnp.sum(rh * kh * lp["rwkv_r_k"].astype(f32), axis=-1, keepdims=True) * vh
    out = y.reshape(B, T, RWKV_WIDTH) * g
    return out.astype(proj.dtype), proj[:, -1], s


def conv_ffn(x, prev, lp):
    T = x.shape[1]
    u = x @ lp["ffn_w_up"]
    gate, val = u[..., :D_FF], u[..., D_FF:]
    ext = jnp.concatenate([prev.astype(gate.dtype), gate], axis=1)
    conv = lp["ffn_conv_b"] + sum(ext[:, j:j + T] * lp["ffn_conv_w"][j] for j in range(CONV_WIDTH))
    h = jax.nn.gelu(conv) * val
    return h @ lp["ffn_w_down"], ext[:, -(CONV_WIDTH - 1):]


def layer_step(x, pos, nsa_fn, shift_prev, wkv_prev, conv_prev, lp):
    h = rms_norm(x, lp["norm_mix_pre"])
    proj = h @ lp["w_in"]
    q, kv_rows, win_rows, gates = nsa_project(proj[..., :NSA_PROJ], pos)
    o_nsa, win_state = nsa_fn(q, kv_rows, win_rows, gates)
    o_rwkv, shift_new, wkv_new = rwkv_time_mix(proj[..., NSA_PROJ:], shift_prev, wkv_prev, lp)
    mixed = jnp.concatenate([o_nsa.astype(x.dtype), o_rwkv.astype(x.dtype)], axis=-1) @ lp["w_out"]
    x = x + rms_norm(mixed, lp["norm_mix_post"])
    f, conv_new = conv_ffn(rms_norm(x, lp["norm_ffn_pre"]), conv_prev, lp)
    x = x + rms_norm(f, lp["norm_ffn_post"])
    return x, kv_rows, win_state, wkv_new, shift_new, conv_new


def setup_inputs(seed: int = 0) -> dict:
    key = jax.random.key(seed)
    ks = iter(jax.random.split(key, 40))
    f32 = jnp.float32
    nrm = lambda shape, scale: scale * jax.random.normal(next(ks), shape, f32)
    n_pages = PAST_LEN // PAGE_SIZE
    n_used = DEC_BATCH * n_pages
    n_pool = n_used + max(1, n_used // 4)
    wb = min(WINDOW, PAST_LEN)
    return {
        "x_prompt": nrm((BATCH, SEQ, D_MODEL), 1.0),
        "x_sample": nrm((DEC_BATCH, DEC_SEQ, D_MODEL), 1.0),
        "cache_kv": nrm((DEPTH, n_pool, PAGE_SIZE, 4, NSA_KV_HEADS, HEAD_DIM), 1.0),
        "page_table": jax.random.permutation(next(ks), n_pool)[:n_used].reshape(DEC_BATCH, n_pages).astype(jnp.int32),
        "state_win": nrm((DEPTH, DEC_BATCH, wb, 2, NSA_KV_HEADS, HEAD_DIM), 1.0),
        "state_wkv": nrm((DEPTH, DEC_BATCH, RWKV_HEADS, RWKV_HEAD, RWKV_HEAD), 0.5),
        "state_shift": nrm((DEPTH, DEC_BATCH, RWKV_PROJ), 1.0),
        "state_conv": nrm((DEPTH, DEC_BATCH, CONV_WIDTH - 1, D_FF), 1.0),
        "norm_mix_pre": 1.0 + nrm((DEPTH, D_MODEL), 0.05),
        "norm_mix_post": 1.0 + nrm((DEPTH, D_MODEL), 0.05),
        "norm_ffn_pre": 1.0 + nrm((DEPTH, D_MODEL), 0.05),
        "norm_ffn_post": 1.0 + nrm((DEPTH, D_MODEL), 0.05),
        "w_in": nrm((DEPTH, D_MODEL, IN_PROJ), D_MODEL ** -0.5),
        "w_out": nrm((DEPTH, MIX_WIDTH, D_MODEL), MIX_WIDTH ** -0.5),
        "cmp_pe": nrm((DEPTH, 2, CMP_BLOCK, HEAD_DIM), 0.1),
        "cmp_w1": nrm((DEPTH, 2, CMP_BLOCK, HEAD_DIM, CMP_HIDDEN), (CMP_BLOCK * HEAD_DIM) ** -0.5),
        "cmp_w2": nrm((DEPTH, 2, CMP_HIDDEN, HEAD_DIM), 1.5 * CMP_HIDDEN ** -0.5),
        "rwkv_mu": jax.random.uniform(next(ks), (DEPTH, RWKV_PROJ), f32, 0.05, 0.95),
        "rwkv_w0": jnp.linspace(-6.5, -1.5, RWKV_WIDTH, dtype=f32)[None, :] + nrm((DEPTH, RWKV_WIDTH), 0.1),
        "rwkv_w_up": nrm((DEPTH, DECAY_LORA, RWKV_WIDTH), 0.1),
        "rwkv_a0": nrm((DEPTH, RWKV_WIDTH), 0.5),
        "rwkv_a_up": nrm((DEPTH, AAA_LORA, RWKV_WIDTH), 0.1),
        "rwkv_g_up": nrm((DEPTH, GATE_LORA, RWKV_WIDTH), GATE_LORA ** -0.5),
        "rwkv_k_k": 0.85 + nrm((DEPTH, RWKV_WIDTH), 0.05),
        "rwkv_k_a": 1.0 + nrm((DEPTH, RWKV_WIDTH), 0.05),
        "rwkv_r_k": nrm((DEPTH, RWKV_HEADS, RWKV_HEAD), 0.1),
        "rwkv_lnx_w": 1.0 + nrm((DEPTH, RWKV_WIDTH), 0.05),
        "rwkv_lnx_b": nrm((DEPTH, RWKV_WIDTH), 0.02),
        "ffn_w_up": nrm((DEPTH, D_MODEL, 2 * D_FF), D_MODEL ** -0.5),
        "ffn_conv_w": nrm((DEPTH, CONV_WIDTH, D_FF), 0.5),
        "ffn_conv_b": nrm((DEPTH, D_FF), 0.02),
        "ffn_w_down": nrm((DEPTH, D_FF, D_MODEL), D_FF ** -0.5),
    }


def reference(x_prompt, x_sample, cache_kv, page_table, state_win, state_wkv, state_shift, state_conv,
              norm_mix_pre, norm_mix_post, norm_ffn_pre, norm_ffn_post, w_in, w_out,
              cmp_pe, cmp_w1, cmp_w2, rwkv_mu, rwkv_w0, rwkv_w_up, rwkv_a0, rwkv_a_up, rwkv_g_up,
              rwkv_k_k, rwkv_k_a, rwkv_r_k, rwkv_lnx_w, rwkv_lnx_b,
              ffn_w_up, ffn_conv_w, ffn_conv_b, ffn_w_down):
    B, T, _ = x_prompt.shape
    DB, TN, _ = x_sample.shape
    pos_p = jnp.arange(T, dtype=jnp.int32)
    pos_s = PAST_LEN + jnp.arange(TN, dtype=jnp.int32)
    xp, xs = x_prompt, x_sample
    outs_p, outs_s = [], []
    for l in range(DEPTH):
        lp = {
            "norm_mix_pre": norm_mix_pre[l], "norm_mix_post": norm_mix_post[l],
            "norm_ffn_pre": norm_ffn_pre[l], "norm_ffn_post": norm_ffn_post[l],
            "w_in": w_in[l], "w_out": w_out[l],
            "cmp_pe": cmp_pe[l], "cmp_w1": cmp_w1[l], "cmp_w2": cmp_w2[l],
            "rwkv_mu": rwkv_mu[l], "rwkv_w0": rwkv_w0[l], "rwkv_w_up": rwkv_w_up[l],
            "rwkv_a0": rwkv_a0[l], "rwkv_a_up": rwkv_a_up[l], "rwkv_g_up": rwkv_g_up[l],
            "rwkv_k_k": rwkv_k_k[l], "rwkv_k_a": rwkv_k_a[l], "rwkv_r_k": rwkv_r_k[l],
            "rwkv_lnx_w": rwkv_lnx_w[l], "rwkv_lnx_b": rwkv_lnx_b[l],
            "ffn_w_up": ffn_w_up[l], "ffn_conv_w": ffn_conv_w[l], "ffn_conv_b": ffn_conv_b[l],
            "ffn_w_down": ffn_w_down[l],
        }
        xp, *st_p = layer_step(
            xp, pos_p, lambda q, kv, w, g: nsa_prompt(q, kv, w, g, lp),
            jnp.zeros((B, RWKV_PROJ), xp.dtype),
            jnp.zeros((B, RWKV_HEADS, RWKV_HEAD, RWKV_HEAD), jnp.float32),
            jnp.zeros((B, CONV_WIDTH - 1, D_FF), xp.dtype), lp)
        past = cache_kv[l, page_table].reshape(DB, -1, 4, NSA_KV_HEADS, HEAD_DIM)
        buf = state_win[l]
        xs, *st_s = layer_step(
            xs, pos_s, lambda q, kv, w, g: nsa_sample(q, kv, w, g, lp, past, buf),
            state_shift[l], state_wkv[l], state_conv[l], lp)
        outs_p.append(st_p)
        outs_s.append(st_s)
    stack = lambda outs, i: jnp.stack([o[i] for o in outs])
    return (xp, xs, stack(outs_p, 0), stack(outs_s, 0), stack(outs_p, 1), stack(outs_s, 1),
            stack(outs_p, 2), stack(outs_s, 2), stack(outs_p, 3), stack(outs_s, 3),
            stack(outs_p, 4), stack(outs_s, 4))
```

```python
import functools

import jax
import jax.numpy as jnp
import numpy as np
from jax import lax
from jax.experimental import pallas as pl
from jax.experimental.pallas import tpu as pltpu

F32, BF16 = jnp.float32, jnp.bfloat16
HIGHEST = lax.Precision.HIGHEST

LANES = 128
VMEM_LIMIT = 56 * 1024 * 1024

HEAD_DIM = 64
NSA_HEADS = 8
KV_HEADS = 2
HEADS_PER_KV = NSA_HEADS // KV_HEADS
NSA_WIDTH = NSA_HEADS * HEAD_DIM
KV_WIDTH = KV_HEADS * HEAD_DIM
N_BRANCH = 3
CMP_STRIDE = 16
CMP_BLOCK = 2 * CMP_STRIDE
SEL_BLOCK = 64
SEL_TOPN = 16
WINDOW = 512
ROPE_THETA = 500000.0
ROPE_DIM = HEAD_DIM // 4
RWKV_HEAD = 64
RWKV_HEADS = 8
RWKV_WIDTH = RWKV_HEADS * RWKV_HEAD
DECAY_LORA = 64
AAA_LORA = 64
GATE_LORA = 128
RWKV_PROJ = 3 * RWKV_WIDTH + DECAY_LORA + AAA_LORA + GATE_LORA
NSA_PROJ = NSA_WIDTH + 6 * KV_WIDTH + N_BRANCH * NSA_HEADS
CONV_WIDTH = 3
NORM_EPS = 1e-6
LNX_EPS = 64e-5
NEG = -1e30
FORCED_SCORE = 1e9
ATTN_SCALE = HEAD_DIM ** -0.5


def _dot(a, b, precision=None):
    return jnp.dot(a, b, preferred_element_type=F32, precision=precision)


def _dot_nt(a, b, precision=None):
    return lax.dot_general(a, b, (((1,), (1,)), ((), ())), preferred_element_type=F32, precision=precision)


def _dot_tn(a, b, precision=None):
    return lax.dot_general(a, b, (((0,), (0,)), ((), ())), preferred_element_type=F32, precision=precision)


def _rms(x, g):
    return x * lax.rsqrt(jnp.mean(x * x, axis=-1, keepdims=True) + NORM_EPS) * g


def _masked_softmax(s, mask):
    s = jnp.where(mask, s, NEG)
    m = jnp.max(s, axis=-1, keepdims=True)
    e = jnp.where(mask, jnp.exp(s - m), 0.0)
    return e / jnp.maximum(jnp.sum(e, axis=-1, keepdims=True), 1e-30)


def _params(*sem):
    return pltpu.CompilerParams(dimension_semantics=sem, vmem_limit_bytes=VMEM_LIMIT)


def _norm_proj_kernel(x_ref, g_ref, wq_ref, wkv_ref, wg_ref, wr_ref, cos_ref, sa_ref, sb_ref,
                      q_ref, kvw_ref, gate_ref, pr_ref):
    h = _rms(x_ref[...], g_ref[...]).astype(BF16)
    cos, sa, sb = cos_ref[...], sa_ref[...], sb_ref[...]

    def rope(v):
        return v * cos + pltpu.roll(v, LANES - ROPE_DIM // 2, 1) * sa + pltpu.roll(v, ROPE_DIM // 2, 1) * sb

    q = _dot(h, wq_ref[...])
    for j in range(NSA_WIDTH // LANES):
        q_ref[:, j * LANES:(j + 1) * LANES] = rope(q[:, j * LANES:(j + 1) * LANES])
    kv = _dot(h, wkv_ref[...])
    for j in range(6 * KV_WIDTH // LANES):
        blk = kv[:, j * LANES:(j + 1) * LANES]
        kvw_ref[:, j * LANES:(j + 1) * LANES] = rope(blk) if j % 2 == 0 else blk
    gate_ref[...] = jax.nn.sigmoid(_dot(h, wg_ref[...]))
    pr_ref[...] = _dot(h, wr_ref[...])


def _norm_proj(x2, g, wq, wkv, wg, wr, tabs, tm):
    m, d = x2.shape
    cos, sa, sb = tabs
    tab_blocks = cos.shape[0] // tm
    row = lambda i: (i, 0)
    full = lambda i: (0, 0)
    tab = lambda i: (i % tab_blocks, 0)
    outs = (jax.ShapeDtypeStruct((m, NSA_WIDTH), F32), jax.ShapeDtypeStruct((m, 6 * KV_WIDTH), F32),
            jax.ShapeDtypeStruct((m, LANES), F32), jax.ShapeDtypeStruct((m, RWKV_PROJ), F32))
    return pl.pallas_call(
        _norm_proj_kernel, grid=(m // tm,),
        in_specs=[pl.BlockSpec((tm, d), row), pl.BlockSpec((1, d), full),
                  pl.BlockSpec(wq.shape, full), pl.BlockSpec(wkv.shape, full),
                  pl.BlockSpec(wg.shape, full), pl.BlockSpec(wr.shape, full),
                  pl.BlockSpec((tm, LANES), tab), pl.BlockSpec((tm, LANES), tab), pl.BlockSpec((tm, LANES), tab)],
        out_specs=[pl.BlockSpec((tm, NSA_WIDTH), row), pl.BlockSpec((tm, 6 * KV_WIDTH), row),
                   pl.BlockSpec((tm, LANES), row), pl.BlockSpec((tm, RWKV_PROJ), row)],
        out_shape=outs, compiler_params=_params("arbitrary"), name="norm_proj",
    )(x2, g, wq, wkv, wg, wr, cos, sa, sb)


def _rope_tables(pos):
    half = ROPE_DIM // 2
    inv_freq = ROPE_THETA ** (-jnp.arange(half, dtype=F32) / half)
    ang = pos.astype(F32)[:, None] * inv_freq[None, :]
    c, s = jnp.cos(ang), jnp.sin(ang)
    n = pos.shape[0]
    one = jnp.ones((n, HEAD_DIM - ROPE_DIM), F32)
    zero = jnp.zeros((n, HEAD_DIM - ROPE_DIM), F32)
    zh = jnp.zeros((n, half), F32)
    cos = jnp.concatenate([c, c, one], axis=1)
    sa = jnp.concatenate([-s, zh, zero], axis=1)
    sb = jnp.concatenate([zh, s, zero], axis=1)
    rep = LANES // HEAD_DIM
    return tuple(jnp.tile(t, (1, rep)) for t in (cos, sa, sb))


CMP_LANES = 2 * KV_WIDTH


def _compress_accumulate(rows_refs, n_chunk, pea_ref, peb_ref, wa_ref, wb_ref, acca_ref, accb_ref, row0):
    for c, rows_ref in enumerate(rows_refs):
        lanes = slice(c * KV_WIDTH, (c + 1) * KV_WIDTH)
        da = db = None
        for l in range(CMP_STRIDE):
            x = rows_ref[pl.ds(l, n_chunk, stride=CMP_STRIDE), :]
            xa = (x + pea_ref[c, l:l + 1, :]).astype(BF16)
            xb = (x + peb_ref[c, l:l + 1, :]).astype(BF16)
            ta, tb = _dot(xa, wa_ref[c, l]), _dot(xb, wb_ref[c, l])
            da, db = (ta, tb) if l == 0 else (da + ta, db + tb)
        acca_ref[pl.ds(row0, n_chunk), lanes] = da
        accb_ref[pl.ds(row0, n_chunk), lanes] = db


def _compress_finish(acca_ref, accb_ref, w2_ref, out_ref):
    n = acca_ref.shape[0]
    rows = lax.broadcasted_iota(jnp.int32, (n, 1), 0)
    for c in range(2):
        lanes = slice(c * KV_WIDTH, (c + 1) * KV_WIDTH)
        nxt = pltpu.roll(accb_ref[:, lanes], n - 1, 0)
        h = jax.nn.gelu(acca_ref[:, lanes] + nxt).astype(BF16)
        out_ref[0, :, lanes] = jnp.where(rows < n - 1, _dot(h, w2_ref[c]), 0.0)


def _compress_prompt_kernel(k_ref, v_ref, pea_ref, peb_ref, wa_ref, wb_ref, w2_ref, out_ref, acca_ref, accb_ref):
    n = acca_ref.shape[0]
    _compress_accumulate((k_ref.at[0], v_ref.at[0]), n, pea_ref, peb_ref, wa_ref, wb_ref, acca_ref, accb_ref, 0)
    _compress_finish(acca_ref, accb_ref, w2_ref, out_ref)


_CW_NAMES = ("pea", "peb", "wa", "wb", "w2")


def _compress_prompt(kvw3, cw):
    b, t, _ = kvw3.shape
    n = t // CMP_STRIDE
    const = lambda a: pl.BlockSpec(a.shape, lambda i: (0,) * a.ndim)
    return pl.pallas_call(
        _compress_prompt_kernel, grid=(b,),
        in_specs=[pl.BlockSpec((1, t, KV_WIDTH), lambda i: (i, 0, 0)),
                  pl.BlockSpec((1, t, KV_WIDTH), lambda i: (i, 0, 1))] + [const(cw[k]) for k in _CW_NAMES],
        out_specs=pl.BlockSpec((1, n, CMP_LANES), lambda i: (i, 0, 0)),
        out_shape=jax.ShapeDtypeStruct((b, n, CMP_LANES), F32),
        scratch_shapes=[pltpu.VMEM((n, CMP_LANES), F32), pltpu.VMEM((n, CMP_LANES), F32)],
        compiler_params=_params("arbitrary"), name="compress_prompt",
    )(kvw3, kvw3, *[cw[k] for k in _CW_NAMES])


def _compress_weights(pe, w1, w2):
    eye = jnp.eye(KV_HEADS, dtype=F32)
    blockdiag = lambda w: jnp.einsum('...de,gj->...gdje', w, eye).reshape(w.shape[:-2] + (KV_WIDTH, KV_WIDTH))
    lanes = lambda p: jnp.tile(p, (1, 1, KV_HEADS))
    return {"pea": lanes(pe[:, :CMP_STRIDE]), "peb": lanes(pe[:, CMP_STRIDE:]),
            "wa": blockdiag(w1[:, :CMP_STRIDE]).astype(BF16), "wb": blockdiag(w1[:, CMP_STRIDE:]).astype(BF16),
            "w2": blockdiag(w2).astype(BF16)}


def _paged_fetch(page_copies, pages_per_step, steps_per_seq):
    step = pl.program_id(0) * steps_per_seq + pl.program_id(1)
    total = pl.num_programs(0) * steps_per_seq

    def start_all(s):
        def body(j, c):
            for cp in page_copies(s, j):
                cp.start()
            return c
        lax.fori_loop(0, pages_per_step, body, 0)

    @pl.when(step == 0)
    def _():
        start_all(step)

    @pl.when(step + 1 < total)
    def _():
        start_all(step + 1)

    def wait_body(j, c):
        for cp in page_copies(step, j):
            cp.wait()
        return c
    lax.fori_loop(0, pages_per_step, wait_body, 0)
    return step % 2


def _page_of(pt_ref, step, j, pages_per_step, steps_per_seq):
    return pt_ref[step // steps_per_seq, (step % steps_per_seq) * pages_per_step + j]


def _compress_sample_kernel(layer, pages_per_step, steps_per_seq, pt_ref, cache_ref, pea_ref, peb_ref, wa_ref,
                            wb_ref, w2_ref, out_ref, buf_ref, sem_ref, acca_ref, accb_ref):
    rows = cache_ref.shape[2]

    def page_copies(step, j):
        page = _page_of(pt_ref, step, j, pages_per_step, steps_per_seq)
        slot = step % 2
        return [pltpu.make_async_copy(cache_ref.at[layer, page, :, pl.ds(c * KV_WIDTH, KV_WIDTH)],
                                      buf_ref.at[slot, c, pl.ds(j * rows, rows), :], sem_ref.at[slot])
                for c in range(2)]

    slot = _paged_fetch(page_copies, pages_per_step, steps_per_seq)
    part = pl.program_id(1)
    n = buf_ref.shape[2] // CMP_STRIDE
    row0 = pl.multiple_of(part * n, n)
    _compress_accumulate((buf_ref.at[slot, 0], buf_ref.at[slot, 1]), n, pea_ref, peb_ref, wa_ref, wb_ref,
                         acca_ref, accb_ref, row0)

    @pl.when(part == steps_per_seq - 1)
    def _():
        _compress_finish(acca_ref, accb_ref, w2_ref, out_ref)


def _compress_sample(cache4, page_table, layer, cw, steps_per_seq=2):
    db, n_pages = page_table.shape
    page = cache4.shape[2]
    pps = n_pages // steps_per_seq
    n = n_pages * page // CMP_STRIDE
    const = lambda a: pl.BlockSpec(a.shape, lambda i, p, pt: (0,) * a.ndim)
    kern = functools.partial(_compress_sample_kernel, layer, pps, steps_per_seq)
    return pl.pallas_call(
        kern,
        grid_spec=pltpu.PrefetchScalarGridSpec(
            num_scalar_prefetch=1, grid=(db, steps_per_seq),
            in_specs=[pl.BlockSpec(memory_space=pl.ANY)] + [const(cw[k]) for k in _CW_NAMES],
            out_specs=pl.BlockSpec((1, n, CMP_LANES), lambda i, p, pt: (i, 0, 0)),
            scratch_shapes=[pltpu.VMEM((2, 2, pps * page, KV_WIDTH), F32), pltpu.SemaphoreType.DMA((2,)),
                            pltpu.VMEM((n, CMP_LANES), F32), pltpu.VMEM((n, CMP_LANES), F32)]),
        out_shape=jax.ShapeDtypeStruct((db, n, CMP_LANES), F32),
        compiler_params=_params("arbitrary", "arbitrary"), name="compress_sample",
    )(page_table, cache4, *[cw[k] for k in _CW_NAMES])


def _cmp_to_sel(n_cmp_pad, n_cmp, n_sel_pad, n_sel):
    c0 = np.arange(n_cmp_pad)[:, None] * CMP_STRIDE
    s0 = np.arange(n_sel_pad)[None, :] * SEL_BLOCK
    hit = (c0 < s0 + SEL_BLOCK) & (c0 + CMP_BLOCK > s0)
    hit &= (np.arange(n_cmp_pad)[:, None] < n_cmp) & (np.arange(n_sel_pad)[None, :] < n_sel)
    return hit.astype(np.float32)


def _block_to_key(n_blk, n_key):
    return (np.arange(n_key)[None, :] // SEL_BLOCK == np.arange(n_blk)[:, None]).astype(np.float32)


def _nsa_prompt_kernel(qb, tk, q_ref, kc_ref, vc_ref, ks_ref, vs_ref, kw_ref, vw_ref, gate_ref, c2s_ref, b2k_ref,
                       o_ref):
    i = pl.program_id(2)
    qs = i * qb
    rows = HEADS_PER_KV * qb
    n_cmp = kc_ref.shape[0]
    n_sel = c2s_ref.shape[0]
    q = q_ref[...].reshape(rows, HEAD_DIM)
    qpos = qs + lax.broadcasted_iota(jnp.int32, (rows, 1), 0) % qb

    s = _dot_nt(q, kc_ref[...])
    cmp_end = lax.broadcasted_iota(jnp.int32, (1, n_cmp), 1) * CMP_STRIDE + (CMP_BLOCK - 1)
    p = _masked_softmax(s, cmp_end <= qpos)
    o_cmp = _dot(p.astype(BF16), vc_ref[...])
    psum = p[0:qb]
    for h in range(1, HEADS_PER_KV):
        psum = psum + p[h * qb:(h + 1) * qb]
    imp_t = _dot_nt(c2s_ref[...], psum, precision=HIGHEST)

    blk = lax.broadcasted_iota(jnp.int32, (n_sel, qb), 0)
    qp = qs + lax.broadcasted_iota(jnp.int32, (n_sel, qb), 1)
    cur = qp // SEL_BLOCK
    forced = (blk == 0) | (blk == cur) | (blk == cur - 1)
    valid = blk <= cur
    score = jnp.where(valid, jnp.where(forced, FORCED_SCORE, imp_t), NEG)
    cnt = jnp.zeros((n_sel, qb), jnp.int32)
    for j in range(n_sel):
        sj = score[j:j + 1, :]
        ahead = (sj > score) | ((sj == score) & (j < blk))
        cnt = cnt + ahead.astype(jnp.int32)
    sel_t = jnp.where((cnt < SEL_TOPN) & valid, 1.0, 0.0).astype(BF16)

    def sel_step(kt, carry):
        m, l, acc = carry
        k0 = pl.multiple_of(kt * tk, tk)
        s = _dot_nt(q, ks_ref[pl.ds(k0, tk), :])
        em = _dot_tn(sel_t, b2k_ref[kt])
        em = jnp.concatenate([em] * HEADS_PER_KV, axis=0)
        kpos = k0 + lax.broadcasted_iota(jnp.int32, (1, tk), 1)
        msk = (em > 0.5) & (kpos <= qpos)
        s = jnp.where(msk, s, NEG)
        m_new = jnp.maximum(m, jnp.max(s, axis=-1, keepdims=True))
        alpha = jnp.exp(m - m_new)
        e = jnp.where(msk, jnp.exp(s - m_new), 0.0)
        l = alpha * l + jnp.sum(e, axis=-1, keepdims=True)
        acc = alpha * acc + _dot(e.astype(BF16), vs_ref[pl.ds(k0, tk), :])
        return m_new, l, acc

    n_kt = (qs + qb + tk - 1) // tk
    init = (jnp.full((rows, 1), NEG, F32), jnp.zeros((rows, 1), F32), jnp.zeros((rows, HEAD_DIM), F32))
    _, l, acc = lax.fori_loop(0, n_kt, sel_step, init)
    o_sel = acc / jnp.maximum(l, 1e-30)

    w0 = pl.multiple_of(jnp.maximum(qs - WINDOW, 0), qb)
    s = _dot_nt(q, kw_ref[pl.ds(w0, WINDOW + qb), :])
    rel = qpos - (w0 + lax.broadcasted_iota(jnp.int32, (1, WINDOW + qb), 1))
    p = _masked_softmax(s, (rel >= 0) & (rel < WINDOW))
    o_win = _dot(p.astype(BF16), vw_ref[pl.ds(w0, WINDOW + qb), :])

    g = gate_ref[...].reshape(rows, N_BRANCH)
    o = g[:, 0:1] * o_cmp + g[:, 1:2] * o_sel + g[:, 2:3] * o_win
    o_ref[...] = o.astype(o_ref.dtype).reshape(HEADS_PER_KV, qb, HEAD_DIM)


def _nsa_prompt(qh, kvh, cvh, gates_h, qb=64, tk=512):
    b, _, t, _ = qh.shape
    n_cmp = cvh.shape[3]
    n_sel = t // SEL_BLOCK
    c2s = jnp.asarray(_cmp_to_sel(n_cmp, n_cmp - 1, n_sel, n_sel).T)
    b2k = jnp.asarray(_block_to_key(n_sel, t).reshape(n_sel, t // tk, tk).transpose(1, 0, 2), dtype=BF16)
    comp = lambda c: pl.BlockSpec((None, None, None, t, HEAD_DIM), lambda bi, g, i: (bi, c, g, 0, 0))
    ccomp = lambda c: pl.BlockSpec((None, None, None, n_cmp, HEAD_DIM), lambda bi, g, i: (bi, c, g, 0, 0))
    qspec = pl.BlockSpec((None, HEADS_PER_KV, qb, HEAD_DIM), lambda bi, g, i: (bi, g, i, 0))
    kern = functools.partial(_nsa_prompt_kernel, qb, tk)
    return pl.pallas_call(
        kern, grid=(b, KV_HEADS, t // qb),
        in_specs=[qspec, ccomp(0), ccomp(1), comp(0), comp(1), comp(2), comp(3),
                  pl.BlockSpec((None, HEADS_PER_KV, qb, N_BRANCH), lambda bi, g, i: (bi, g, i, 0)),
                  pl.BlockSpec(c2s.shape, lambda bi, g, i: (0, 0)),
                  pl.BlockSpec(b2k.shape, lambda bi, g, i: (0, 0, 0))],
        out_specs=qspec,
        out_shape=jax.ShapeDtypeStruct(qh.shape, BF16),
        compiler_params=_params("arbitrary", "arbitrary", "arbitrary"), name="nsa_prompt",
    )(qh, cvh, cvh, kvh, kvh, kvh, kvh, gates_h, c2s, b2k)


SROWS = 8 * KV_HEADS


def _sample_cmp_win_kernel(past_len, n_cmp, qz_ref, kvc_ref, win_ref, wnew_ref, c2s_ref, ocmp_ref, owin_ref,
                           imp_ref):
    qz = qz_ref[0]
    kvc = kvc_ref[0]
    n_pad = kvc.shape[0]
    kc, vc = kvc[:, :KV_WIDTH].astype(BF16), kvc[:, KV_WIDTH:].astype(BF16)
    s = _dot_nt(qz, kc)
    n = lax.broadcasted_iota(jnp.int32, (1, n_pad), 1)
    p = _masked_softmax(s, (n < n_cmp) & (n * CMP_STRIDE + (CMP_BLOCK - 1) <= past_len))
    ocmp_ref[0] = _dot(p.astype(BF16), vc)
    r = lax.broadcasted_iota(jnp.int32, (SROWS, 1), 0)
    p = jnp.where(r % 8 < HEADS_PER_KV, p, 0.0)
    psum = jnp.concatenate([jnp.sum(p[8 * g:8 * g + 8], axis=0, keepdims=True) for g in range(KV_HEADS)] +
                           [jnp.zeros((8 - KV_HEADS, n_pad), F32)], axis=0)
    imp_ref[0] = _dot(psum, c2s_ref[...], precision=HIGHEST)

    win = win_ref[0]
    wb = win.shape[0]
    kw, vw = win[:, :KV_WIDTH].astype(BF16), win[:, KV_WIDTH:].astype(BF16)
    wnew = wnew_ref[0]
    s = _dot_nt(qz, kw)
    s_new = jnp.sum(qz.astype(F32) * wnew[:, :KV_WIDTH].astype(BF16).astype(F32), axis=-1, keepdims=True)
    rel = wb - lax.broadcasted_iota(jnp.int32, (1, wb), 1)
    msk = (rel >= 0) & (rel < WINDOW)
    s = jnp.where(msk, s, NEG)
    m = jnp.maximum(jnp.max(s, axis=-1, keepdims=True), s_new)
    e = jnp.where(msk, jnp.exp(s - m), 0.0)
    e_new = jnp.exp(s_new - m)
    den = jnp.maximum(jnp.sum(e, axis=-1, keepdims=True) + e_new, 1e-30)
    pv = _dot((e / den).astype(BF16), vw)
    pn = (e_new / den).astype(BF16).astype(F32)
    owin_ref[0] = pv + pn * wnew[:, KV_WIDTH:].astype(BF16).astype(F32)


def _sample_cmp_win(past_len, n_cmp, qz, kvc, win, wnew, c2s):
    db = qz.shape[0]
    blk3 = lambda a: pl.BlockSpec((1,) + a.shape[1:], lambda i: (i, 0, 0))
    kern = functools.partial(_sample_cmp_win_kernel, past_len, n_cmp)
    outs = (jax.ShapeDtypeStruct((db, SROWS, KV_WIDTH), F32), jax.ShapeDtypeStruct((db, SROWS, KV_WIDTH), F32),
            jax.ShapeDtypeStruct((db, 8, c2s.shape[1]), F32))
    return pl.pallas_call(
        kern, grid=(db,),
        in_specs=[blk3(qz), blk3(kvc), blk3(win), blk3(wnew), pl.BlockSpec(c2s.shape, lambda i: (0, 0))],
        out_specs=[pl.BlockSpec((1, SROWS, KV_WIDTH), lambda i: (i, 0, 0)),
                   pl.BlockSpec((1, SROWS, KV_WIDTH), lambda i: (i, 0, 0)),
                   pl.BlockSpec((1, 8, c2s.shape[1]), lambda i: (i, 0, 0))],
        out_shape=outs, compiler_params=_params("arbitrary"), name="sample_cmp_win",
    )(qz, kvc, win, wnew, c2s)


def _select_kernel(n_sel, cur, imp_ref, sel_ref):
    imp = imp_ref[...]
    blk = lax.broadcasted_iota(jnp.int32, imp.shape, 1)
    forced = (blk == 0) | (blk == cur) | (blk == cur - 1)
    valid = blk <= cur
    score = jnp.where(valid, jnp.where(forced, FORCED_SCORE, imp), NEG)

    def body(j, cnt):
        sj = jnp.sum(jnp.where(blk == j, score, 0.0), axis=-1, keepdims=True)
        ahead = (sj > score) | ((sj == score) & (j < blk))
        return cnt + ahead.astype(jnp.int32)

    cnt = lax.fori_loop(0, n_sel, body, jnp.zeros(imp.shape, jnp.int32))
    sel_ref[...] = jnp.where((cnt < SEL_TOPN) & valid, 1.0, 0.0)


def _select(imp2, n_sel, cur):
    return pl.pallas_call(
        functools.partial(_select_kernel, n_sel, cur),
        out_shape=jax.ShapeDtypeStruct(imp2.shape, F32), name="sample_select",
        compiler_params=pltpu.CompilerParams(vmem_limit_bytes=VMEM_LIMIT),
    )(imp2)


def _sample_sel_kernel(layer, pages_per_step, steps_per_seq, pt_ref, cache_ref, qz_ref, sel_ref, selnew_ref,
                       knew_ref, b2k_ref, ocmp_ref, owin_ref, gate_ref, o_ref, buf_ref, sem_ref, m_ref, l_ref,
                       acc_ref):
    rows = cache_ref.shape[2]

    def page_copies(step, j):
        page = _page_of(pt_ref, step, j, pages_per_step, steps_per_seq)
        slot = step % 2
        return [pltpu.make_async_copy(cache_ref.at[layer, page, :, pl.ds(CMP_LANES, CMP_LANES)],
                                      buf_ref.at[slot, pl.ds(j * rows, rows), :], sem_ref.at[slot])]

    slot = _paged_fetch(page_copies, pages_per_step, steps_per_seq)
    part = pl.program_id(1)
    qz = qz_ref[0]

    @pl.when(part == 0)
    def _():
        m_ref[...] = jnp.full(m_ref.shape, NEG, F32)
        l_ref[...] = jnp.zeros(l_ref.shape, F32)
        acc_ref[...] = jnp.zeros(acc_ref.shape, F32)

    kv = buf_ref[slot].astype(BF16)
    s = _dot_nt(qz, kv)
    msk = _dot(sel_ref[0].astype(BF16), b2k_ref[...]) > 0.5
    s = jnp.where(msk, s, NEG)
    m_old = m_ref[...]
    m_new = jnp.maximum(m_old, jnp.max(s, axis=-1, keepdims=True))
    alpha = jnp.exp(m_old - m_new)
    e = jnp.where(msk, jnp.exp(s - m_new), 0.0)
    l_ref[...] = alpha * l_ref[...] + jnp.sum(e, axis=-1, keepdims=True)
    acc_ref[...] = alpha * acc_ref[...] + _dot(e.astype(BF16), kv)
    m_ref[...] = m_new

    @pl.when(part == steps_per_seq - 1)
    def _():
        knew = knew_ref[0].astype(BF16).astype(F32)
        s_new = jnp.sum(qz.astype(F32) * knew, axis=-1, keepdims=True)
        on = selnew_ref[0][:, 0:1] > 0.5
        s_new = jnp.where(on, s_new, NEG)
        m_old = m_ref[...]
        m_new = jnp.maximum(m_old, s_new)
        alpha = jnp.exp(m_old - m_new)
        e_new = jnp.where(on, jnp.exp(s_new - m_new), 0.0)
        l = alpha * l_ref[...] + e_new
        acc = alpha * acc_ref[...] + e_new.astype(BF16).astype(F32) * knew
        o_sel = (acc / jnp.maximum(l, 1e-30))[:, KV_WIDTH:]
        g = gate_ref[0]
        o_ref[0] = g[:, 0:1] * ocmp_ref[0] + g[:, 1:2] * o_sel + g[:, 2:3] * owin_ref[0]


def _sample_sel(cache4, page_table, layer, qz2, sel16, knew, ocmp, owin, gz, steps_per_seq=2):
    db, n_pages = page_table.shape
    page = cache4.shape[2]
    pps = n_pages // steps_per_seq
    keys = pps * page
    blocks = keys // SEL_BLOCK
    b2k = jnp.asarray(_block_to_key(blocks, keys), dtype=BF16)
    blk3 = lambda a: pl.BlockSpec((1,) + a.shape[1:], lambda i, p, pt: (i, 0, 0))
    kern = functools.partial(_sample_sel_kernel, layer, pps, steps_per_seq)
    return pl.pallas_call(
        kern,
        grid_spec=pltpu.PrefetchScalarGridSpec(
            num_scalar_prefetch=1, grid=(db, steps_per_seq),
            in_specs=[pl.BlockSpec(memory_space=pl.ANY), blk3(qz2),
                      pl.BlockSpec((1, SROWS, blocks), lambda i, p, pt: (i, 0, p)),
                      pl.BlockSpec((1, SROWS, LANES), lambda i, p, pt: (i, 0, steps_per_seq * blocks // LANES)),
                      blk3(knew),
                      pl.BlockSpec(b2k.shape, lambda i, p, pt: (0, 0)), blk3(ocmp), blk3(owin), blk3(gz)],
            out_specs=pl.BlockSpec((1, SROWS, KV_WIDTH), lambda i, p, pt: (i, 0, 0)),
            scratch_shapes=[pltpu.VMEM((2, keys, CMP_LANES), F32), pltpu.SemaphoreType.DMA((2,)),
                            pltpu.VMEM((SROWS, 1), F32), pltpu.VMEM((SROWS, 1), F32),
                            pltpu.VMEM((SROWS, CMP_LANES), F32)]),
        out_shape=jax.ShapeDtypeStruct((db, SROWS, KV_WIDTH), F32),
        compiler_params=_params("arbitrary", "arbitrary"), name="sample_sel",
    )(page_table, cache4, qz2, sel16, sel16, knew, b2k, ocmp, owin, gz)


def _split_dot(dot, a, b):
    ah, bh = a.astype(BF16), b.astype(BF16)
    al, bl = (a - ah.astype(F32)).astype(BF16), (b - bh.astype(F32)).astype(BF16)
    return dot(ah, bh) + dot(ah, bl) + dot(al, bh)


def _head_sum(x, hm):
    xh = x.astype(BF16)
    xl = (x - xh.astype(F32)).astype(BF16)
    return _dot(xh, hm) + _dot(xl, hm)


def _softplus(z):
    return jnp.maximum(z, 0.0) + jnp.log(1.0 + jnp.exp(-jnp.abs(z)))


def _rwkv_kernel(chunk, n_valid, pr_ref, prev_ref, s0_ref, mu_ref, w0_ref, wup_ref, a0_ref, aup_ref, gup_ref,
                 kk_ref, ka_ref, rk_ref, lnw_ref, lnb_ref, hm_ref, tri_ref, out_ref, sout_ref,
                 carry_ref, state_ref, r_ref, k_ref, v_ref, lw_ref, a_ref, b_ref, y_ref):
    ti = pl.program_id(1)
    tt = pr_ref.shape[1]
    w_ = RWKV_WIDTH

    @pl.when(ti == 0)
    def _():
        carry_ref[...] = prev_ref[0]
        state_ref[...] = s0_ref[0]

    p = pr_ref[0]
    row = lax.broadcasted_iota(jnp.int32, (tt, 1), 0)
    shifted = jnp.where(row == 0, carry_ref[...], pltpu.roll(p, 1, 0))
    carry_ref[...] = p[tt - 1:tt]
    xs = p + (shifted - p) * mu_ref[...]
    r, k, v = xs[:, :w_], xs[:, w_:2 * w_], xs[:, 2 * w_:3 * w_]
    wa = xs[:, 3 * w_:3 * w_ + DECAY_LORA + AAA_LORA]
    gl = xs[:, 3 * w_ + DECAY_LORA + AAA_LORA:]
    hm = hm_ref[...]
    w = -_softplus(-(w0_ref[...] + _dot(jnp.tanh(wa).astype(BF16), wup_ref[...]))) - 0.5
    lw = -jnp.exp(w)
    a = jax.nn.sigmoid(a0_ref[...] + _dot(wa.astype(BF16), aup_ref[...]))
    g = _dot(jax.nn.sigmoid(gl).astype(BF16), gup_ref[...])
    kk = k * kk_ref[...]
    kk = kk / jnp.maximum(jnp.sqrt(_head_sum(kk * kk, hm)), 1e-12)
    k = k * (1.0 + (a - 1.0) * ka_ref[...])
    av, bv = -kk, kk * a
    if n_valid < tt:
        keep = row < n_valid
        lw, k, v, av, bv = (jnp.where(keep, t, 0.0) for t in (lw, k, v, av, bv))
    r_ref[...], k_ref[...], v_ref[...], lw_ref[...], a_ref[...], b_ref[...] = r, k, v, lw, av, bv

    ci = lax.broadcasted_iota(jnp.int32, (chunk, chunk), 0)
    cj = lax.broadcasted_iota(jnp.int32, (chunk, chunk), 1)
    strict, incl = ci > cj, ci >= cj
    dotf = functools.partial(_split_dot, _dot)
    dotf_nt = functools.partial(_split_dot, _dot_nt)
    dotf_tn = functools.partial(_split_dot, _dot_tn)

    def chunk_step(c, carry):
        c0 = pl.multiple_of(c * chunk, chunk)
        rows = pl.ds(c0, chunk)
        lwc = lw_ref[rows, :]
        cs = _dot(tri_ref[...], lwc, precision=HIGHEST)
        e_in, e_out = jnp.exp(cs), jnp.exp(-cs)
        total = cs[chunk - 1:chunk]
        e_end = jnp.exp(total - cs)
        rt = r_ref[rows, :] * e_in
        at = a_ref[rows, :] * jnp.exp(cs - lwc)
        kc, bc, vc = k_ref[rows, :], b_ref[rows, :], v_ref[rows, :]
        kt, bt = kc * e_out, bc * e_out
        kh, bh = kc * e_end, bc * e_end
        e_tot = jnp.exp(total)
        for h in range(RWKV_HEADS):
            sl = slice(h * RWKV_HEAD, (h + 1) * RWKV_HEAD)
            s0 = state_ref[h]
            ar = jnp.concatenate([at[:, sl], rt[:, sl]], axis=0)
            bk = jnp.concatenate([bt[:, sl], kt[:, sl]], axis=0)
            gm = dotf_nt(ar, bk)
            lab = jnp.where(strict, gm[:chunk, :chunk], 0.0)
            lak = jnp.where(strict, gm[:chunk, chunk:], 0.0)
            mrb = jnp.where(incl, gm[chunk:, :chunk], 0.0)
            mrk = jnp.where(incl, gm[chunk:, chunk:], 0.0)
            a_s = dotf_nt(ar, s0)
            vh = vc[:, sl]
            u = a_s[:chunk] + dotf(lak, vh)
            x, n = lab, 1
            while True:
                u = u + dotf(x, u)
                n *= 2
                if n >= chunk:
                    break
                x = dotf(x, x)
            uv = jnp.concatenate([u, vh], axis=0)
            y = a_s[chunk:] + dotf(jnp.concatenate([mrb, mrk], axis=1), uv)
            y_ref[rows, sl] = y
            state_ref[h] = s0 * e_tot[:, sl] + dotf_tn(uv, jnp.concatenate([bh[:, sl], kh[:, sl]], axis=0))
        return carry

    lax.fori_loop(0, tt // chunk, chunk_step, 0)

    y = y_ref[...]
    inv_n = 1.0 / RWKV_HEAD
    mean = _head_sum(y, hm) * inv_n
    d = y - mean
    var = _head_sum(d * d, hm) * inv_n
    y = d * lax.rsqrt(var + LNX_EPS) * lnw_ref[...] + lnb_ref[...]
    y = y + _head_sum(r_ref[...] * k_ref[...] * rk_ref[...], hm) * v_ref[...]
    out_ref[0] = (y * g).astype(out_ref.dtype)
    sout_ref[0] = state_ref[...]


def _rwkv(pr3, prev, s0, rw, chunk, tt, n_valid):
    b, t, _ = pr3.shape
    hm = jnp.asarray(np.kron(np.eye(RWKV_HEADS), np.ones((RWKV_HEAD, RWKV_HEAD))), dtype=BF16)
    tri = jnp.asarray(np.tril(np.ones((chunk, chunk))), dtype=F32)
    vec = lambda a: pl.BlockSpec(a.shape, lambda i, j: (0, 0))
    prev3 = prev[:, None, :]
    names = ("mu", "w0", "wup", "a0", "aup", "gup", "kk", "ka", "rk", "lnw", "lnb")
    kern = functools.partial(_rwkv_kernel, chunk, n_valid)
    return pl.pallas_call(
        kern, grid=(b, t // tt),
        in_specs=[pl.BlockSpec((1, tt, RWKV_PROJ), lambda i, j: (i, j, 0)),
                  pl.BlockSpec((1, 1, RWKV_PROJ), lambda i, j: (i, 0, 0)),
                  pl.BlockSpec((1, RWKV_HEADS, RWKV_HEAD, RWKV_HEAD), lambda i, j: (i, 0, 0, 0))] +
                 [vec(rw[n]) for n in names] + [vec(hm), vec(tri)],
        out_specs=[pl.BlockSpec((1, tt, RWKV_WIDTH), lambda i, j: (i, j, 0)),
                   pl.BlockSpec((1, RWKV_HEADS, RWKV_HEAD, RWKV_HEAD), lambda i, j: (i, 0, 0, 0))],
        out_shape=(jax.ShapeDtypeStruct((b, t, RWKV_WIDTH), BF16),
                   jax.ShapeDtypeStruct((b, RWKV_HEADS, RWKV_HEAD, RWKV_HEAD), F32)),
        scratch_shapes=[pltpu.VMEM((1, RWKV_PROJ), F32), pltpu.VMEM((RWKV_HEADS, RWKV_HEAD, RWKV_HEAD), F32)] +
                       [pltpu.VMEM((tt, RWKV_WIDTH), F32) for _ in range(7)],
        compiler_params=_params("arbitrary", "arbitrary"), name="rwkv",
    )(pr3, prev3, s0, *[rw[n] for n in names], hm, tri)


def _ffn_head(x_ref, mix_ref, wo_ref, gpost_ref, gpre_ref, x1_ref, h2_ref, acc_ref):
    x1 = x_ref[...] + _rms(_dot(mix_ref[...], wo_ref[...]), gpost_ref[...])
    x1_ref[...] = x1
    h2_ref[...] = _rms(x1, gpre_ref[...]).astype(BF16)
    acc_ref[...] = jnp.zeros(acc_ref.shape, F32)


def _ffn_tail(gate2, gate1, gate, val, cw_ref, cb_ref, wd_ref, acc_ref):
    conv = cb_ref[...] + (gate2 * cw_ref[0:1, :] + gate1 * cw_ref[1:2, :] + gate * cw_ref[2:3, :])
    acc_ref[...] += _dot((jax.nn.gelu(conv) * val).astype(BF16), wd_ref[...])


def _ffn_seq_kernel(tiles_per_seq, x_ref, mix_ref, wo_ref, gpost_ref, gpre_ref, gfpost_ref, wg_ref, wv_ref, cw_ref,
                    cb_ref, wd_ref, out_ref, tail_ref, x1_ref, h2_ref, acc_ref, carry_ref):
    i, j = pl.program_id(0), pl.program_id(1)
    tm = x_ref.shape[0]

    @pl.when(j == 0)
    def _():
        _ffn_head(x_ref, mix_ref, wo_ref, gpost_ref, gpre_ref, x1_ref, h2_ref, acc_ref)

    h2 = h2_ref[...]
    gate, val = _dot(h2, wg_ref[...]), _dot(h2, wv_ref[...])
    @pl.when(i % tiles_per_seq == 0)
    def _():
        carry_ref[j] = jnp.zeros(carry_ref.shape[1:], F32)

    prev = carry_ref[j]
    row = lax.broadcasted_iota(jnp.int32, (tm, 1), 0)
    gate1 = jnp.where(row == 0, prev[7:8], pltpu.roll(gate, 1, 0))
    gate2 = jnp.where(row == 0, prev[6:7], jnp.where(row == 1, prev[7:8], pltpu.roll(gate, 2, 0)))
    carry_ref[j] = gate[tm - 8:tm]
    tail_ref[0] = gate[tm - 8:tm]
    _ffn_tail(gate2, gate1, gate, val, cw_ref, cb_ref, wd_ref, acc_ref)

    @pl.when(j == pl.num_programs(1) - 1)
    def _():
        out_ref[...] = x1_ref[...] + _rms(acc_ref[...], gfpost_ref[...])


def _ffn_step_kernel(x_ref, mix_ref, wo_ref, gpost_ref, gpre_ref, gfpost_ref, wg_ref, wv_ref, cw_ref, cb_ref,
                     wd_ref, prev_ref, out_ref, gate_ref, x1_ref, h2_ref, acc_ref):
    j = pl.program_id(0)

    @pl.when(j == 0)
    def _():
        _ffn_head(x_ref, mix_ref, wo_ref, gpost_ref, gpre_ref, x1_ref, h2_ref, acc_ref)

    h2 = h2_ref[...]
    gate, val = _dot(h2, wg_ref[...]), _dot(h2, wv_ref[...])
    gate_ref[...] = gate
    _ffn_tail(prev_ref[0], prev_ref[1], gate, val, cw_ref, cb_ref, wd_ref, acc_ref)

    @pl.when(j == pl.num_programs(0) - 1)
    def _():
        out_ref[...] = x1_ref[...] + _rms(acc_ref[...], gfpost_ref[...])


def _ffn_seq(x2, mix2, fw, seq_len, tm, tf):
    m, d = x2.shape
    d_ff = fw["wd"].shape[0]
    nf = d_ff // tf
    row = lambda i, j: (i, 0)
    full = lambda i, j: (0, 0)
    tiles_per_seq = seq_len // tm
    scratch = [pltpu.VMEM((tm, d), F32), pltpu.VMEM((tm, d), BF16), pltpu.VMEM((tm, d), F32),
               pltpu.VMEM((nf, 8, tf), F32)]
    return pl.pallas_call(
        functools.partial(_ffn_seq_kernel, tiles_per_seq), grid=(m // tm, nf),
        in_specs=[pl.BlockSpec((tm, d), row), pl.BlockSpec((tm, d), row), pl.BlockSpec((d, d), full),
                  pl.BlockSpec((1, d), full), pl.BlockSpec((1, d), full), pl.BlockSpec((1, d), full),
                  pl.BlockSpec((d, tf), lambda i, j: (0, j)), pl.BlockSpec((d, tf), lambda i, j: (0, nf + j)),
                  pl.BlockSpec((CONV_WIDTH, tf), lambda i, j: (0, j)), pl.BlockSpec((1, tf), lambda i, j: (0, j)),
                  pl.BlockSpec((tf, d), lambda i, j: (j, 0))],
        out_specs=[pl.BlockSpec((tm, d), row), pl.BlockSpec((1, 8, tf), lambda i, j: (i, 0, j))],
        out_shape=(jax.ShapeDtypeStruct((m, d), F32), jax.ShapeDtypeStruct((m // tm, 8, d_ff), F32)),
        scratch_shapes=scratch, compiler_params=_params("arbitrary", "arbitrary"), name="ffn_seq",
    )(x2, mix2, fw["wo"], fw["gpost"], fw["gpre"], fw["gfpost"], fw["wup"], fw["wup"], fw["cw"], fw["cb"], fw["wd"])


def _ffn_step(x2, mix2, fw, prev, tf):
    m, d = x2.shape
    d_ff = fw["wd"].shape[0]
    nf = d_ff // tf
    full = lambda j: (0, 0)
    scratch = [pltpu.VMEM((m, d), F32), pltpu.VMEM((m, d), BF16), pltpu.VMEM((m, d), F32)]
    return pl.pallas_call(
        _ffn_step_kernel, grid=(nf,),
        in_specs=[pl.BlockSpec((m, d), full), pl.BlockSpec((m, d), full), pl.BlockSpec((d, d), full),
                  pl.BlockSpec((1, d), full), pl.BlockSpec((1, d), full), pl.BlockSpec((1, d), full),
                  pl.BlockSpec((d, tf), lambda j: (0, j)), pl.BlockSpec((d, tf), lambda j: (0, nf + j)),
                  pl.BlockSpec((CONV_WIDTH, tf), lambda j: (0, j)), pl.BlockSpec((1, tf), lambda j: (0, j)),
                  pl.BlockSpec((tf, d), lambda j: (j, 0)),
                  pl.BlockSpec((CONV_WIDTH - 1, m, tf), lambda j: (0, 0, j))],
        out_specs=[pl.BlockSpec((m, d), full), pl.BlockSpec((m, tf), lambda j: (0, j))],
        out_shape=(jax.ShapeDtypeStruct((m, d), F32), jax.ShapeDtypeStruct((m, d_ff), F32)),
        scratch_shapes=scratch, compiler_params=_params("arbitrary"), name="ffn_step",
    )(x2, mix2, fw["wo"], fw["gpost"], fw["gpre"], fw["gfpost"], fw["wup"], fw["wup"], fw["cw"], fw["cb"],
      fw["wd"], prev)


def _layer_weights(l, p):
    w_in = p["w_in"][l]
    o = NSA_WIDTH + 6 * KV_WIDTH
    wg = jnp.pad(w_in[:, o:NSA_PROJ], ((0, 0), (0, LANES - N_BRANCH * NSA_HEADS)))
    row = lambda a: a[l][None, :].astype(F32)
    zpad = jnp.zeros((AAA_LORA, RWKV_WIDTH), F32)
    rw = {"mu": row(p["rwkv_mu"]), "w0": row(p["rwkv_w0"]),
          "wup": jnp.concatenate([p["rwkv_w_up"][l], zpad], axis=0).astype(BF16),
          "a0": row(p["rwkv_a0"]), "aup": jnp.concatenate([zpad, p["rwkv_a_up"][l]], axis=0).astype(BF16),
          "gup": p["rwkv_g_up"][l].astype(BF16), "kk": row(p["rwkv_k_k"]), "ka": row(p["rwkv_k_a"]),
          "rk": p["rwkv_r_k"][l].reshape(1, RWKV_WIDTH), "lnw": row(p["rwkv_lnx_w"]), "lnb": row(p["rwkv_lnx_b"])}
    fw = {"wo": p["w_out"][l].astype(BF16), "gpost": row(p["norm_mix_post"]), "gpre": row(p["norm_ffn_pre"]),
          "gfpost": row(p["norm_ffn_post"]), "wup": p["ffn_w_up"][l].astype(BF16), "cw": p["ffn_conv_w"][l],
          "cb": row(p["ffn_conv_b"]), "wd": p["ffn_w_down"][l].astype(BF16)}
    return {"g_in": row(p["norm_mix_pre"]), "wq": w_in[:, :NSA_WIDTH].astype(BF16),
            "wkv": w_in[:, NSA_WIDTH:o].astype(BF16), "wg": wg.astype(BF16), "wr": w_in[:, NSA_PROJ:].astype(BF16),
            "cw": _compress_weights(p["cmp_pe"][l], p["cmp_w1"][l], p["cmp_w2"][l]), "rw": rw, "fw": fw}


def _prompt_layer(x, lw, tabs):
    b, t, d = x.shape
    m = b * t
    x2 = x.reshape(m, d)
    q, kvw, gates, pr = _norm_proj(x2, lw["g_in"], lw["wq"], lw["wkv"], lw["wg"], lw["wr"], tabs, tm=256)
    kvw3 = kvw.reshape(b, t, 6 * KV_WIDTH)
    kvc = _compress_prompt(kvw3, lw["cw"])
    cvh = kvc.astype(BF16).reshape(b, -1, 2, KV_HEADS, HEAD_DIM).transpose(0, 2, 3, 1, 4)
    qh = (q * ATTN_SCALE).astype(BF16).reshape(b, t, NSA_HEADS, HEAD_DIM).transpose(0, 2, 1, 3)
    kvh = kvw3[:, :, 2 * KV_WIDTH:].astype(BF16).reshape(b, t, 4, KV_HEADS, HEAD_DIM).transpose(0, 2, 3, 1, 4)
    gates_h = gates[:, :N_BRANCH * NSA_HEADS].reshape(b, t, NSA_HEADS, N_BRANCH).transpose(0, 2, 1, 3)
    o_nsa = _nsa_prompt(qh, kvh, cvh, gates_h).transpose(0, 2, 1, 3).reshape(m, NSA_WIDTH)
    s0 = jnp.zeros((b, RWKV_HEADS, RWKV_HEAD, RWKV_HEAD), F32)
    pr3 = pr.reshape(b, t, RWKV_PROJ)
    o_rwkv, wkv = _rwkv(pr3, jnp.zeros((b, RWKV_PROJ), F32), s0, lw["rw"], chunk=64, tt=min(t, 512), n_valid=t)
    mixed = jnp.concatenate([o_nsa, o_rwkv.reshape(m, RWKV_WIDTH)], axis=1)
    d_ff = lw["fw"]["wd"].shape[0]
    tm = min(t, 512)
    y, tail = _ffn_seq(x2, mixed, lw["fw"], t, tm=tm, tf=d_ff // 2)
    tail = tail[t // tm - 1::t // tm]
    kv_rows = kvw3[:, :, :4 * KV_WIDTH].reshape(b, t, 4, KV_HEADS, HEAD_DIM)
    wn = min(WINDOW, t)
    win = kvw3[:, t - wn:, 4 * KV_WIDTH:].reshape(b, wn, 2, KV_HEADS, HEAD_DIM)
    return y.reshape(b, t, d), kv_rows, win, wkv, pr3[:, -1], tail[:, 8 - (CONV_WIDTH - 1):]


def _sample_layer(x, lw, tabs, l, cache4, page_table, st_win, st_wkv, st_shift, st_conv):
    db, tn, d = x.shape
    past_len = page_table.shape[1] * cache4.shape[2]
    x2 = x.reshape(db, d)
    q, kvw, gates, pr = _norm_proj(x2, lw["g_in"], lw["wq"], lw["wkv"], lw["wg"], lw["wr"], tabs, tm=db)
    qs = (q * ATTN_SCALE).astype(BF16).reshape(db, KV_HEADS, HEADS_PER_KV, HEAD_DIM)
    eye = jnp.eye(KV_HEADS, dtype=BF16)
    qz = jnp.einsum('bghd,gj->bghjd', qs, eye).reshape(db, KV_HEADS, HEADS_PER_KV, KV_WIDTH)
    qz = jnp.pad(qz, ((0, 0), (0, 0), (0, 8 - HEADS_PER_KV), (0, 0))).reshape(db, SROWS, KV_WIDTH)
    qz2 = jnp.pad(qz, ((0, 0), (0, 0), (0, KV_WIDTH)))
    kvc = _compress_sample(cache4, page_table, l, lw["cw"])
    n_cmp = past_len // CMP_STRIDE - 1
    n_sel = -(-(past_len + tn) // SEL_BLOCK)
    n_sel_pad = -(-n_sel // LANES) * LANES
    c2s = jnp.asarray(_cmp_to_sel(kvc.shape[1], n_cmp, n_sel_pad, n_sel))
    wb = st_win.shape[1]
    win = st_win.reshape(db, wb, 2 * KV_WIDTH)
    wnew = kvw[:, None, 4 * KV_WIDTH:]
    ocmp, owin, imp = _sample_cmp_win(past_len, n_cmp, qz, kvc, win, wnew, c2s)
    sel = _select(imp.reshape(db * 8, n_sel_pad), n_sel, past_len // SEL_BLOCK).reshape(db, 8, n_sel_pad)
    sel16 = jnp.broadcast_to(sel[:, :KV_HEADS, None, :], (db, KV_HEADS, 8, n_sel_pad)).reshape(db, SROWS, n_sel_pad)
    gz = gates[:, :N_BRANCH * NSA_HEADS].reshape(db, KV_HEADS, HEADS_PER_KV, N_BRANCH)
    gz = jnp.pad(gz, ((0, 0), (0, 0), (0, 8 - HEADS_PER_KV), (0, 0))).reshape(db, SROWS, N_BRANCH)
    knew = kvw[:, None, 2 * KV_WIDTH:4 * KV_WIDTH]
    o16 = _sample_sel(cache4, page_table, l, qz2, sel16, knew, ocmp, owin, gz)
    o4 = o16.reshape(db, KV_HEADS, 8, KV_HEADS, HEAD_DIM)[:, :, :HEADS_PER_KV]
    o_nsa = jnp.stack([o4[:, g, :, g] for g in range(KV_HEADS)], axis=1).reshape(db, NSA_WIDTH)
    pad_t = 8
    pr3 = jnp.pad(pr[:, None, :], ((0, 0), (0, pad_t - tn), (0, 0)))
    o_rwkv, wkv = _rwkv(pr3, st_shift, st_wkv, lw["rw"], chunk=pad_t, tt=pad_t, n_valid=tn)
    mixed = jnp.concatenate([o_nsa.astype(BF16), o_rwkv[:, 0]], axis=1)
    prev = st_conv.transpose(1, 0, 2)
    d_ff = lw["fw"]["wd"].shape[0]
    y, gate = _ffn_step(x2, mixed, lw["fw"], prev, tf=d_ff // 2)
    kv_rows = kvw[:, :4 * KV_WIDTH].reshape(db, tn, 4, KV_HEADS, HEAD_DIM)
    win_new = jnp.concatenate([st_win[:, tn:], kvw[:, 4 * KV_WIDTH:].reshape(db, tn, 2, KV_HEADS, HEAD_DIM)], axis=1)
    conv_new = jnp.concatenate([st_conv[:, 1:], gate[:, None, :]], axis=1)
    return y.reshape(db, tn, d), kv_rows, win_new, wkv, pr, conv_new


def kernel(x_prompt, x_sample, cache_kv, page_table, state_win, state_wkv, state_shift, state_conv, norm_mix_pre, norm_mix_post, norm_ffn_pre, norm_ffn_post, w_in, w_out, cmp_pe, cmp_w1, cmp_w2, rwkv_mu, rwkv_w0, rwkv_w_up, rwkv_a0, rwkv_a_up, rwkv_g_up, rwkv_k_k, rwkv_k_a, rwkv_r_k, rwkv_lnx_w, rwkv_lnx_b, ffn_w_up, ffn_conv_w, ffn_conv_b, ffn_w_down):
    p = dict(norm_mix_pre=norm_mix_pre, norm_mix_post=norm_mix_post, norm_ffn_pre=norm_ffn_pre,
             norm_ffn_post=norm_ffn_post, w_in=w_in, w_out=w_out, cmp_pe=cmp_pe, cmp_w1=cmp_w1, cmp_w2=cmp_w2,
             rwkv_mu=rwkv_mu, rwkv_w0=rwkv_w0, rwkv_w_up=rwkv_w_up, rwkv_a0=rwkv_a0, rwkv_a_up=rwkv_a_up,
             rwkv_g_up=rwkv_g_up, rwkv_k_k=rwkv_k_k, rwkv_k_a=rwkv_k_a, rwkv_r_k=rwkv_r_k,
             rwkv_lnx_w=rwkv_lnx_w, rwkv_lnx_b=rwkv_lnx_b, ffn_w_up=ffn_w_up, ffn_conv_w=ffn_conv_w,
             ffn_conv_b=ffn_conv_b, ffn_w_down=ffn_w_down)
    depth = w_in.shape[0]
    b, t, _ = x_prompt.shape
    db, tn, _ = x_sample.shape
    n_pool, page = cache_kv.shape[1], cache_kv.shape[2]
    past_len = page_table.shape[1] * page
    cache4 = cache_kv.reshape(depth, n_pool, page, 2 * CMP_LANES)
    tabs_p = _rope_tables(jnp.arange(t, dtype=jnp.int32))
    tabs_s = _rope_tables(jnp.full((db,), past_len, jnp.int32))
    xp, xs = x_prompt, x_sample
    outs_p, outs_s = [], []
    for l in range(depth):
        lw = _layer_weights(l, p)
        xp, *st_p = _prompt_layer(xp, lw, tabs_p)
        xs, *st_s = _sample_layer(xs, lw, tabs_s, l, cache4, page_table, state_win[l], state_wkv[l],
                                  state_shift[l], state_conv[l])
        outs_p.append(st_p)
        outs_s.append(st_s)
    stack = lambda outs, i: jnp.stack([o[i] for o in outs])
    return (xp, xs, stack(outs_p, 0), stack(outs_s, 0), stack(outs_p, 1), stack(outs_s, 1),
            stack(outs_p, 2), stack(outs_s, 2), stack(outs_p, 3), stack(outs_s, 3),
            stack(outs_p, 4), stack(outs_s, 4))
```

```python
import functools

import jax
import jax.numpy as jnp
import numpy as np
from jax import lax
from jax.experimental import pallas as pl
from jax.experimental.pallas import tpu as pltpu

F32, BF16 = jnp.float32, jnp.bfloat16
HIGHEST = lax.Precision.HIGHEST

LANES = 128
VMEM_LIMIT = 56 * 1024 * 1024

HEAD_DIM = 64
NSA_HEADS = 8
KV_HEADS = 2
HEADS_PER_KV = NSA_HEADS // KV_HEADS
NSA_WIDTH = NSA_HEADS * HEAD_DIM
KV_WIDTH = KV_HEADS * HEAD_DIM
N_BRANCH = 3
CMP_STRIDE = 16
CMP_BLOCK = 2 * CMP_STRIDE
SEL_BLOCK = 64
SEL_TOPN = 16
WINDOW = 512
ROPE_THETA = 500000.0
ROPE_DIM = HEAD_DIM // 4
RWKV_HEAD = 64
RWKV_HEADS = 8
RWKV_WIDTH = RWKV_HEADS * RWKV_HEAD
DECAY_LORA = 64
AAA_LORA = 64
GATE_LORA = 128
RWKV_PROJ = 3 * RWKV_WIDTH + DECAY_LORA + AAA_LORA + GATE_LORA
NSA_PROJ = NSA_WIDTH + 6 * KV_WIDTH + N_BRANCH * NSA_HEADS
CONV_WIDTH = 3
NORM_EPS = 1e-6
LNX_EPS = 64e-5
NEG = -1e30
FORCED_SCORE = 1e9
ATTN_SCALE = HEAD_DIM ** -0.5


def _dot(a, b, precision=None):
    return jnp.dot(a, b, preferred_element_type=F32, precision=precision)


def _dot_nt(a, b, precision=None):
    return lax.dot_general(a, b, (((1,), (1,)), ((), ())), preferred_element_type=F32, precision=precision)


def _dot_tn(a, b, precision=None):
    return lax.dot_general(a, b, (((0,), (0,)), ((), ())), preferred_element_type=F32, precision=precision)


def _rms(x, g):
    return x * lax.rsqrt(jnp.mean(x * x, axis=-1, keepdims=True) + NORM_EPS) * g


def _masked_softmax(s, mask):
    s = jnp.where(mask, s, NEG)
    m = jnp.max(s, axis=-1, keepdims=True)
    e = jnp.where(mask, jnp.exp(s - m), 0.0)
    return e / jnp.maximum(jnp.sum(e, axis=-1, keepdims=True), 1e-30)


def _params(*sem):
    return pltpu.CompilerParams(dimension_semantics=sem, vmem_limit_bytes=VMEM_LIMIT)


def _norm_proj_kernel(x_ref, g_ref, wq_ref, wkv_ref, wg_ref, wr_ref, cos_ref, sa_ref, sb_ref,
                      q_ref, kvw_ref, gate_ref, pr_ref):
    h = _rms(x_ref[...], g_ref[...]).astype(BF16)
    cos, sa, sb = cos_ref[...], sa_ref[...], sb_ref[...]

    def rope(v):
        return v * cos + pltpu.roll(v, LANES - ROPE_DIM // 2, 1) * sa + pltpu.roll(v, ROPE_DIM // 2, 1) * sb

    q = _dot(h, wq_ref[...])
    for j in range(NSA_WIDTH // LANES):
        q_ref[:, j * LANES:(j + 1) * LANES] = rope(q[:, j * LANES:(j + 1) * LANES])
    kv = _dot(h, wkv_ref[...])
    for j in range(6 * KV_WIDTH // LANES):
        blk = kv[:, j * LANES:(j + 1) * LANES]
        kvw_ref[:, j * LANES:(j + 1) * LANES] = rope(blk) if j % 2 == 0 else blk
    gate_ref[...] = jax.nn.sigmoid(_dot(h, wg_ref[...]))
    pr_ref[...] = _dot(h, wr_ref[...])


def _norm_proj(x2, g, wq, wkv, wg, wr, tabs, tm):
    m, d = x2.shape
    cos, sa, sb = tabs
    tab_blocks = cos.shape[0] // tm
    row = lambda i: (i, 0)
    full = lambda i: (0, 0)
    tab = lambda i: (i % tab_blocks, 0)
    outs = (jax.ShapeDtypeStruct((m, NSA_WIDTH), F32), jax.ShapeDtypeStruct((m, 6 * KV_WIDTH), F32),
            jax.ShapeDtypeStruct((m, LANES), F32), jax.ShapeDtypeStruct((m, RWKV_PROJ), F32))
    return pl.pallas_call(
        _norm_proj_kernel, grid=(m // tm,),
        in_specs=[pl.BlockSpec((tm, d), row), pl.BlockSpec((1, d), full),
                  pl.BlockSpec(wq.shape, full), pl.BlockSpec(wkv.shape, full),
                  pl.BlockSpec(wg.shape, full), pl.BlockSpec(wr.shape, full),
                  pl.BlockSpec((tm, LANES), tab), pl.BlockSpec((tm, LANES), tab), pl.BlockSpec((tm, LANES), tab)],
        out_specs=[pl.BlockSpec((tm, NSA_WIDTH), row), pl.BlockSpec((tm, 6 * KV_WIDTH), row),
                   pl.BlockSpec((tm, LANES), row), pl.BlockSpec((tm, RWKV_PROJ), row)],
        out_shape=outs, compiler_params=_params("arbitrary"), name="norm_proj",
    )(x2, g, wq, wkv, wg, wr, cos, sa, sb)


def _rope_tables(pos):
    half = ROPE_DIM // 2
    inv_freq = ROPE_THETA ** (-jnp.arange(half, dtype=F32) / half)
    ang = pos.astype(F32)[:, None] * inv_freq[None, :]
    c, s = jnp.cos(ang), jnp.sin(ang)
    n = pos.shape[0]
    one = jnp.ones((n, HEAD_DIM - ROPE_DIM), F32)
    zero = jnp.zeros((n, HEAD_DIM - ROPE_DIM), F32)
    zh = jnp.zeros((n, half), F32)
    cos = jnp.concatenate([c, c, one], axis=1)
    sa = jnp.concatenate([-s, zh, zero], axis=1)
    sb = jnp.concatenate([zh, s, zero], axis=1)
    rep = LANES // HEAD_DIM
    return tuple(jnp.tile(t, (1, rep)) for t in (cos, sa, sb))


CMP_LANES = 2 * KV_WIDTH


def _compress_accumulate(rows_refs, n_chunk, pea_ref, peb_ref, wa_ref, wb_ref, acca_ref, accb_ref, row0):
    for c, rows_ref in enumerate(rows_refs):
        lanes = slice(c * KV_WIDTH, (c + 1) * KV_WIDTH)
        da = db = None
        for l in range(CMP_STRIDE):
            x = rows_ref[pl.ds(l, n_chunk, stride=CMP_STRIDE), :]
            xa = (x + pea_ref[c, l:l + 1, :]).astype(BF16)
            xb = (x + peb_ref[c, l:l + 1, :]).astype(BF16)
            ta, tb = _dot(xa, wa_ref[c, l]), _dot(xb, wb_ref[c, l])
            da, db = (ta, tb) if l == 0 else (da + ta, db + tb)
        acca_ref[pl.ds(row0, n_chunk), lanes] = da
        accb_ref[pl.ds(row0, n_chunk), lanes] = db


def _compress_finish(acca_ref, accb_ref, w2_ref, out_ref):
    n = acca_ref.shape[0]
    rows = lax.broadcasted_iota(jnp.int32, (n, 1), 0)
    for c in range(2):
        lanes = slice(c * KV_WIDTH, (c + 1) * KV_WIDTH)
        nxt = pltpu.roll(accb_ref[:, lanes], n - 1, 0)
        h = jax.nn.gelu(acca_ref[:, lanes] + nxt).astype(BF16)
        out_ref[0, :, lanes] = jnp.where(rows < n - 1, _dot(h, w2_ref[c]), 0.0)


def _compress_prompt_kernel(k_ref, v_ref, pea_ref, peb_ref, wa_ref, wb_ref, w2_ref, out_ref, acca_ref, accb_ref):
    n = acca_ref.shape[0]
    _compress_accumulate((k_ref.at[0], v_ref.at[0]), n, pea_ref, peb_ref, wa_ref, wb_ref, acca_ref, accb_ref, 0)
    _compress_finish(acca_ref, accb_ref, w2_ref, out_ref)


_CW_NAMES = ("pea", "peb", "wa", "wb", "w2")


def _compress_prompt(kvw3, cw):
    b, t, _ = kvw3.shape
    n = t // CMP_STRIDE
    const = lambda a: pl.BlockSpec(a.shape, lambda i: (0,) * a.ndim)
    return pl.pallas_call(
        _compress_prompt_kernel, grid=(b,),
        in_specs=[pl.BlockSpec((1, t, KV_WIDTH), lambda i: (i, 0, 0)),
                  pl.BlockSpec((1, t, KV_WIDTH), lambda i: (i, 0, 1))] + [const(cw[k]) for k in _CW_NAMES],
        out_specs=pl.BlockSpec((1, n, CMP_LANES), lambda i: (i, 0, 0)),
        out_shape=jax.ShapeDtypeStruct((b, n, CMP_LANES), F32),
        scratch_shapes=[pltpu.VMEM((n, CMP_LANES), F32), pltpu.VMEM((n, CMP_LANES), F32)],
        compiler_params=_params("arbitrary"), name="compress_prompt",
    )(kvw3, kvw3, *[cw[k] for k in _CW_NAMES])


def _compress_weights(pe, w1, w2):
    eye = jnp.eye(KV_HEADS, dtype=F32)
    blockdiag = lambda w: jnp.einsum('...de,gj->...gdje', w, eye).reshape(w.shape[:-2] + (KV_WIDTH, KV_WIDTH))
    lanes = lambda p: jnp.tile(p, (1, 1, KV_HEADS))
    return {"pea": lanes(pe[:, :CMP_STRIDE]), "peb": lanes(pe[:, CMP_STRIDE:]),
            "wa": blockdiag(w1[:, :CMP_STRIDE]).astype(BF16), "wb": blockdiag(w1[:, CMP_STRIDE:]).astype(BF16),
            "w2": blockdiag(w2).astype(BF16)}


def _paged_fetch(page_copies, pages_per_step, steps_per_seq):
    step = pl.program_id(0) * steps_per_seq + pl.program_id(1)
    total = pl.num_programs(0) * steps_per_seq

    def start_all(s):
        def body(j, c):
            for cp in page_copies(s, j):
                cp.start()
            return c
        lax.fori_loop(0, pages_per_step, body, 0)

    @pl.when(step == 0)
    def _():
        start_all(step)

    @pl.when(step + 1 < total)
    def _():
        start_all(step + 1)

    def wait_body(j, c):
        for cp in page_copies(step, j):
            cp.wait()
        return c
    lax.fori_loop(0, pages_per_step, wait_body, 0)
    return step % 2


def _page_of(pt_ref, step, j, pages_per_step, steps_per_seq):
    return pt_ref[step // steps_per_seq, (step % steps_per_seq) * pages_per_step + j]


def _compress_sample_kernel(layer, pages_per_step, steps_per_seq, pt_ref, cache_ref, pea_ref, peb_ref, wa_ref,
                            wb_ref, w2_ref, out_ref, buf_ref, sem_ref, rows_ref, acca_ref, accb_ref):
    page_len = cache_ref.shape[3]

    def page_copies(step, j):
        page = _page_of(pt_ref, step, j, pages_per_step, steps_per_seq)
        return [pltpu.make_async_copy(cache_ref.at[layer, page, pl.ds(0, CMP_LANES), :], buf_ref.at[step % 2, j],
                                      sem_ref.at[step % 2])]

    slot = _paged_fetch(page_copies, pages_per_step, steps_per_seq)

    def to_rows(j, carry):
        r0 = pl.multiple_of(j * page_len, page_len)
        for c in range(2):
            rows_ref[c, pl.ds(r0, page_len), :] = buf_ref[slot, j, c * KV_WIDTH:(c + 1) * KV_WIDTH, :].T
        return carry
    lax.fori_loop(0, pages_per_step, to_rows, 0)

    part = pl.program_id(1)
    n = rows_ref.shape[1] // CMP_STRIDE
    row0 = pl.multiple_of(part * n, n)
    _compress_accumulate((rows_ref.at[0], rows_ref.at[1]), n, pea_ref, peb_ref, wa_ref, wb_ref,
                         acca_ref, accb_ref, row0)

    @pl.when(part == steps_per_seq - 1)
    def _():
        _compress_finish(acca_ref, accb_ref, w2_ref, out_ref)


def _compress_sample(cache_t, page_table, layer, cw, steps_per_seq=2):
    db, n_pages = page_table.shape
    page = cache_t.shape[3]
    pps = n_pages // steps_per_seq
    n = n_pages * page // CMP_STRIDE
    const = lambda a: pl.BlockSpec(a.shape, lambda i, p, pt: (0,) * a.ndim)
    kern = functools.partial(_compress_sample_kernel, layer, pps, steps_per_seq)
    return pl.pallas_call(
        kern,
        grid_spec=pltpu.PrefetchScalarGridSpec(
            num_scalar_prefetch=1, grid=(db, steps_per_seq),
            in_specs=[pl.BlockSpec(memory_space=pl.ANY)] + [const(cw[k]) for k in _CW_NAMES],
            out_specs=pl.BlockSpec((1, n, CMP_LANES), lambda i, p, pt: (i, 0, 0)),
            scratch_shapes=[pltpu.VMEM((2, pps, CMP_LANES, page), F32), pltpu.SemaphoreType.DMA((2,)),
                            pltpu.VMEM((2, pps * page, KV_WIDTH), F32),
                            pltpu.VMEM((n, CMP_LANES), F32), pltpu.VMEM((n, CMP_LANES), F32)]),
        out_shape=jax.ShapeDtypeStruct((db, n, CMP_LANES), F32),
        compiler_params=_params("arbitrary", "arbitrary"), name="compress_sample",
    )(page_table, cache_t, *[cw[k] for k in _CW_NAMES])


def _cmp_to_sel(n_cmp_pad, n_cmp, n_sel_pad, n_sel):
    c0 = np.arange(n_cmp_pad)[:, None] * CMP_STRIDE
    s0 = np.arange(n_sel_pad)[None, :] * SEL_BLOCK
    hit = (c0 < s0 + SEL_BLOCK) & (c0 + CMP_BLOCK > s0)
    hit &= (np.arange(n_cmp_pad)[:, None] < n_cmp) & (np.arange(n_sel_pad)[None, :] < n_sel)
    return hit.astype(np.float32)


def _block_to_key(n_blk, n_key):
    return (np.arange(n_key)[None, :] // SEL_BLOCK == np.arange(n_blk)[:, None]).astype(np.float32)


def _nsa_prompt_kernel(qb, tk, q_ref, kc_ref, vc_ref, ks_ref, vs_ref, kw_ref, vw_ref, gate_ref, c2s_ref, b2k_ref,
                       o_ref):
    i = pl.program_id(2)
    qs = pl.multiple_of(i * qb, qb)
    rows = HEADS_PER_KV * qb
    n_cmp = kc_ref.shape[0]
    n_sel = c2s_ref.shape[0]
    q = q_ref[...].reshape(rows, HEAD_DIM)
    qpos = qs + lax.broadcasted_iota(jnp.int32, (rows, 1), 0) % qb

    def biased(s, bias):
        return (s.reshape(HEADS_PER_KV, qb, s.shape[-1]) + bias[None]).reshape(s.shape)

    s = _dot_nt(q, kc_ref[...])
    cmp_end = lax.broadcasted_iota(jnp.int32, (1, n_cmp), 1) * CMP_STRIDE + (CMP_BLOCK - 1)
    p = _masked_softmax(s, cmp_end <= qpos)
    o_cmp = _dot(p.astype(BF16), vc_ref[...])
    psum = p[0:qb]
    for h in range(1, HEADS_PER_KV):
        psum = psum + p[h * qb:(h + 1) * qb]
    imp_t = _dot_nt(c2s_ref[...], psum, precision=HIGHEST)

    blk = lax.broadcasted_iota(jnp.int32, (n_sel, qb), 0)
    qp = qs + lax.broadcasted_iota(jnp.int32, (n_sel, qb), 1)
    cur = qp // SEL_BLOCK
    forced = (blk == 0) | (blk == cur) | (blk == cur - 1)
    valid = blk <= cur
    score = jnp.where(valid, jnp.where(forced, FORCED_SCORE, imp_t), NEG)
    cnt = jnp.zeros((n_sel, qb), jnp.int32)
    for j in range(n_sel):
        sj = score[j:j + 1, :]
        ahead = (sj > score) | ((sj == score) & (j < blk))
        cnt = cnt + ahead.astype(jnp.int32)
    sel_t = jnp.where((cnt < SEL_TOPN) & valid & (blk < qs // SEL_BLOCK), 1.0, 0.0).astype(BF16)

    qrow = lax.broadcasted_iota(jnp.int32, (qb, qb), 0)
    kcol = lax.broadcasted_iota(jnp.int32, (qb, qb), 1)
    s = biased(_dot_nt(q, ks_ref[pl.ds(qs, qb), :]), jnp.where(kcol <= qrow, 0.0, NEG))
    m = jnp.max(s, axis=-1, keepdims=True)
    e = jnp.exp(s - m)
    l = jnp.sum(e, axis=-1, keepdims=True)
    acc = _dot(e.astype(BF16), vs_ref[pl.ds(qs, qb), :])

    def sel_step(kt, carry):
        m, l, acc = carry
        k0 = pl.multiple_of(kt * tk, tk)
        bias = (_dot_tn(sel_t, b2k_ref[kt]) - 1.0) * -NEG
        s = biased(_dot_nt(q, ks_ref[pl.ds(k0, tk), :]), bias)
        m_new = jnp.maximum(m, jnp.max(s, axis=-1, keepdims=True))
        alpha = jnp.exp(m - m_new)
        e = jnp.exp(s - m_new)
        l = alpha * l + jnp.sum(e, axis=-1, keepdims=True)
        acc = alpha * acc + _dot(e.astype(BF16), vs_ref[pl.ds(k0, tk), :])
        return m_new, l, acc

    _, l, acc = lax.fori_loop(0, (qs + tk - 1) // tk, sel_step, (m, l, acc))
    o_sel = acc / jnp.maximum(l, 1e-30)

    nw = WINDOW + qb
    w0 = pl.multiple_of(jnp.maximum(qs - WINDOW, 0), qb)
    rel = (qs + lax.broadcasted_iota(jnp.int32, (qb, nw), 0)) - (w0 + lax.broadcasted_iota(jnp.int32, (qb, nw), 1))
    s = biased(_dot_nt(q, kw_ref[pl.ds(w0, nw), :]), jnp.where((rel >= 0) & (rel < WINDOW), 0.0, NEG))
    e = jnp.exp(s - jnp.max(s, axis=-1, keepdims=True))
    p = e / jnp.maximum(jnp.sum(e, axis=-1, keepdims=True), 1e-30)
    o_win = _dot(p.astype(BF16), vw_ref[pl.ds(w0, nw), :])

    g = gate_ref[...].reshape(rows, N_BRANCH)
    o = g[:, 0:1] * o_cmp + g[:, 1:2] * o_sel + g[:, 2:3] * o_win
    o_ref[...] = o.astype(o_ref.dtype).reshape(HEADS_PER_KV, qb, HEAD_DIM)


def _nsa_prompt(qh, kvh, cvh, gates_h, qb=128, tk=512):
    b, _, t, _ = qh.shape
    assert qb % SEL_BLOCK == 0 and t % tk == 0 and WINDOW % qb == 0 and WINDOW + qb <= t
    n_cmp = cvh.shape[3]
    n_sel = t // SEL_BLOCK
    c2s = jnp.asarray(_cmp_to_sel(n_cmp, n_cmp - 1, n_sel, n_sel).T)
    b2k = jnp.asarray(_block_to_key(n_sel, t).reshape(n_sel, t // tk, tk).transpose(1, 0, 2), dtype=BF16)
    comp = lambda c: pl.BlockSpec((None, None, None, t, HEAD_DIM), lambda bi, g, i: (bi, c, g, 0, 0))
    ccomp = lambda c: pl.BlockSpec((None, None, None, n_cmp, HEAD_DIM), lambda bi, g, i: (bi, c, g, 0, 0))
    qspec = pl.BlockSpec((None, HEADS_PER_KV, qb, HEAD_DIM), lambda bi, g, i: (bi, g, i, 0))
    kern = functools.partial(_nsa_prompt_kernel, qb, tk)
    return pl.pallas_call(
        kern, grid=(b, KV_HEADS, t // qb),
        in_specs=[qspec, ccomp(0), ccomp(1), comp(0), comp(1), comp(2), comp(3),
                  pl.BlockSpec((None, HEADS_PER_KV, qb, N_BRANCH), lambda bi, g, i: (bi, g, i, 0)),
                  pl.BlockSpec(c2s.shape, lambda bi, g, i: (0, 0)),
                  pl.BlockSpec(b2k.shape, lambda bi, g, i: (0, 0, 0))],
        out_specs=qspec,
        out_shape=jax.ShapeDtypeStruct(qh.shape, BF16),
        compiler_params=_params("arbitrary", "arbitrary", "arbitrary"), name="nsa_prompt",
    )(qh, cvh, cvh, kvh, kvh, kvh, kvh, gates_h, c2s, b2k)


SROWS = 8 * KV_HEADS


def _sample_cmp_win_kernel(past_len, n_cmp, qz_ref, kvc_ref, win_ref, wnew_ref, c2s_ref, ocmp_ref, owin_ref,
                           imp_ref):
    qz = qz_ref[0]
    kvc = kvc_ref[0]
    n_pad = kvc.shape[0]
    kc, vc = kvc[:, :KV_WIDTH].astype(BF16), kvc[:, KV_WIDTH:].astype(BF16)
    s = _dot_nt(qz, kc)
    n = lax.broadcasted_iota(jnp.int32, (1, n_pad), 1)
    p = _masked_softmax(s, (n < n_cmp) & (n * CMP_STRIDE + (CMP_BLOCK - 1) <= past_len))
    ocmp_ref[0] = _dot(p.astype(BF16), vc)
    r = lax.broadcasted_iota(jnp.int32, (SROWS, 1), 0)
    p = jnp.where(r % 8 < HEADS_PER_KV, p, 0.0)
    psum = jnp.concatenate([jnp.sum(p[8 * g:8 * g + 8], axis=0, keepdims=True) for g in range(KV_HEADS)] +
                           [jnp.zeros((8 - KV_HEADS, n_pad), F32)], axis=0)
    imp_ref[0] = _dot(psum, c2s_ref[...], precision=HIGHEST)

    win = win_ref[0]
    wb = win.shape[1]
    kw, vw = win[:KV_WIDTH].astype(BF16), win[KV_WIDTH:].astype(BF16)
    wnew = wnew_ref[0]
    s = _dot(qz, kw)
    s_new = jnp.sum(qz.astype(F32) * wnew[:, :KV_WIDTH].astype(BF16).astype(F32), axis=-1, keepdims=True)
    rel = wb - lax.broadcasted_iota(jnp.int32, (1, wb), 1)
    msk = (rel >= 0) & (rel < WINDOW)
    s = jnp.where(msk, s, NEG)
    m = jnp.maximum(jnp.max(s, axis=-1, keepdims=True), s_new)
    e = jnp.where(msk, jnp.exp(s - m), 0.0)
    e_new = jnp.exp(s_new - m)
    den = jnp.maximum(jnp.sum(e, axis=-1, keepdims=True) + e_new, 1e-30)
    pv = _dot_nt((e / den).astype(BF16), vw)
    pn = (e_new / den).astype(BF16).astype(F32)
    owin_ref[0] = pv + pn * wnew[:, KV_WIDTH:].astype(BF16).astype(F32)


def _sample_cmp_win(past_len, n_cmp, qz, kvc, win, wnew, c2s):
    db = qz.shape[0]
    blk3 = lambda a: pl.BlockSpec((1,) + a.shape[1:], lambda i: (i, 0, 0))
    kern = functools.partial(_sample_cmp_win_kernel, past_len, n_cmp)
    outs = (jax.ShapeDtypeStruct((db, SROWS, KV_WIDTH), F32), jax.ShapeDtypeStruct((db, SROWS, KV_WIDTH), F32),
            jax.ShapeDtypeStruct((db, 8, c2s.shape[1]), F32))
    return pl.pallas_call(
        kern, grid=(db,),
        in_specs=[blk3(qz), blk3(kvc), blk3(win), blk3(wnew), pl.BlockSpec(c2s.shape, lambda i: (0, 0))],
        out_specs=[pl.BlockSpec((1, SROWS, KV_WIDTH), lambda i: (i, 0, 0)),
                   pl.BlockSpec((1, SROWS, KV_WIDTH), lambda i: (i, 0, 0)),
                   pl.BlockSpec((1, 8, c2s.shape[1]), lambda i: (i, 0, 0))],
        out_shape=outs, compiler_params=_params("arbitrary"), name="sample_cmp_win",
    )(qz, kvc, win, wnew, c2s)


def _select_kernel(n_sel, cur, imp_ref, sel_ref):
    imp = imp_ref[...]
    blk = lax.broadcasted_iota(jnp.int32, imp.shape, 1)
    forced = (blk == 0) | (blk == cur) | (blk == cur - 1)
    valid = blk <= cur
    score = jnp.where(valid, jnp.where(forced, FORCED_SCORE, imp), NEG)

    def body(j, cnt):
        sj = jnp.sum(jnp.where(blk == j, score, 0.0), axis=-1, keepdims=True)
        ahead = (sj > score) | ((sj == score) & (j < blk))
        return cnt + ahead.astype(jnp.int32)

    cnt = lax.fori_loop(0, n_sel, body, jnp.zeros(imp.shape, jnp.int32))
    sel_ref[...] = jnp.where((cnt < SEL_TOPN) & valid, 1.0, 0.0)


def _select(imp2, n_sel, cur):
    return pl.pallas_call(
        functools.partial(_select_kernel, n_sel, cur),
        out_shape=jax.ShapeDtypeStruct(imp2.shape, F32), name="sample_select",
        compiler_params=pltpu.CompilerParams(vmem_limit_bytes=VMEM_LIMIT),
    )(imp2)


def _sample_sel_kernel(layer, pages_per_step, steps_per_seq, pt_ref, cache_ref, qz_ref, sel_ref, selnew_ref,
                       knew_ref, b2k_ref, ocmp_ref, owin_ref, gate_ref, o_ref, buf_ref, sem_ref, m_ref, l_ref,
                       acc_ref):
    page_len = cache_ref.shape[3]

    def page_copies(step, j):
        page = _page_of(pt_ref, step, j, pages_per_step, steps_per_seq)
        return [pltpu.make_async_copy(cache_ref.at[layer, page, pl.ds(CMP_LANES, CMP_LANES), :],
                                      buf_ref.at[step % 2, :, pl.ds(pl.multiple_of(j * page_len, page_len), page_len)],
                                      sem_ref.at[step % 2])]

    slot = _paged_fetch(page_copies, pages_per_step, steps_per_seq)
    part = pl.program_id(1)
    qz = qz_ref[0]

    @pl.when(part == 0)
    def _():
        m_ref[...] = jnp.full(m_ref.shape, NEG, F32)
        l_ref[...] = jnp.zeros(l_ref.shape, F32)
        acc_ref[...] = jnp.zeros(acc_ref.shape, F32)

    kv = buf_ref[slot].astype(BF16)
    s = _dot(qz, kv[:KV_WIDTH])
    msk = _dot(sel_ref[0].astype(BF16), b2k_ref[...]) > 0.5
    s = jnp.where(msk, s, NEG)
    m_old = m_ref[...]
    m_new = jnp.maximum(m_old, jnp.max(s, axis=-1, keepdims=True))
    alpha = jnp.exp(m_old - m_new)
    e = jnp.where(msk, jnp.exp(s - m_new), 0.0)
    l_ref[...] = alpha * l_ref[...] + jnp.sum(e, axis=-1, keepdims=True)
    acc_ref[...] = alpha * acc_ref[...] + _dot_nt(e.astype(BF16), kv[KV_WIDTH:])
    m_ref[...] = m_new

    @pl.when(part == steps_per_seq - 1)
    def _():
        knew = knew_ref[0].astype(BF16).astype(F32)
        s_new = jnp.sum(qz.astype(F32) * knew[:, :KV_WIDTH], axis=-1, keepdims=True)
        on = selnew_ref[0][:, 0:1] > 0.5
        s_new = jnp.where(on, s_new, NEG)
        m_old = m_ref[...]
        m_new = jnp.maximum(m_old, s_new)
        alpha = jnp.exp(m_old - m_new)
        e_new = jnp.where(on, jnp.exp(s_new - m_new), 0.0)
        l = alpha * l_ref[...] + e_new
        acc = alpha * acc_ref[...] + e_new.astype(BF16).astype(F32) * knew[:, KV_WIDTH:]
        o_sel = acc / jnp.maximum(l, 1e-30)
        g = gate_ref[0]
        o_ref[0] = g[:, 0:1] * ocmp_ref[0] + g[:, 1:2] * o_sel + g[:, 2:3] * owin_ref[0]


def _sample_sel(cache_t, page_table, layer, qz, sel16, knew, ocmp, owin, gz, steps_per_seq=2):
    db, n_pages = page_table.shape
    page = cache_t.shape[3]
    pps = n_pages // steps_per_seq
    keys = pps * page
    blocks = keys // SEL_BLOCK
    b2k = jnp.asarray(_block_to_key(blocks, keys), dtype=BF16)
    blk3 = lambda a: pl.BlockSpec((1,) + a.shape[1:], lambda i, p, pt: (i, 0, 0))
    kern = functools.partial(_sample_sel_kernel, layer, pps, steps_per_seq)
    return pl.pallas_call(
        kern,
        grid_spec=pltpu.PrefetchScalarGridSpec(
            num_scalar_prefetch=1, grid=(db, steps_per_seq),
            in_specs=[pl.BlockSpec(memory_space=pl.ANY), blk3(qz),
                      pl.BlockSpec((1, SROWS, blocks), lambda i, p, pt: (i, 0, p)),
                      pl.BlockSpec((1, SROWS, LANES), lambda i, p, pt: (i, 0, steps_per_seq * blocks // LANES)),
                      blk3(knew),
                      pl.BlockSpec(b2k.shape, lambda i, p, pt: (0, 0)), blk3(ocmp), blk3(owin), blk3(gz)],
            out_specs=pl.BlockSpec((1, SROWS, KV_WIDTH), lambda i, p, pt: (i, 0, 0)),
            scratch_shapes=[pltpu.VMEM((2, CMP_LANES, keys), F32), pltpu.SemaphoreType.DMA((2,)),
                            pltpu.VMEM((SROWS, 1), F32), pltpu.VMEM((SROWS, 1), F32),
                            pltpu.VMEM((SROWS, KV_WIDTH), F32)]),
        out_shape=jax.ShapeDtypeStruct((db, SROWS, KV_WIDTH), F32),
        compiler_params=_params("arbitrary", "arbitrary"), name="sample_sel",
    )(page_table, cache_t, qz, sel16, sel16, knew, b2k, ocmp, owin, gz)


def _bdot(dot, a, b):
    return dot(a.astype(BF16), b.astype(BF16))


def _head_sum(x, hm):
    xh = x.astype(BF16)
    xl = (x - xh.astype(F32)).astype(BF16)
    return _dot(xh, hm) + _dot(xl, hm)


def _softplus(z):
    return jnp.maximum(z, 0.0) + jnp.log(1.0 + jnp.exp(-jnp.abs(z)))


def _rwkv_kernel(chunk, n_valid, pr_ref, prev_ref, s0_ref, mu_ref, w0_ref, wup_ref, a0_ref, aup_ref, gup_ref,
                 kk_ref, ka_ref, rk_ref, lnw_ref, lnb_ref, hm_ref, tri_ref, out_ref, sout_ref,
                 carry_ref, state_ref, r_ref, k_ref, v_ref, lw_ref, a_ref, b_ref, rh_ref, y_ref, m_ref, n_ref, et_ref):
    ti = pl.program_id(1)
    tt = pr_ref.shape[1]
    w_ = RWKV_WIDTH
    hd = RWKV_HEAD

    @pl.when(ti == 0)
    def _():
        carry_ref[...] = prev_ref[0]
        state_ref[...] = s0_ref[0]

    p = pr_ref[0]
    row = lax.broadcasted_iota(jnp.int32, (tt, 1), 0)
    shifted = jnp.where(row == 0, carry_ref[...], pltpu.roll(p, 1, 0))
    carry_ref[...] = p[tt - 1:tt]
    xs = p + (shifted - p) * mu_ref[...]
    r, k, v = xs[:, :w_], xs[:, w_:2 * w_], xs[:, 2 * w_:3 * w_]
    wa = xs[:, 3 * w_:3 * w_ + DECAY_LORA + AAA_LORA]
    gl = xs[:, 3 * w_ + DECAY_LORA + AAA_LORA:]
    hm = hm_ref[...]
    w = -_softplus(-(w0_ref[...] + _dot(jnp.tanh(wa).astype(BF16), wup_ref[...]))) - 0.5
    lw = -jnp.exp(w)
    a = jax.nn.sigmoid(a0_ref[...] + _dot(wa.astype(BF16), aup_ref[...]))
    g = _dot(jax.nn.sigmoid(gl).astype(BF16), gup_ref[...])
    kk = k * kk_ref[...]
    kk = kk / jnp.maximum(jnp.sqrt(_head_sum(kk * kk, hm)), 1e-12)
    k = k * (1.0 + (a - 1.0) * ka_ref[...])
    av, bv = -kk, kk * a
    if n_valid < tt:
        keep = row < n_valid
        lw, k, v, av, bv = (jnp.where(keep, t, 0.0) for t in (lw, k, v, av, bv))
    r_ref[...], k_ref[...], v_ref[...], lw_ref[...], a_ref[...], b_ref[...] = r, k, v, lw, av, bv

    ci = lax.broadcasted_iota(jnp.int32, (chunk, chunk), 0)
    cj = lax.broadcasted_iota(jnp.int32, (chunk, chunk), 1)
    strict, incl = ci > cj, ci >= cj
    dot = functools.partial(_bdot, _dot)
    dot_nt = functools.partial(_bdot, _dot_nt)
    dot_tn = functools.partial(_bdot, _dot_tn)

    def prepare(c, carry):
        rows = pl.ds(pl.multiple_of(c * chunk, chunk), chunk)
        lwc = lw_ref[rows, :]
        cs = _dot(tri_ref[...], lwc, precision=HIGHEST)
        e_in, e_out = jnp.exp(cs), jnp.exp(-cs)
        total = cs[chunk - 1:chunk]
        e_end = jnp.exp(total - cs)
        rt = r_ref[rows, :] * e_in
        at = a_ref[rows, :] * jnp.exp(cs - lwc)
        kc, bc, vc = k_ref[rows, :], b_ref[rows, :], v_ref[rows, :]
        kt, bt = kc * e_out, bc * e_out
        kh, bh = kc * e_end, bc * e_end
        et_ref[c] = jnp.broadcast_to(jnp.exp(total), et_ref.shape[1:])
        heads = range(RWKV_HEADS)
        sls = [slice(h * hd, (h + 1) * hd) for h in heads]
        gm = [dot_nt(jnp.concatenate([at[:, sl], rt[:, sl]], axis=0),
                     jnp.concatenate([bt[:, sl], kt[:, sl]], axis=0)) for sl in sls]
        lab = [jnp.where(strict, g_[:chunk, :chunk], 0.0) for g_ in gm]
        lak = [jnp.where(strict, g_[:chunk, chunk:], 0.0) for g_ in gm]
        mrb = [jnp.where(incl, g_[chunk:, :chunk], 0.0) for g_ in gm]
        mrk = [jnp.where(incl, g_[chunk:, chunk:], 0.0) for g_ in gm]
        vh = [vc[:, sl] for sl in sls]
        wm = [jnp.concatenate([at[:, sl], dot(lak[h], vh[h])], axis=1) for h, sl in enumerate(sls)]
        x, n = lab, 1
        while True:
            wm = [wm[h] + dot(x[h], wm[h]) for h in heads]
            n *= 2
            if n >= chunk:
                break
            x = [dot(x[h], x[h]) for h in heads]
        mw = [dot(mrb[h], wm[h]) for h in heads]
        y0 = [mw[h][:, hd:] + dot(mrk[h], vh[h]) for h in heads]
        wb = [dot_tn(wm[h], bh[:, sl]) for h, sl in enumerate(sls)]
        vk = [dot_tn(vh[h], kh[:, sl]) for h, sl in enumerate(sls)]
        for h, sl in enumerate(sls):
            rh_ref[rows, sl] = rt[:, sl] + mw[h][:, :hd]
            y_ref[rows, sl] = y0[h]
            m_ref[c, h] = wb[h][:hd]
            n_ref[c, h] = wb[h][hd:] + vk[h]
        return carry

    def advance(c, carry):
        rows = pl.ds(pl.multiple_of(c * chunk, chunk), chunk)
        e_tot = et_ref[c]
        sls = [slice(h * hd, (h + 1) * hd) for h in range(RWKV_HEADS)]
        s0 = [state_ref[h] for h in range(RWKV_HEADS)]
        ys = [dot_nt(rh_ref[rows, sl], s0[h]) for h, sl in enumerate(sls)]
        sm = [dot(s0[h], m_ref[c, h]) for h in range(RWKV_HEADS)]
        for h, sl in enumerate(sls):
            y_ref[rows, sl] += ys[h]
            state_ref[h] = s0[h] * e_tot[0:1, sl] + sm[h] + n_ref[c, h]
        return carry

    lax.fori_loop(0, tt // chunk, prepare, 0)
    lax.fori_loop(0, tt // chunk, advance, 0)

    y = y_ref[...]
    inv_n = 1.0 / hd
    mean = _head_sum(y, hm) * inv_n
    d = y - mean
    var = _head_sum(d * d, hm) * inv_n
    y = d * lax.rsqrt(var + LNX_EPS) * lnw_ref[...] + lnb_ref[...]
    y = y + _head_sum(r_ref[...] * k_ref[...] * rk_ref[...], hm) * v_ref[...]
    out_ref[0] = (y * g).astype(out_ref.dtype)
    sout_ref[0] = state_ref[...]


def _rwkv(pr3, prev, s0, rw, chunk, tt, n_valid):
    b, t, _ = pr3.shape
    hm = jnp.asarray(np.kron(np.eye(RWKV_HEADS), np.ones((RWKV_HEAD, RWKV_HEAD))), dtype=BF16)
    tri = jnp.asarray(np.tril(np.ones((chunk, chunk))), dtype=F32)
    vec = lambda a: pl.BlockSpec(a.shape, lambda i, j: (0, 0))
    prev3 = prev[:, None, :]
    names = ("mu", "w0", "wup", "a0", "aup", "gup", "kk", "ka", "rk", "lnw", "lnb")
    kern = functools.partial(_rwkv_kernel, chunk, n_valid)
    per_chunk = (tt // chunk, RWKV_HEADS, RWKV_HEAD, RWKV_HEAD)
    return pl.pallas_call(
        kern, grid=(b, t // tt),
        in_specs=[pl.BlockSpec((1, tt, RWKV_PROJ), lambda i, j: (i, j, 0)),
                  pl.BlockSpec((1, 1, RWKV_PROJ), lambda i, j: (i, 0, 0)),
                  pl.BlockSpec((1, RWKV_HEADS, RWKV_HEAD, RWKV_HEAD), lambda i, j: (i, 0, 0, 0))] +
                 [vec(rw[n]) for n in names] + [vec(hm), vec(tri)],
        out_specs=[pl.BlockSpec((1, tt, RWKV_WIDTH), lambda i, j: (i, j, 0)),
                   pl.BlockSpec((1, RWKV_HEADS, RWKV_HEAD, RWKV_HEAD), lambda i, j: (i, 0, 0, 0))],
        out_shape=(jax.ShapeDtypeStruct((b, t, RWKV_WIDTH), BF16),
                   jax.ShapeDtypeStruct((b, RWKV_HEADS, RWKV_HEAD, RWKV_HEAD), F32)),
        scratch_shapes=[pltpu.VMEM((1, RWKV_PROJ), F32), pltpu.VMEM((RWKV_HEADS, RWKV_HEAD, RWKV_HEAD), F32)] +
                       [pltpu.VMEM((tt, RWKV_WIDTH), F32) for _ in range(8)] +
                       [pltpu.VMEM(per_chunk, F32), pltpu.VMEM(per_chunk, F32),
                        pltpu.VMEM((tt // chunk, 8, RWKV_WIDTH), F32)],
        compiler_params=_params("arbitrary", "arbitrary"), name="rwkv",
    )(pr3, prev3, s0, *[rw[n] for n in names], hm, tri)


def _ffn_head(x_ref, mix_ref, wo_ref, gpost_ref, gpre_ref, x1_ref, h2_ref, acc_ref):
    x1 = x_ref[...] + _rms(_dot(mix_ref[...], wo_ref[...]), gpost_ref[...])
    x1_ref[...] = x1
    h2_ref[...] = _rms(x1, gpre_ref[...]).astype(BF16)
    acc_ref[...] = jnp.zeros(acc_ref.shape, F32)


def _ffn_tail(gate2, gate1, gate, val, cw_ref, cb_ref, wd_ref, acc_ref):
    conv = cb_ref[...] + (gate2 * cw_ref[0:1, :] + gate1 * cw_ref[1:2, :] + gate * cw_ref[2:3, :])
    acc_ref[...] += _dot((jax.nn.gelu(conv) * val).astype(BF16), wd_ref[...])


def _ffn_seq_kernel(tiles_per_seq, x_ref, mix_ref, wo_ref, gpost_ref, gpre_ref, gfpost_ref, wg_ref, wv_ref, cw_ref,
                    cb_ref, wd_ref, out_ref, tail_ref, x1_ref, h2_ref, acc_ref, carry_ref):
    i, j = pl.program_id(0), pl.program_id(1)
    tm = x_ref.shape[0]

    @pl.when(j == 0)
    def _():
        _ffn_head(x_ref, mix_ref, wo_ref, gpost_ref, gpre_ref, x1_ref, h2_ref, acc_ref)

    h2 = h2_ref[...]
    gate, val = _dot(h2, wg_ref[...]), _dot(h2, wv_ref[...])
    @pl.when(i % tiles_per_seq == 0)
    def _():
        carry_ref[j] = jnp.zeros(carry_ref.shape[1:], F32)

    prev = carry_ref[j]
    row = lax.broadcasted_iota(jnp.int32, (tm, 1), 0)
    gate1 = jnp.where(row == 0, prev[7:8], pltpu.roll(gate, 1, 0))
    gate2 = jnp.where(row == 0, prev[6:7], jnp.where(row == 1, prev[7:8], pltpu.roll(gate, 2, 0)))
    carry_ref[j] = gate[tm - 8:tm]
    tail_ref[0] = gate[tm - 8:tm]
    _ffn_tail(gate2, gate1, gate, val, cw_ref, cb_ref, wd_ref, acc_ref)

    @pl.when(j == pl.num_programs(1) - 1)
    def _():
        out_ref[...] = x1_ref[...] + _rms(acc_ref[...], gfpost_ref[...])


def _ffn_step_kernel(x_ref, mix_ref, wo_ref, gpost_ref, gpre_ref, gfpost_ref, wg_ref, wv_ref, cw_ref, cb_ref,
                     wd_ref, prev_ref, out_ref, gate_ref, x1_ref, h2_ref, acc_ref):
    j = pl.program_id(0)

    @pl.when(j == 0)
    def _():
        _ffn_head(x_ref, mix_ref, wo_ref, gpost_ref, gpre_ref, x1_ref, h2_ref, acc_ref)

    h2 = h2_ref[...]
    gate, val = _dot(h2, wg_ref[...]), _dot(h2, wv_ref[...])
    gate_ref[...] = gate
    _ffn_tail(prev_ref[0], prev_ref[1], gate, val, cw_ref, cb_ref, wd_ref, acc_ref)

    @pl.when(j == pl.num_programs(0) - 1)
    def _():
        out_ref[...] = x1_ref[...] + _rms(acc_ref[...], gfpost_ref[...])


def _ffn_seq(x2, mix2, fw, seq_len, tm, tf):
    m, d = x2.shape
    d_ff = fw["wd"].shape[0]
    nf = d_ff // tf
    row = lambda i, j: (i, 0)
    full = lambda i, j: (0, 0)
    tiles_per_seq = seq_len // tm
    scratch = [pltpu.VMEM((tm, d), F32), pltpu.VMEM((tm, d), BF16), pltpu.VMEM((tm, d), F32),
               pltpu.VMEM((nf, 8, tf), F32)]
    return pl.pallas_call(
        functools.partial(_ffn_seq_kernel, tiles_per_seq), grid=(m // tm, nf),
        in_specs=[pl.BlockSpec((tm, d), row), pl.BlockSpec((tm, d), row), pl.BlockSpec((d, d), full),
                  pl.BlockSpec((1, d), full), pl.BlockSpec((1, d), full), pl.BlockSpec((1, d), full),
                  pl.BlockSpec((d, tf), lambda i, j: (0, j)), pl.BlockSpec((d, tf), lambda i, j: (0, nf + j)),
                  pl.BlockSpec((CONV_WIDTH, tf), lambda i, j: (0, j)), pl.BlockSpec((1, tf), lambda i, j: (0, j)),
                  pl.BlockSpec((tf, d), lambda i, j: (j, 0))],
        out_specs=[pl.BlockSpec((tm, d), row), pl.BlockSpec((1, 8, tf), lambda i, j: (i, 0, j))],
        out_shape=(jax.ShapeDtypeStruct((m, d), F32), jax.ShapeDtypeStruct((m // tm, 8, d_ff), F32)),
        scratch_shapes=scratch, compiler_params=_params("arbitrary", "arbitrary"), name="ffn_seq",
    )(x2, mix2, fw["wo"], fw["gpost"], fw["gpre"], fw["gfpost"], fw["wup"], fw["wup"], fw["cw"], fw["cb"], fw["wd"])


def _ffn_step(x2, mix2, fw, prev, tf):
    m, d = x2.shape
    d_ff = fw["wd"].shape[0]
    nf = d_ff // tf
    full = lambda j: (0, 0)
    scratch = [pltpu.VMEM((m, d), F32), pltpu.VMEM((m, d), BF16), pltpu.VMEM((m, d), F32)]
    return pl.pallas_call(
        _ffn_step_kernel, grid=(nf,),
        in_specs=[pl.BlockSpec((m, d), full), pl.BlockSpec((m, d), full), pl.BlockSpec((d, d), full),
                  pl.BlockSpec((1, d), full), pl.BlockSpec((1, d), full), pl.BlockSpec((1, d), full),
                  pl.BlockSpec((d, tf), lambda j: (0, j)), pl.BlockSpec((d, tf), lambda j: (0, nf + j)),
                  pl.BlockSpec((CONV_WIDTH, tf), lambda j: (0, j)), pl.BlockSpec((1, tf), lambda j: (0, j)),
                  pl.BlockSpec((tf, d), lambda j: (j, 0)),
                  pl.BlockSpec((CONV_WIDTH - 1, m, tf), lambda j: (0, 0, j))],
        out_specs=[pl.BlockSpec((m, d), full), pl.BlockSpec((m, tf), lambda j: (0, j))],
        out_shape=(jax.ShapeDtypeStruct((m, d), F32), jax.ShapeDtypeStruct((m, d_ff), F32)),
        scratch_shapes=scratch, compiler_params=_params("arbitrary"), name="ffn_step",
    )(x2, mix2, fw["wo"], fw["gpost"], fw["gpre"], fw["gfpost"], fw["wup"], fw["wup"], fw["cw"], fw["cb"],
      fw["wd"], prev)


def _layer_weights(l, p):
    w_in = p["w_in"][l]
    o = NSA_WIDTH + 6 * KV_WIDTH
    wg = jnp.pad(w_in[:, o:NSA_PROJ], ((0, 0), (0, LANES - N_BRANCH * NSA_HEADS)))
    row = lambda a: a[l][None, :].astype(F32)
    zpad = jnp.zeros((AAA_LORA, RWKV_WIDTH), F32)
    rw = {"mu": row(p["rwkv_mu"]), "w0": row(p["rwkv_w0"]),
          "wup": jnp.concatenate([p["rwkv_w_up"][l], zpad], axis=0).astype(BF16),
          "a0": row(p["rwkv_a0"]), "aup": jnp.concatenate([zpad, p["rwkv_a_up"][l]], axis=0).astype(BF16),
          "gup": p["rwkv_g_up"][l].astype(BF16), "kk": row(p["rwkv_k_k"]), "ka": row(p["rwkv_k_a"]),
          "rk": p["rwkv_r_k"][l].reshape(1, RWKV_WIDTH), "lnw": row(p["rwkv_lnx_w"]), "lnb": row(p["rwkv_lnx_b"])}
    fw = {"wo": p["w_out"][l].astype(BF16), "gpost": row(p["norm_mix_post"]), "gpre": row(p["norm_ffn_pre"]),
          "gfpost": row(p["norm_ffn_post"]), "wup": p["ffn_w_up"][l].astype(BF16), "cw": p["ffn_conv_w"][l],
          "cb": row(p["ffn_conv_b"]), "wd": p["ffn_w_down"][l].astype(BF16)}
    return {"g_in": row(p["norm_mix_pre"]), "wq": w_in[:, :NSA_WIDTH].astype(BF16),
            "wkv": w_in[:, NSA_WIDTH:o].astype(BF16), "wg": wg.astype(BF16), "wr": w_in[:, NSA_PROJ:].astype(BF16),
            "cw": _compress_weights(p["cmp_pe"][l], p["cmp_w1"][l], p["cmp_w2"][l]), "rw": rw, "fw": fw}


def _prompt_layer(x, lw, tabs):
    b, t, d = x.shape
    m = b * t
    x2 = x.reshape(m, d)
    q, kvw, gates, pr = _norm_proj(x2, lw["g_in"], lw["wq"], lw["wkv"], lw["wg"], lw["wr"], tabs, tm=256)
    kvw3 = kvw.reshape(b, t, 6 * KV_WIDTH)
    kvc = _compress_prompt(kvw3, lw["cw"])
    cvh = kvc.astype(BF16).reshape(b, -1, 2, KV_HEADS, HEAD_DIM).transpose(0, 2, 3, 1, 4)
    qh = (q * ATTN_SCALE).astype(BF16).reshape(b, t, NSA_HEADS, HEAD_DIM).transpose(0, 2, 1, 3)
    kvh = kvw3[:, :, 2 * KV_WIDTH:].astype(BF16).reshape(b, t, 4, KV_HEADS, HEAD_DIM).transpose(0, 2, 3, 1, 4)
    gates_h = gates[:, :N_BRANCH * NSA_HEADS].reshape(b, t, NSA_HEADS, N_BRANCH).transpose(0, 2, 1, 3)
    o_nsa = _nsa_prompt(qh, kvh, cvh, gates_h).transpose(0, 2, 1, 3).reshape(m, NSA_WIDTH)
    s0 = jnp.zeros((b, RWKV_HEADS, RWKV_HEAD, RWKV_HEAD), F32)
    pr3 = pr.reshape(b, t, RWKV_PROJ)
    o_rwkv, wkv = _rwkv(pr3, jnp.zeros((b, RWKV_PROJ), F32), s0, lw["rw"], chunk=64, tt=min(t, 512), n_valid=t)
    mixed = jnp.concatenate([o_nsa, o_rwkv.reshape(m, RWKV_WIDTH)], axis=1)
    d_ff = lw["fw"]["wd"].shape[0]
    tm = min(t, 512)
    y, tail = _ffn_seq(x2, mixed, lw["fw"], t, tm=tm, tf=d_ff // 2)
    tail = tail[t // tm - 1::t // tm]
    kv_rows = kvw3[:, :, :4 * KV_WIDTH].reshape(b, t, 4, KV_HEADS, HEAD_DIM)
    wn = min(WINDOW, t)
    win = kvw3[:, t - wn:, 4 * KV_WIDTH:].reshape(b, wn, 2, KV_HEADS, HEAD_DIM)
    return y.reshape(b, t, d), kv_rows, win, wkv, pr3[:, -1], tail[:, 8 - (CONV_WIDTH - 1):]


def _sample_layer(x, lw, tabs, l, cache_t, page_table, st_win, st_wkv, st_shift, st_conv):
    db, tn, d = x.shape
    past_len = page_table.shape[1] * cache_t.shape[3]
    x2 = x.reshape(db, d)
    q, kvw, gates, pr = _norm_proj(x2, lw["g_in"], lw["wq"], lw["wkv"], lw["wg"], lw["wr"], tabs, tm=db)
    qs = (q * ATTN_SCALE).astype(BF16).reshape(db, KV_HEADS, HEADS_PER_KV, HEAD_DIM)
    eye = jnp.eye(KV_HEADS, dtype=BF16)
    qz = jnp.einsum('bghd,gj->bghjd', qs, eye).reshape(db, KV_HEADS, HEADS_PER_KV, KV_WIDTH)
    qz = jnp.pad(qz, ((0, 0), (0, 0), (0, 8 - HEADS_PER_KV), (0, 0))).reshape(db, SROWS, KV_WIDTH)
    kvc = _compress_sample(cache_t, page_table, l, lw["cw"])
    n_cmp = past_len // CMP_STRIDE - 1
    n_sel = -(-(past_len + tn) // SEL_BLOCK)
    n_sel_pad = -(-n_sel // LANES) * LANES
    c2s = jnp.asarray(_cmp_to_sel(kvc.shape[1], n_cmp, n_sel_pad, n_sel))
    wb = st_win.shape[1]
    win = st_win.transpose(0, 2, 3, 4, 1).reshape(db, 2 * KV_WIDTH, wb)
    wnew = kvw[:, None, 4 * KV_WIDTH:]
    ocmp, owin, imp = _sample_cmp_win(past_len, n_cmp, qz, kvc, win, wnew, c2s)
    sel = _select(imp.reshape(db * 8, n_sel_pad), n_sel, past_len // SEL_BLOCK).reshape(db, 8, n_sel_pad)
    sel16 = jnp.broadcast_to(sel[:, :KV_HEADS, None, :], (db, KV_HEADS, 8, n_sel_pad)).reshape(db, SROWS, n_sel_pad)
    gz = gates[:, :N_BRANCH * NSA_HEADS].reshape(db, KV_HEADS, HEADS_PER_KV, N_BRANCH)
    gz = jnp.pad(gz, ((0, 0), (0, 0), (0, 8 - HEADS_PER_KV), (0, 0))).reshape(db, SROWS, N_BRANCH)
    knew = kvw[:, None, 2 * KV_WIDTH:4 * KV_WIDTH]
    o16 = _sample_sel(cache_t, page_table, l, qz, sel16, knew, ocmp, owin, gz)
    o4 = o16.reshape(db, KV_HEADS, 8, KV_HEADS, HEAD_DIM)[:, :, :HEADS_PER_KV]
    o_nsa = jnp.stack([o4[:, g, :, g] for g in range(KV_HEADS)], axis=1).reshape(db, NSA_WIDTH)
    pad_t = 8
    pr3 = jnp.pad(pr[:, None, :], ((0, 0), (0, pad_t - tn), (0, 0)))
    o_rwkv, wkv = _rwkv(pr3, st_shift, st_wkv, lw["rw"], chunk=pad_t, tt=pad_t, n_valid=tn)
    mixed = jnp.concatenate([o_nsa.astype(BF16), o_rwkv[:, 0]], axis=1)
    prev = st_conv.transpose(1, 0, 2)
    d_ff = lw["fw"]["wd"].shape[0]
    y, gate = _ffn_step(x2, mixed, lw["fw"], prev, tf=d_ff // 2)
    kv_rows = kvw[:, :4 * KV_WIDTH].reshape(db, tn, 4, KV_HEADS, HEAD_DIM)
    win_new = jnp.concatenate([st_win[:, tn:], kvw[:, 4 * KV_WIDTH:].reshape(db, tn, 2, KV_HEADS, HEAD_DIM)], axis=1)
    conv_new = jnp.concatenate([st_conv[:, 1:], gate[:, None, :]], axis=1)
    return y.reshape(db, tn, d), kv_rows, win_new, wkv, pr, conv_new


def kernel(x_prompt, x_sample, cache_kv, page_table, state_win, state_wkv, state_shift, state_conv, norm_mix_pre, norm_mix_post, norm_ffn_pre, norm_ffn_post, w_in, w_out, cmp_pe, cmp_w1, cmp_w2, rwkv_mu, rwkv_w0, rwkv_w_up, rwkv_a0, rwkv_a_up, rwkv_g_up, rwkv_k_k, rwkv_k_a, rwkv_r_k, rwkv_lnx_w, rwkv_lnx_b, ffn_w_up, ffn_conv_w, ffn_conv_b, ffn_w_down):
    p = dict(norm_mix_pre=norm_mix_pre, norm_mix_post=norm_mix_post, norm_ffn_pre=norm_ffn_pre,
             norm_ffn_post=norm_ffn_post, w_in=w_in, w_out=w_out, cmp_pe=cmp_pe, cmp_w1=cmp_w1, cmp_w2=cmp_w2,
             rwkv_mu=rwkv_mu, rwkv_w0=rwkv_w0, rwkv_w_up=rwkv_w_up, rwkv_a0=rwkv_a0, rwkv_a_up=rwkv_a_up,
             rwkv_g_up=rwkv_g_up, rwkv_k_k=rwkv_k_k, rwkv_k_a=rwkv_k_a, rwkv_r_k=rwkv_r_k,
             rwkv_lnx_w=rwkv_lnx_w, rwkv_lnx_b=rwkv_lnx_b, ffn_w_up=ffn_w_up, ffn_conv_w=ffn_conv_w,
             ffn_conv_b=ffn_conv_b, ffn_w_down=ffn_w_down)
    depth = w_in.shape[0]
    b, t, _ = x_prompt.shape
    db, tn, _ = x_sample.shape
    n_pool, page = cache_kv.shape[1], cache_kv.shape[2]
    past_len = page_table.shape[1] * page
    cache_t = cache_kv.transpose(0, 1, 3, 4, 5, 2).reshape(depth, n_pool, 2 * CMP_LANES, page)
    tabs_p = _rope_tables(jnp.arange(t, dtype=jnp.int32))
    tabs_s = _rope_tables(jnp.full((db,), past_len, jnp.int32))
    xp, xs = x_prompt, x_sample
    outs_p, outs_s = [], []
    for l in range(depth):
        lw = _layer_weights(l, p)
        xp, *st_p = _prompt_layer(xp, lw, tabs_p)
        xs, *st_s = _sample_layer(xs, lw, tabs_s, l, cache_t, page_table, state_win[l], state_wkv[l],
                                  state_shift[l], state_conv[l])
        outs_p.append(st_p)
        outs_s.append(st_s)
    stack = lambda outs, i: jnp.stack([o[i] for o in outs])
    return (xp, xs, stack(outs_p, 0), stack(outs_s, 0), stack(outs_p, 1), stack(outs_s, 1),
            stack(outs_p, 2), stack(outs_s, 2), stack(outs_p, 3), stack(outs_s, 3),
            stack(outs_p, 4), stack(outs_s, 4))
```

```python
import functools

import jax
import jax.numpy as jnp
import numpy as np
from jax import lax
from jax.experimental import pallas as pl
from jax.experimental.pallas import tpu as pltpu

F32, BF16 = jnp.float32, jnp.bfloat16
HIGHEST = lax.Precision.HIGHEST

LANES = 128
VMEM_LIMIT = 56 * 1024 * 1024

HEAD_DIM = 64
NSA_HEADS = 8
KV_HEADS = 2
HEADS_PER_KV = NSA_HEADS // KV_HEADS
NSA_WIDTH = NSA_HEADS * HEAD_DIM
KV_WIDTH = KV_HEADS * HEAD_DIM
N_BRANCH = 3
CMP_STRIDE = 16
CMP_BLOCK = 2 * CMP_STRIDE
SEL_BLOCK = 64
SEL_TOPN = 16
WINDOW = 512
ROPE_THETA = 500000.0
ROPE_DIM = HEAD_DIM // 4
RWKV_HEAD = 64
RWKV_HEADS = 8
RWKV_WIDTH = RWKV_HEADS * RWKV_HEAD
DECAY_LORA = 64
AAA_LORA = 64
GATE_LORA = 128
RWKV_PROJ = 3 * RWKV_WIDTH + DECAY_LORA + AAA_LORA + GATE_LORA
NSA_PROJ = NSA_WIDTH + 6 * KV_WIDTH + N_BRANCH * NSA_HEADS
CONV_WIDTH = 3
NORM_EPS = 1e-6
LNX_EPS = 64e-5
NEG = -1e30
FORCED_SCORE = 1e9
ATTN_SCALE = HEAD_DIM ** -0.5
LOG2E = 1.4426950408889634


def _dot(a, b, precision=None):
    return jnp.dot(a, b, preferred_element_type=F32, precision=precision)


def _dot_nt(a, b, precision=None):
    return lax.dot_general(a, b, (((1,), (1,)), ((), ())), preferred_element_type=F32, precision=precision)


def _dot_tn(a, b, precision=None):
    return lax.dot_general(a, b, (((0,), (0,)), ((), ())), preferred_element_type=F32, precision=precision)


def _rms(x, g):
    return x * lax.rsqrt(jnp.mean(x * x, axis=-1, keepdims=True) + NORM_EPS) * g


def _masked_softmax(s, mask):
    s = jnp.where(mask, s, NEG)
    m = jnp.max(s, axis=-1, keepdims=True)
    e = jnp.where(mask, jnp.exp2(s - m), 0.0)
    return e / jnp.maximum(jnp.sum(e, axis=-1, keepdims=True), 1e-30)


def _params(*sem):
    return pltpu.CompilerParams(dimension_semantics=sem, vmem_limit_bytes=VMEM_LIMIT)


def _norm_proj_kernel(x_ref, g_ref, wq_ref, wkv_ref, wg_ref, wr_ref, cos_ref, sa_ref, sb_ref,
                      q_ref, kvw_ref, gate_ref, pr_ref):
    h = _rms(x_ref[...], g_ref[...]).astype(BF16)
    cos, sa, sb = cos_ref[...], sa_ref[...], sb_ref[...]

    def rope(v):
        return v * cos + pltpu.roll(v, LANES - ROPE_DIM // 2, 1) * sa + pltpu.roll(v, ROPE_DIM // 2, 1) * sb

    q = _dot(h, wq_ref[...])
    for j in range(NSA_WIDTH // LANES):
        q_ref[:, j * LANES:(j + 1) * LANES] = rope(q[:, j * LANES:(j + 1) * LANES])
    kv = _dot(h, wkv_ref[...])
    for j in range(6 * KV_WIDTH // LANES):
        blk = kv[:, j * LANES:(j + 1) * LANES]
        kvw_ref[:, j * LANES:(j + 1) * LANES] = rope(blk) if j % 2 == 0 else blk
    gate_ref[...] = jax.nn.sigmoid(_dot(h, wg_ref[...]))
    pr_ref[...] = _dot(h, wr_ref[...])


def _norm_proj(x2, g, wq, wkv, wg, wr, tabs, tm):
    m, d = x2.shape
    cos, sa, sb = tabs
    tab_blocks = cos.shape[0] // tm
    row = lambda i: (i, 0)
    full = lambda i: (0, 0)
    tab = lambda i: (i % tab_blocks, 0)
    outs = (jax.ShapeDtypeStruct((m, NSA_WIDTH), F32), jax.ShapeDtypeStruct((m, 6 * KV_WIDTH), F32),
            jax.ShapeDtypeStruct((m, LANES), F32), jax.ShapeDtypeStruct((m, RWKV_PROJ), F32))
    return pl.pallas_call(
        _norm_proj_kernel, grid=(m // tm,),
        in_specs=[pl.BlockSpec((tm, d), row), pl.BlockSpec((1, d), full),
                  pl.BlockSpec(wq.shape, full), pl.BlockSpec(wkv.shape, full),
                  pl.BlockSpec(wg.shape, full), pl.BlockSpec(wr.shape, full),
                  pl.BlockSpec((tm, LANES), tab), pl.BlockSpec((tm, LANES), tab), pl.BlockSpec((tm, LANES), tab)],
        out_specs=[pl.BlockSpec((tm, NSA_WIDTH), row), pl.BlockSpec((tm, 6 * KV_WIDTH), row),
                   pl.BlockSpec((tm, LANES), row), pl.BlockSpec((tm, RWKV_PROJ), row)],
        out_shape=outs, compiler_params=_params("arbitrary"), name="norm_proj",
    )(x2, g, wq, wkv, wg, wr, cos, sa, sb)


def _rope_tables(pos):
    half = ROPE_DIM // 2
    inv_freq = ROPE_THETA ** (-jnp.arange(half, dtype=F32) / half)
    ang = pos.astype(F32)[:, None] * inv_freq[None, :]
    c, s = jnp.cos(ang), jnp.sin(ang)
    n = pos.shape[0]
    one = jnp.ones((n, HEAD_DIM - ROPE_DIM), F32)
    zero = jnp.zeros((n, HEAD_DIM - ROPE_DIM), F32)
    zh = jnp.zeros((n, half), F32)
    cos = jnp.concatenate([c, c, one], axis=1)
    sa = jnp.concatenate([-s, zh, zero], axis=1)
    sb = jnp.concatenate([zh, s, zero], axis=1)
    rep = LANES // HEAD_DIM
    return tuple(jnp.tile(t, (1, rep)) for t in (cos, sa, sb))


CMP_LANES = 2 * KV_WIDTH


def _compress_accumulate(position_rows, n_chunk, pea_ref, peb_ref, wa_ref, wb_ref, acca_ref, accb_ref, row0):
    for c in range(2):
        lanes = slice(c * KV_WIDTH, (c + 1) * KV_WIDTH)
        da = db = None
        for l in range(CMP_STRIDE):
            x = position_rows(c, l)
            xa = (x + pea_ref[c, l:l + 1, :]).astype(BF16)
            xb = (x + peb_ref[c, l:l + 1, :]).astype(BF16)
            ta, tb = _dot(xa, wa_ref[c, l]), _dot(xb, wb_ref[c, l])
            da, db = (ta, tb) if l == 0 else (da + ta, db + tb)
        acca_ref[pl.ds(row0, n_chunk), lanes] = da
        accb_ref[pl.ds(row0, n_chunk), lanes] = db


def _compress_finish(acca_ref, accb_ref, w2_ref, out_ref):
    n = acca_ref.shape[0]
    rows = lax.broadcasted_iota(jnp.int32, (n, 1), 0)
    for c in range(2):
        lanes = slice(c * KV_WIDTH, (c + 1) * KV_WIDTH)
        nxt = pltpu.roll(accb_ref[:, lanes], n - 1, 0)
        h = jax.nn.gelu(acca_ref[:, lanes] + nxt).astype(BF16)
        out_ref[0, :, lanes] = jnp.where(rows < n - 1, _dot(h, w2_ref[c]), 0.0)


def _compress_prompt_kernel(k_ref, v_ref, pea_ref, peb_ref, wa_ref, wb_ref, w2_ref, out_ref, acca_ref, accb_ref):
    n = acca_ref.shape[0]
    refs = (k_ref.at[0], v_ref.at[0])
    strided = lambda c, l: refs[c][pl.ds(l, n, stride=CMP_STRIDE), :]
    _compress_accumulate(strided, n, pea_ref, peb_ref, wa_ref, wb_ref, acca_ref, accb_ref, 0)
    _compress_finish(acca_ref, accb_ref, w2_ref, out_ref)


_CW_NAMES = ("pea", "peb", "wa", "wb", "w2")


def _compress_prompt(kvw3, cw):
    b, t, _ = kvw3.shape
    n = t // CMP_STRIDE
    const = lambda a: pl.BlockSpec(a.shape, lambda i: (0,) * a.ndim)
    return pl.pallas_call(
        _compress_prompt_kernel, grid=(b,),
        in_specs=[pl.BlockSpec((1, t, KV_WIDTH), lambda i: (i, 0, 0)),
                  pl.BlockSpec((1, t, KV_WIDTH), lambda i: (i, 0, 1))] + [const(cw[k]) for k in _CW_NAMES],
        out_specs=pl.BlockSpec((1, n, CMP_LANES), lambda i: (i, 0, 0)),
        out_shape=jax.ShapeDtypeStruct((b, n, CMP_LANES), F32),
        scratch_shapes=[pltpu.VMEM((n, CMP_LANES), F32), pltpu.VMEM((n, CMP_LANES), F32)],
        compiler_params=_params("arbitrary"), name="compress_prompt",
    )(kvw3, kvw3, *[cw[k] for k in _CW_NAMES])


def _compress_weights(pe, w1, w2):
    eye = jnp.eye(KV_HEADS, dtype=F32)
    blockdiag = lambda w: jnp.einsum('...de,gj->...gdje', w, eye).reshape(w.shape[:-2] + (KV_WIDTH, KV_WIDTH))
    lanes = lambda p: jnp.tile(p, (1, 1, KV_HEADS))
    return {"pea": lanes(pe[:, :CMP_STRIDE]), "peb": lanes(pe[:, CMP_STRIDE:]),
            "wa": blockdiag(w1[:, :CMP_STRIDE]).astype(BF16), "wb": blockdiag(w1[:, CMP_STRIDE:]).astype(BF16),
            "w2": blockdiag(w2).astype(BF16)}


def _paged_fetch(page_copies, pages_per_step, steps_per_seq):
    step = pl.program_id(0) * steps_per_seq + pl.program_id(1)
    total = pl.num_programs(0) * steps_per_seq

    def start_all(s):
        def body(j, c):
            for cp in page_copies(s, j):
                cp.start()
            return c
        lax.fori_loop(0, pages_per_step, body, 0)

    @pl.when(step == 0)
    def _():
        start_all(step)

    @pl.when(step + 1 < total)
    def _():
        start_all(step + 1)

    def wait_body(j, c):
        for cp in page_copies(step, j):
            cp.wait()
        return c
    lax.fori_loop(0, pages_per_step, wait_body, 0)
    return step % 2


UNROLL_PAGES = 4


def _page_of(pt_ref, step, j, pages_per_step, steps_per_seq):
    return pt_ref[step // steps_per_seq, (step % steps_per_seq) * pages_per_step + j]


def _compress_sample_kernel(layer, pages_per_step, steps_per_seq, pt_ref, cache_ref, pea_ref, peb_ref, wa_ref,
                            wb_ref, w2_ref, out_ref, buf_ref, sem_ref, rows_ref, acca_ref, accb_ref):
    page_len = cache_ref.shape[3]

    def page_copies(step, j):
        page = _page_of(pt_ref, step, j, pages_per_step, steps_per_seq)
        return [pltpu.make_async_copy(cache_ref.at[layer, page, pl.ds(0, CMP_LANES), :], buf_ref.at[step % 2, j],
                                      sem_ref.at[step % 2])]

    slot = _paged_fetch(page_copies, pages_per_step, steps_per_seq)

    chunks = page_len // CMP_STRIDE

    def to_rows(j, carry):
        c0 = pl.multiple_of(j * chunks, chunks)
        for c in range(2):
            tok = buf_ref[slot, j, c * KV_WIDTH:(c + 1) * KV_WIDTH, :].T
            pos = jnp.swapaxes(tok.reshape(chunks, CMP_STRIDE, KV_WIDTH), 0, 1)
            rows_ref[c, :, pl.ds(c0, chunks), :] = pos
        return carry
    lax.fori_loop(0, pages_per_step, to_rows, 0, unroll=UNROLL_PAGES)

    part = pl.program_id(1)
    n = rows_ref.shape[2]
    row0 = pl.multiple_of(part * n, n)
    _compress_accumulate(lambda c, l: rows_ref[c, l], n, pea_ref, peb_ref, wa_ref, wb_ref, acca_ref, accb_ref, row0)

    @pl.when(part == steps_per_seq - 1)
    def _():
        _compress_finish(acca_ref, accb_ref, w2_ref, out_ref)


def _compress_sample(cache_t, page_table, layer, cw, steps_per_seq=2):
    db, n_pages = page_table.shape
    page = cache_t.shape[3]
    pps = n_pages // steps_per_seq
    n = n_pages * page // CMP_STRIDE
    const = lambda a: pl.BlockSpec(a.shape, lambda i, p, pt: (0,) * a.ndim)
    kern = functools.partial(_compress_sample_kernel, layer, pps, steps_per_seq)
    return pl.pallas_call(
        kern,
        grid_spec=pltpu.PrefetchScalarGridSpec(
            num_scalar_prefetch=1, grid=(db, steps_per_seq),
            in_specs=[pl.BlockSpec(memory_space=pl.ANY)] + [const(cw[k]) for k in _CW_NAMES],
            out_specs=pl.BlockSpec((1, n, CMP_LANES), lambda i, p, pt: (i, 0, 0)),
            scratch_shapes=[pltpu.VMEM((2, pps, CMP_LANES, page), F32), pltpu.SemaphoreType.DMA((2,)),
                            pltpu.VMEM((2, CMP_STRIDE, pps * page // CMP_STRIDE, KV_WIDTH), F32),
                            pltpu.VMEM((n, CMP_LANES), F32), pltpu.VMEM((n, CMP_LANES), F32)]),
        out_shape=jax.ShapeDtypeStruct((db, n, CMP_LANES), F32),
        compiler_params=_params("arbitrary", "arbitrary"), name="compress_sample",
    )(page_table, cache_t, *[cw[k] for k in _CW_NAMES])


def _cmp_to_sel(n_cmp_pad, n_cmp, n_sel_pad, n_sel):
    c0 = np.arange(n_cmp_pad)[:, None] * CMP_STRIDE
    s0 = np.arange(n_sel_pad)[None, :] * SEL_BLOCK
    hit = (c0 < s0 + SEL_BLOCK) & (c0 + CMP_BLOCK > s0)
    hit &= (np.arange(n_cmp_pad)[:, None] < n_cmp) & (np.arange(n_sel_pad)[None, :] < n_sel)
    return hit.astype(np.float32)


def _block_to_key(n_blk, n_key):
    return (np.arange(n_key)[None, :] // SEL_BLOCK == np.arange(n_blk)[:, None]).astype(np.float32)


def _nsa_prompt_kernel(qb, tk, q_ref, kc_ref, vc_ref, ks_ref, vs_ref, kw_ref, vw_ref, gate_ref, c2s_ref, hi_ref,
                       o_ref):
    i = pl.program_id(2)
    qs = pl.multiple_of(i * qb, qb)
    rows = HEADS_PER_KV * qb
    n_cmp = kc_ref.shape[0]
    n_sel = c2s_ref.shape[0]
    q_pad = q_ref[...].reshape(rows, 2 * HEAD_DIM)
    q = q_pad[:, :HEAD_DIM]
    qpos = qs + lax.broadcasted_iota(jnp.int32, (rows, 1), 0) % qb

    def biased(s, bias):
        return (s.reshape(HEADS_PER_KV, qb, s.shape[-1]) + bias[None]).reshape(s.shape)

    def weights(s, m):
        return jnp.exp2((s - m).astype(BF16))

    def normalised(acc):
        return acc[:, :HEAD_DIM] / jnp.maximum(acc[:, HEAD_DIM:HEAD_DIM + 1], 1e-30)

    s = _dot_nt(q, kc_ref[...])
    cmp_end = lax.broadcasted_iota(jnp.int32, (1, n_cmp), 1) * CMP_STRIDE + (CMP_BLOCK - 1)
    p = _masked_softmax(s, cmp_end <= qpos)
    o_cmp = _dot(p.astype(BF16), vc_ref[...])
    psum = p[0:qb]
    for h in range(1, HEADS_PER_KV):
        psum = psum + p[h * qb:(h + 1) * qb]
    imp_t = _dot_nt(c2s_ref[...], psum, precision=HIGHEST)

    blk = lax.broadcasted_iota(jnp.int32, (n_sel, qb), 0)
    qp = qs + lax.broadcasted_iota(jnp.int32, (n_sel, qb), 1)
    cur = qp // SEL_BLOCK
    forced = (blk == 0) | (blk == cur) | (blk == cur - 1)
    valid = blk <= cur
    score = jnp.where(valid, jnp.where(forced, FORCED_SCORE, imp_t), NEG)
    cnt = jnp.zeros((n_sel, qb), jnp.int32)
    for j in range(n_sel):
        sj = score[j:j + 1, :]
        ahead = (sj > score) | ((sj == score) & (j < blk))
        cnt = cnt + ahead.astype(jnp.int32)
    unsel_t = jnp.where((cnt < SEL_TOPN) & valid & (blk < qs // SEL_BLOCK), 0.0, 1.0).astype(BF16)

    q_sel = (q_pad.astype(F32) + jnp.concatenate([_dot_tn(unsel_t, hi_ref[...])] * HEADS_PER_KV, axis=0)).astype(BF16)
    qrow = lax.broadcasted_iota(jnp.int32, (qb, qb), 0)
    kcol = lax.broadcasted_iota(jnp.int32, (qb, qb), 1)
    s = biased(_dot_nt(q_pad, ks_ref[pl.ds(qs, qb), :]), jnp.where(kcol <= qrow, 0.0, NEG))
    m = jnp.max(s, axis=-1, keepdims=True)
    acc = _dot(weights(s, m), vs_ref[pl.ds(qs, qb), :])

    def sel_step(kt, carry):
        m, acc = carry
        k0 = pl.multiple_of(kt * tk, tk)
        s = _dot_nt(q_sel, ks_ref[pl.ds(k0, tk), :])
        m_new = jnp.maximum(m, jnp.max(s, axis=-1, keepdims=True))
        acc = jnp.exp2(m - m_new) * acc + _dot(weights(s, m_new), vs_ref[pl.ds(k0, tk), :])
        return m_new, acc

    _, acc = lax.fori_loop(0, (qs + tk - 1) // tk, sel_step, (m, acc))
    o_sel = normalised(acc)

    nw = WINDOW + qb
    w0 = pl.multiple_of(jnp.maximum(qs - WINDOW, 0), qb)
    rel = (qs + lax.broadcasted_iota(jnp.int32, (qb, nw), 0)) - (w0 + lax.broadcasted_iota(jnp.int32, (qb, nw), 1))
    s = biased(_dot_nt(q, kw_ref[pl.ds(w0, nw), :]), jnp.where((rel >= 0) & (rel < WINDOW), 0.0, NEG))
    o_win = normalised(_dot(weights(s, jnp.max(s, axis=-1, keepdims=True)), vw_ref[pl.ds(w0, nw), :]))

    g = gate_ref[...].reshape(rows, N_BRANCH)
    o = g[:, 0:1] * o_cmp + g[:, 1:2] * o_sel + g[:, 2:3] * o_win
    o_ref[...] = o.astype(o_ref.dtype).reshape(HEADS_PER_KV, qb, HEAD_DIM)


def _nsa_prompt(qh, ksa, kw, va, cvh, gates_h, qb=128, tk=512):
    b, _, t, _ = qh.shape
    n_cmp = cvh.shape[3]
    n_sel = t // SEL_BLOCK
    assert qb % SEL_BLOCK == 0 and t % tk == 0 and WINDOW % qb == 0 and WINDOW + qb <= t and n_sel <= HEAD_DIM
    c2s = jnp.asarray(_cmp_to_sel(n_cmp, n_cmp - 1, n_sel, n_sel).T)
    hi = jnp.asarray(np.concatenate([np.zeros((n_sel, HEAD_DIM)), np.eye(n_sel, HEAD_DIM)], axis=1), dtype=BF16)
    vspec = lambda c: pl.BlockSpec((None, None, None, t, 2 * HEAD_DIM), lambda bi, g, i: (bi, c, g, 0, 0))
    ccomp = lambda c: pl.BlockSpec((None, None, None, n_cmp, HEAD_DIM), lambda bi, g, i: (bi, c, g, 0, 0))
    qspec = pl.BlockSpec((None, HEADS_PER_KV, qb, 2 * HEAD_DIM), lambda bi, g, i: (bi, g, i, 0))
    ospec = pl.BlockSpec((None, HEADS_PER_KV, qb, HEAD_DIM), lambda bi, g, i: (bi, g, i, 0))
    kern = functools.partial(_nsa_prompt_kernel, qb, tk)
    return pl.pallas_call(
        kern, grid=(b, KV_HEADS, t // qb),
        in_specs=[qspec, ccomp(0), ccomp(1),
                  pl.BlockSpec((None, None, t, 2 * HEAD_DIM), lambda bi, g, i: (bi, g, 0, 0)), vspec(0),
                  pl.BlockSpec((None, None, t, HEAD_DIM), lambda bi, g, i: (bi, g, 0, 0)), vspec(1),
                  pl.BlockSpec((None, HEADS_PER_KV, qb, N_BRANCH), lambda bi, g, i: (bi, g, i, 0)),
                  pl.BlockSpec(c2s.shape, lambda bi, g, i: (0, 0)), pl.BlockSpec(hi.shape, lambda bi, g, i: (0, 0))],
        out_specs=ospec,
        out_shape=jax.ShapeDtypeStruct(qh.shape[:3] + (HEAD_DIM,), BF16),
        compiler_params=_params("arbitrary", "arbitrary", "arbitrary"), name="nsa_prompt",
    )(qh, cvh, cvh, ksa, va, kw, va, gates_h, c2s, hi)


SROWS = 8 * KV_HEADS


def _sample_cmp_win_kernel(past_len, n_cmp, qz_ref, kvc_ref, win_ref, wnew_ref, c2s_ref, ocmp_ref, owin_ref,
                           imp_ref):
    qz = qz_ref[0]
    kvc = kvc_ref[0]
    n_pad = kvc.shape[0]
    kc, vc = kvc[:, :KV_WIDTH].astype(BF16), kvc[:, KV_WIDTH:].astype(BF16)
    s = _dot_nt(qz, kc)
    n = lax.broadcasted_iota(jnp.int32, (1, n_pad), 1)
    p = _masked_softmax(s, (n < n_cmp) & (n * CMP_STRIDE + (CMP_BLOCK - 1) <= past_len))
    ocmp_ref[0] = _dot(p.astype(BF16), vc)
    r = lax.broadcasted_iota(jnp.int32, (SROWS, 1), 0)
    p = jnp.where(r % 8 < HEADS_PER_KV, p, 0.0)
    psum = jnp.concatenate([jnp.sum(p[8 * g:8 * g + 8], axis=0, keepdims=True) for g in range(KV_HEADS)] +
                           [jnp.zeros((8 - KV_HEADS, n_pad), F32)], axis=0)
    imp_ref[0] = _dot(psum, c2s_ref[...], precision=HIGHEST)

    win = win_ref[0]
    wb = win.shape[1]
    kw, vw = win[:KV_WIDTH].astype(BF16), win[KV_WIDTH:].astype(BF16)
    wnew = wnew_ref[0]
    s = _dot(qz, kw)
    s_new = jnp.sum(qz.astype(F32) * wnew[:, :KV_WIDTH].astype(BF16).astype(F32), axis=-1, keepdims=True)
    rel = wb - lax.broadcasted_iota(jnp.int32, (1, wb), 1)
    msk = (rel >= 0) & (rel < WINDOW)
    s = jnp.where(msk, s, NEG)
    m = jnp.maximum(jnp.max(s, axis=-1, keepdims=True), s_new)
    e = jnp.where(msk, jnp.exp2(s - m), 0.0)
    e_new = jnp.exp2(s_new - m)
    den = jnp.maximum(jnp.sum(e, axis=-1, keepdims=True) + e_new, 1e-30)
    pv = _dot_nt((e / den).astype(BF16), vw)
    pn = (e_new / den).astype(BF16).astype(F32)
    owin_ref[0] = pv + pn * wnew[:, KV_WIDTH:].astype(BF16).astype(F32)


def _sample_cmp_win(past_len, n_cmp, qz, kvc, win, wnew, c2s):
    db = qz.shape[0]
    blk3 = lambda a: pl.BlockSpec((1,) + a.shape[1:], lambda i: (i, 0, 0))
    kern = functools.partial(_sample_cmp_win_kernel, past_len, n_cmp)
    outs = (jax.ShapeDtypeStruct((db, SROWS, KV_WIDTH), F32), jax.ShapeDtypeStruct((db, SROWS, KV_WIDTH), F32),
            jax.ShapeDtypeStruct((db, 8, c2s.shape[1]), F32))
    return pl.pallas_call(
        kern, grid=(db,),
        in_specs=[blk3(qz), blk3(kvc), blk3(win), blk3(wnew), pl.BlockSpec(c2s.shape, lambda i: (0, 0))],
        out_specs=[pl.BlockSpec((1, SROWS, KV_WIDTH), lambda i: (i, 0, 0)),
                   pl.BlockSpec((1, SROWS, KV_WIDTH), lambda i: (i, 0, 0)),
                   pl.BlockSpec((1, 8, c2s.shape[1]), lambda i: (i, 0, 0))],
        out_shape=outs, compiler_params=_params("arbitrary"), name="sample_cmp_win",
    )(qz, kvc, win, wnew, c2s)


def _select_kernel(n_sel, cur, imp_ref, sel_ref):
    imp = imp_ref[...]
    blk = lax.broadcasted_iota(jnp.int32, imp.shape, 1)
    forced = (blk == 0) | (blk == cur) | (blk == cur - 1)
    valid = blk <= cur
    score = jnp.where(valid, jnp.where(forced, FORCED_SCORE, imp), NEG)

    def body(j, cnt):
        sj = jnp.sum(jnp.where(blk == j, score, 0.0), axis=-1, keepdims=True)
        ahead = (sj > score) | ((sj == score) & (j < blk))
        return cnt + ahead.astype(jnp.int32)

    cnt = lax.fori_loop(0, n_sel, body, jnp.zeros(imp.shape, jnp.int32))
    sel_ref[...] = jnp.where((cnt < SEL_TOPN) & valid, 1.0, 0.0)


def _select(imp2, n_sel, cur):
    return pl.pallas_call(
        functools.partial(_select_kernel, n_sel, cur),
        out_shape=jax.ShapeDtypeStruct(imp2.shape, F32), name="sample_select",
        compiler_params=pltpu.CompilerParams(vmem_limit_bytes=VMEM_LIMIT),
    )(imp2)


def _sample_sel_kernel(layer, pages_per_step, steps_per_seq, pt_ref, cache_ref, qz_ref, sel_ref, selnew_ref,
                       knew_ref, b2k_ref, ocmp_ref, owin_ref, gate_ref, o_ref, buf_ref, sem_ref, m_ref, l_ref,
                       acc_ref):
    page_len = cache_ref.shape[3]

    def page_copies(step, j):
        page = _page_of(pt_ref, step, j, pages_per_step, steps_per_seq)
        return [pltpu.make_async_copy(cache_ref.at[layer, page, pl.ds(CMP_LANES, CMP_LANES), :],
                                      buf_ref.at[step % 2, :, pl.ds(pl.multiple_of(j * page_len, page_len), page_len)],
                                      sem_ref.at[step % 2])]

    slot = _paged_fetch(page_copies, pages_per_step, steps_per_seq)
    part = pl.program_id(1)
    qz = qz_ref[0]

    @pl.when(part == 0)
    def _():
        m_ref[...] = jnp.full(m_ref.shape, NEG, F32)
        l_ref[...] = jnp.zeros(l_ref.shape, F32)
        acc_ref[...] = jnp.zeros(acc_ref.shape, F32)

    kv = buf_ref[slot].astype(BF16)
    s = _dot(qz, kv[:KV_WIDTH])
    msk = _dot(sel_ref[0].astype(BF16), b2k_ref[...]) > 0.5
    s = jnp.where(msk, s, NEG)
    m_old = m_ref[...]
    m_new = jnp.maximum(m_old, jnp.max(s, axis=-1, keepdims=True))
    alpha = jnp.exp2(m_old - m_new)
    e = jnp.where(msk, jnp.exp2(s - m_new), 0.0)
    l_ref[...] = alpha * l_ref[...] + jnp.sum(e, axis=-1, keepdims=True)
    acc_ref[...] = alpha * acc_ref[...] + _dot_nt(e.astype(BF16), kv[KV_WIDTH:])
    m_ref[...] = m_new

    @pl.when(part == steps_per_seq - 1)
    def _():
        knew = knew_ref[0].astype(BF16).astype(F32)
        s_new = jnp.sum(qz.astype(F32) * knew[:, :KV_WIDTH], axis=-1, keepdims=True)
        on = selnew_ref[0][:, 0:1] > 0.5
        s_new = jnp.where(on, s_new, NEG)
        m_old = m_ref[...]
        m_new = jnp.maximum(m_old, s_new)
        alpha = jnp.exp2(m_old - m_new)
        e_new = jnp.where(on, jnp.exp2(s_new - m_new), 0.0)
        l = alpha * l_ref[...] + e_new
        acc = alpha * acc_ref[...] + e_new.astype(BF16).astype(F32) * knew[:, KV_WIDTH:]
        o_sel = acc / jnp.maximum(l, 1e-30)
        g = gate_ref[0]
        o_ref[0] = g[:, 0:1] * ocmp_ref[0] + g[:, 1:2] * o_sel + g[:, 2:3] * owin_ref[0]


def _sample_sel(cache_t, page_table, layer, qz, sel16, knew, ocmp, owin, gz, steps_per_seq=2):
    db, n_pages = page_table.shape
    page = cache_t.shape[3]
    pps = n_pages // steps_per_seq
    keys = pps * page
    blocks = keys // SEL_BLOCK
    b2k = jnp.asarray(_block_to_key(blocks, keys), dtype=BF16)
    blk3 = lambda a: pl.BlockSpec((1,) + a.shape[1:], lambda i, p, pt: (i, 0, 0))
    kern = functools.partial(_sample_sel_kernel, layer, pps, steps_per_seq)
    return pl.pallas_call(
        kern,
        grid_spec=pltpu.PrefetchScalarGridSpec(
            num_scalar_prefetch=1, grid=(db, steps_per_seq),
            in_specs=[pl.BlockSpec(memory_space=pl.ANY), blk3(qz),
                      pl.BlockSpec((1, SROWS, blocks), lambda i, p, pt: (i, 0, p)),
                      pl.BlockSpec((1, SROWS, LANES), lambda i, p, pt: (i, 0, steps_per_seq * blocks // LANES)),
                      blk3(knew),
                      pl.BlockSpec(b2k.shape, lambda i, p, pt: (0, 0)), blk3(ocmp), blk3(owin), blk3(gz)],
            out_specs=pl.BlockSpec((1, SROWS, KV_WIDTH), lambda i, p, pt: (i, 0, 0)),
            scratch_shapes=[pltpu.VMEM((2, CMP_LANES, keys), F32), pltpu.SemaphoreType.DMA((2,)),
                            pltpu.VMEM((SROWS, 1), F32), pltpu.VMEM((SROWS, 1), F32),
                            pltpu.VMEM((SROWS, KV_WIDTH), F32)]),
        out_shape=jax.ShapeDtypeStruct((db, SROWS, KV_WIDTH), F32),
        compiler_params=_params("arbitrary", "arbitrary"), name="sample_sel",
    )(page_table, cache_t, qz, sel16, sel16, knew, b2k, ocmp, owin, gz)


def _bdot(dot, a, b):
    return dot(a.astype(BF16), b.astype(BF16))


def _head_sum(x, hm):
    xh = x.astype(BF16)
    xl = (x - xh.astype(F32)).astype(BF16)
    return _dot(xh, hm) + _dot(xl, hm)


def _softplus(z):
    return jnp.maximum(z, 0.0) + jnp.log(1.0 + jnp.exp(-jnp.abs(z)))


CHUNK_GROUP = 4


def _rwkv_kernel(chunk, n_valid, pr_ref, prev_ref, s0_ref, mu_ref, w0_ref, wup_ref, a0_ref, aup_ref, gup_ref,
                 kk_ref, ka_ref, rk_ref, lnw_ref, lnb_ref, hm_ref, tri_ref, out_ref, sout_ref,
                 carry_ref, state_ref, r_ref, k_ref, v_ref, lw_ref, a_ref, b_ref, rh_ref, y_ref, m_ref, n_ref, et_ref):
    ti = pl.program_id(1)
    tt = pr_ref.shape[1]
    w_ = RWKV_WIDTH
    hd = RWKV_HEAD

    @pl.when(ti == 0)
    def _():
        carry_ref[...] = prev_ref[0]
        state_ref[...] = s0_ref[0]

    p = pr_ref[0]
    row = lax.broadcasted_iota(jnp.int32, (tt, 1), 0)
    shifted = jnp.where(row == 0, carry_ref[...], pltpu.roll(p, 1, 0))
    carry_ref[...] = p[tt - 1:tt]
    xs = p + (shifted - p) * mu_ref[...]
    r, k, v = xs[:, :w_], xs[:, w_:2 * w_], xs[:, 2 * w_:3 * w_]
    wa = xs[:, 3 * w_:3 * w_ + DECAY_LORA + AAA_LORA]
    gl = xs[:, 3 * w_ + DECAY_LORA + AAA_LORA:]
    hm = hm_ref[...]
    w = -_softplus(-(w0_ref[...] + _dot(jnp.tanh(wa).astype(BF16), wup_ref[...]))) - 0.5
    lw = -jnp.exp(w)
    a = jax.nn.sigmoid(a0_ref[...] + _dot(wa.astype(BF16), aup_ref[...]))
    g = _dot(jax.nn.sigmoid(gl).astype(BF16), gup_ref[...])
    kk = k * kk_ref[...]
    kk = kk / jnp.maximum(jnp.sqrt(_head_sum(kk * kk, hm)), 1e-12)
    k = k * (1.0 + (a - 1.0) * ka_ref[...])
    av, bv = -kk, kk * a
    if n_valid < tt:
        keep = row < n_valid
        lw, k, v, av, bv = (jnp.where(keep, t, 0.0) for t in (lw, k, v, av, bv))
    r_ref[...], k_ref[...], v_ref[...], lw_ref[...], a_ref[...], b_ref[...] = r, k, v, lw, av, bv

    ci = lax.broadcasted_iota(jnp.int32, (chunk, chunk), 0)
    cj = lax.broadcasted_iota(jnp.int32, (chunk, chunk), 1)
    strict, incl = ci > cj, ci >= cj
    dot = functools.partial(_bdot, _dot)
    dot_nt = functools.partial(_bdot, _dot_nt)
    dot_tn = functools.partial(_bdot, _dot_tn)
    group = min(CHUNK_GROUP, tt // chunk)

    def prepare(cg, carry):
        jobs, pre = [], []
        for ci in range(group):
            c = cg * group + ci
            rows = pl.ds(pl.multiple_of(c * chunk, chunk), chunk)
            lwc = lw_ref[rows, :]
            cs = _dot(tri_ref[...], lwc, precision=HIGHEST)
            e_in, e_out = jnp.exp(cs), jnp.exp(-cs)
            total = cs[chunk - 1:chunk]
            e_end = jnp.exp(total - cs)
            kc, bc = k_ref[rows, :], b_ref[rows, :]
            pre.append(dict(c=c, rows=rows, rt=r_ref[rows, :] * e_in, at=a_ref[rows, :] * jnp.exp(cs - lwc),
                            kt=kc * e_out, bt=bc * e_out, kh=kc * e_end, bh=bc * e_end, vc=v_ref[rows, :]))
            et_ref[c] = jnp.broadcast_to(jnp.exp(total), et_ref.shape[1:])
            jobs += [(ci, h, slice(h * hd, (h + 1) * hd)) for h in range(RWKV_HEADS)]
        nj = range(len(jobs))
        part = lambda name: [pre[ci][name][:, sl] for ci, _, sl in jobs]
        at, rt, bt, kt, bh, kh, vh = (part(n_) for n_ in ("at", "rt", "bt", "kt", "bh", "kh", "vc"))
        gm = [dot_nt(jnp.concatenate([at[j], rt[j]], axis=0), jnp.concatenate([bt[j], kt[j]], axis=0))
              for j in nj]
        lab = [jnp.where(strict, g_[:chunk, :chunk], 0.0) for g_ in gm]
        lak = [jnp.where(strict, g_[:chunk, chunk:], 0.0) for g_ in gm]
        mrb = [jnp.where(incl, g_[chunk:, :chunk], 0.0) for g_ in gm]
        mrk = [jnp.where(incl, g_[chunk:, chunk:], 0.0) for g_ in gm]
        wm = [jnp.concatenate([at[j], dot(lak[j], vh[j])], axis=1) for j in nj]
        x, n = lab, 1
        while 2 * n < chunk:
            xw = [dot(x[j], jnp.concatenate([wm[j], x[j]], axis=1)) for j in nj]
            x = [t[:, 2 * hd:] for t in xw]
            wm = [wm[j] + xw[j][:, :2 * hd] for j in nj]
            n *= 2
        wm = [wm[j] + dot(x[j], wm[j]) for j in nj]
        mw = [dot(mrb[j], wm[j]) for j in nj]
        y0 = [mw[j][:, hd:] + dot(mrk[j], vh[j]) for j in nj]
        wb = [dot_tn(wm[j], bh[j]) for j in nj]
        vk = [dot_tn(vh[j], kh[j]) for j in nj]
        for j, (ci, h, sl) in enumerate(jobs):
            c, rows = pre[ci]["c"], pre[ci]["rows"]
            rh_ref[rows, sl] = rt[j] + mw[j][:, :hd]
            y_ref[rows, sl] = y0[j]
            m_ref[c, h] = wb[j][:hd]
            n_ref[c, h] = wb[j][hd:] + vk[j]
        return carry

    def advance(c, carry):
        rows = pl.ds(pl.multiple_of(c * chunk, chunk), chunk)
        e_tot = et_ref[c]
        sls = [slice(h * hd, (h + 1) * hd) for h in range(RWKV_HEADS)]
        s0 = [state_ref[h] for h in range(RWKV_HEADS)]
        ys = [dot_nt(rh_ref[rows, sl], s0[h]) for h, sl in enumerate(sls)]
        sm = [dot(s0[h], m_ref[c, h]) for h in range(RWKV_HEADS)]
        for h, sl in enumerate(sls):
            y_ref[rows, sl] += ys[h]
            state_ref[h] = s0[h] * e_tot[0:1, sl] + sm[h] + n_ref[c, h]
        return carry

    lax.fori_loop(0, tt // chunk // group, prepare, 0)
    lax.fori_loop(0, tt // chunk, advance, 0)

    y = y_ref[...]
    inv_n = 1.0 / hd
    mean = _head_sum(y, hm) * inv_n
    d = y - mean
    var = _head_sum(d * d, hm) * inv_n
    y = d * lax.rsqrt(var + LNX_EPS) * lnw_ref[...] + lnb_ref[...]
    y = y + _head_sum(r_ref[...] * k_ref[...] * rk_ref[...], hm) * v_ref[...]
    out_ref[0] = (y * g).astype(out_ref.dtype)
    sout_ref[0] = state_ref[...]


def _rwkv(pr3, prev, s0, rw, chunk, tt, n_valid):
    b, t, _ = pr3.shape
    hm = jnp.asarray(np.kron(np.eye(RWKV_HEADS), np.ones((RWKV_HEAD, RWKV_HEAD))), dtype=BF16)
    tri = jnp.asarray(np.tril(np.ones((chunk, chunk))), dtype=F32)
    vec = lambda a: pl.BlockSpec(a.shape, lambda i, j: (0, 0))
    prev3 = prev[:, None, :]
    names = ("mu", "w0", "wup", "a0", "aup", "gup", "kk", "ka", "rk", "lnw", "lnb")
    kern = functools.partial(_rwkv_kernel, chunk, n_valid)
    per_chunk = (tt // chunk, RWKV_HEADS, RWKV_HEAD, RWKV_HEAD)
    return pl.pallas_call(
        kern, grid=(b, t // tt),
        in_specs=[pl.BlockSpec((1, tt, RWKV_PROJ), lambda i, j: (i, j, 0)),
                  pl.BlockSpec((1, 1, RWKV_PROJ), lambda i, j: (i, 0, 0)),
                  pl.BlockSpec((1, RWKV_HEADS, RWKV_HEAD, RWKV_HEAD), lambda i, j: (i, 0, 0, 0))] +
                 [vec(rw[n]) for n in names] + [vec(hm), vec(tri)],
        out_specs=[pl.BlockSpec((1, tt, RWKV_WIDTH), lambda i, j: (i, j, 0)),
                   pl.BlockSpec((1, RWKV_HEADS, RWKV_HEAD, RWKV_HEAD), lambda i, j: (i, 0, 0, 0))],
        out_shape=(jax.ShapeDtypeStruct((b, t, RWKV_WIDTH), BF16),
                   jax.ShapeDtypeStruct((b, RWKV_HEADS, RWKV_HEAD, RWKV_HEAD), F32)),
        scratch_shapes=[pltpu.VMEM((1, RWKV_PROJ), F32), pltpu.VMEM((RWKV_HEADS, RWKV_HEAD, RWKV_HEAD), F32)] +
                       [pltpu.VMEM((tt, RWKV_WIDTH), F32) for _ in range(8)] +
                       [pltpu.VMEM(per_chunk, F32), pltpu.VMEM(per_chunk, F32),
                        pltpu.VMEM((tt // chunk, 8, RWKV_WIDTH), F32)],
        compiler_params=_params("arbitrary", "arbitrary"), name="rwkv",
    )(pr3, prev3, s0, *[rw[n] for n in names], hm, tri)


def _ffn_head(x_ref, mix_ref, wo_ref, gpost_ref, gpre_ref, x1_ref, h2_ref, acc_ref):
    x1 = x_ref[...] + _rms(_dot(mix_ref[...], wo_ref[...]), gpost_ref[...])
    x1_ref[...] = x1
    h2_ref[...] = _rms(x1, gpre_ref[...]).astype(BF16)
    acc_ref[...] = jnp.zeros(acc_ref.shape, F32)


def _ffn_tail(gate2, gate1, gate, val, cw_ref, cb_ref, wd_ref, acc_ref):
    conv = cb_ref[...] + (gate2 * cw_ref[0:1, :] + gate1 * cw_ref[1:2, :] + gate * cw_ref[2:3, :])
    acc_ref[...] += _dot((jax.nn.gelu(conv) * val).astype(BF16), wd_ref[...])


def _ffn_seq_kernel(tiles_per_seq, x_ref, mix_ref, wo_ref, gpost_ref, gpre_ref, gfpost_ref, wg_ref, wv_ref, cw_ref,
                    cb_ref, wd_ref, out_ref, tail_ref, x1_ref, h2_ref, acc_ref, carry_ref):
    i, j = pl.program_id(0), pl.program_id(1)
    tm = x_ref.shape[0]

    @pl.when(j == 0)
    def _():
        _ffn_head(x_ref, mix_ref, wo_ref, gpost_ref, gpre_ref, x1_ref, h2_ref, acc_ref)

    h2 = h2_ref[...]
    gate, val = _dot(h2, wg_ref[...]), _dot(h2, wv_ref[...])
    @pl.when(i % tiles_per_seq == 0)
    def _():
        carry_ref[j] = jnp.zeros(carry_ref.shape[1:], F32)

    prev = carry_ref[j]
    row = lax.broadcasted_iota(jnp.int32, (tm, 1), 0)
    gate1 = jnp.where(row == 0, prev[7:8], pltpu.roll(gate, 1, 0))
    gate2 = jnp.where(row == 0, prev[6:7], jnp.where(row == 1, prev[7:8], pltpu.roll(gate, 2, 0)))
    carry_ref[j] = gate[tm - 8:tm]
    tail_ref[0] = gate[tm - 8:tm]
    _ffn_tail(gate2, gate1, gate, val, cw_ref, cb_ref, wd_ref, acc_ref)

    @pl.when(j == pl.num_programs(1) - 1)
    def _():
        out_ref[...] = x1_ref[...] + _rms(acc_ref[...], gfpost_ref[...])


def _ffn_step_kernel(x_ref, mix_ref, wo_ref, gpost_ref, gpre_ref, gfpost_ref, wg_ref, wv_ref, cw_ref, cb_ref,
                     wd_ref, prev_ref, out_ref, gate_ref, x1_ref, h2_ref, acc_ref):
    j = pl.program_id(0)

    @pl.when(j == 0)
    def _():
        _ffn_head(x_ref, mix_ref, wo_ref, gpost_ref, gpre_ref, x1_ref, h2_ref, acc_ref)

    h2 = h2_ref[...]
    gate, val = _dot(h2, wg_ref[...]), _dot(h2, wv_ref[...])
    gate_ref[...] = gate
    _ffn_tail(prev_ref[0], prev_ref[1], gate, val, cw_ref, cb_ref, wd_ref, acc_ref)

    @pl.when(j == pl.num_programs(0) - 1)
    def _():
        out_ref[...] = x1_ref[...] + _rms(acc_ref[...], gfpost_ref[...])


def _ffn_seq(x2, mix2, fw, seq_len, tm, tf):
    m, d = x2.shape
    d_ff = fw["wd"].shape[0]
    nf = d_ff // tf
    row = lambda i, j: (i, 0)
    full = lambda i, j: (0, 0)
    tiles_per_seq = seq_len // tm
    scratch = [pltpu.VMEM((tm, d), F32), pltpu.VMEM((tm, d), BF16), pltpu.VMEM((tm, d), F32),
               pltpu.VMEM((nf, 8, tf), F32)]
    return pl.pallas_call(
        functools.partial(_ffn_seq_kernel, tiles_per_seq), grid=(m // tm, nf),
        in_specs=[pl.BlockSpec((tm, d), row), pl.BlockSpec((tm, d), row), pl.BlockSpec((d, d), full),
                  pl.BlockSpec((1, d), full), pl.BlockSpec((1, d), full), pl.BlockSpec((1, d), full),
                  pl.BlockSpec((d, tf), lambda i, j: (0, j)), pl.BlockSpec((d, tf), lambda i, j: (0, nf + j)),
                  pl.BlockSpec((CONV_WIDTH, tf), lambda i, j: (0, j)), pl.BlockSpec((1, tf), lambda i, j: (0, j)),
                  pl.BlockSpec((tf, d), lambda i, j: (j, 0))],
        out_specs=[pl.BlockSpec((tm, d), row), pl.BlockSpec((1, 8, tf), lambda i, j: (i, 0, j))],
        out_shape=(jax.ShapeDtypeStruct((m, d), F32), jax.ShapeDtypeStruct((m // tm, 8, d_ff), F32)),
        scratch_shapes=scratch, compiler_params=_params("arbitrary", "arbitrary"), name="ffn_seq",
    )(x2, mix2, fw["wo"], fw["gpost"], fw["gpre"], fw["gfpost"], fw["wup"], fw["wup"], fw["cw"], fw["cb"], fw["wd"])


def _ffn_step(x2, mix2, fw, prev, tf):
    m, d = x2.shape
    d_ff = fw["wd"].shape[0]
    nf = d_ff // tf
    full = lambda j: (0, 0)
    scratch = [pltpu.VMEM((m, d), F32), pltpu.VMEM((m, d), BF16), pltpu.VMEM((m, d), F32)]
    return pl.pallas_call(
        _ffn_step_kernel, grid=(nf,),
        in_specs=[pl.BlockSpec((m, d), full), pl.BlockSpec((m, d), full), pl.BlockSpec((d, d), full),
                  pl.BlockSpec((1, d), full), pl.BlockSpec((1, d), full), pl.BlockSpec((1, d), full),
                  pl.BlockSpec((d, tf), lambda j: (0, j)), pl.BlockSpec((d, tf), lambda j: (0, nf + j)),
                  pl.BlockSpec((CONV_WIDTH, tf), lambda j: (0, j)), pl.BlockSpec((1, tf), lambda j: (0, j)),
                  pl.BlockSpec((tf, d), lambda j: (j, 0)),
                  pl.BlockSpec((CONV_WIDTH - 1, m, tf), lambda j: (0, 0, j))],
        out_specs=[pl.BlockSpec((m, d), full), pl.BlockSpec((m, tf), lambda j: (0, j))],
        out_shape=(jax.ShapeDtypeStruct((m, d), F32), jax.ShapeDtypeStruct((m, d_ff), F32)),
        scratch_shapes=scratch, compiler_params=_params("arbitrary"), name="ffn_step",
    )(x2, mix2, fw["wo"], fw["gpost"], fw["gpre"], fw["gfpost"], fw["wup"], fw["wup"], fw["cw"], fw["cb"],
      fw["wd"], prev)


def _layer_weights(l, p):
    w_in = p["w_in"][l]
    o = NSA_WIDTH + 6 * KV_WIDTH
    wg = jnp.pad(w_in[:, o:NSA_PROJ], ((0, 0), (0, LANES - N_BRANCH * NSA_HEADS)))
    row = lambda a: a[l][None, :].astype(F32)
    zpad = jnp.zeros((AAA_LORA, RWKV_WIDTH), F32)
    rw = {"mu": row(p["rwkv_mu"]), "w0": row(p["rwkv_w0"]),
          "wup": jnp.concatenate([p["rwkv_w_up"][l], zpad], axis=0).astype(BF16),
          "a0": row(p["rwkv_a0"]), "aup": jnp.concatenate([zpad, p["rwkv_a_up"][l]], axis=0).astype(BF16),
          "gup": p["rwkv_g_up"][l].astype(BF16), "kk": row(p["rwkv_k_k"]), "ka": row(p["rwkv_k_a"]),
          "rk": p["rwkv_r_k"][l].reshape(1, RWKV_WIDTH), "lnw": row(p["rwkv_lnx_w"]), "lnb": row(p["rwkv_lnx_b"])}
    fw = {"wo": p["w_out"][l].astype(BF16), "gpost": row(p["norm_mix_post"]), "gpre": row(p["norm_ffn_pre"]),
          "gfpost": row(p["norm_ffn_post"]), "wup": p["ffn_w_up"][l].astype(BF16), "cw": p["ffn_conv_w"][l],
          "cb": row(p["ffn_conv_b"]), "wd": p["ffn_w_down"][l].astype(BF16)}
    return {"g_in": row(p["norm_mix_pre"]), "wq": w_in[:, :NSA_WIDTH].astype(BF16),
            "wkv": w_in[:, NSA_WIDTH:o].astype(BF16), "wg": wg.astype(BF16), "wr": w_in[:, NSA_PROJ:].astype(BF16),
            "cw": _compress_weights(p["cmp_pe"][l], p["cmp_w1"][l], p["cmp_w2"][l]), "rw": rw, "fw": fw}


def _prompt_layer(x, lw, tabs):
    b, t, d = x.shape
    m = b * t
    x2 = x.reshape(m, d)
    q, kvw, gates, pr = _norm_proj(x2, lw["g_in"], lw["wq"], lw["wkv"], lw["wg"], lw["wr"], tabs, tm=256)
    kvw3 = kvw.reshape(b, t, 6 * KV_WIDTH)
    kvc = _compress_prompt(kvw3, lw["cw"])
    cvh = kvc.astype(BF16).reshape(b, -1, 2, KV_HEADS, HEAD_DIM).transpose(0, 2, 3, 1, 4)
    qh = (q * (ATTN_SCALE * LOG2E)).astype(BF16).reshape(b, t, NSA_HEADS, HEAD_DIM).transpose(0, 2, 1, 3)
    qh = jnp.pad(qh, ((0, 0), (0, 0), (0, 0), (0, HEAD_DIM)))
    own_block = jnp.asarray(NEG * _block_to_key(HEAD_DIM, t).T, dtype=BF16)
    ks = kvw3[:, :, 2 * KV_WIDTH:3 * KV_WIDTH].astype(BF16).reshape(b, t, KV_HEADS, HEAD_DIM).transpose(0, 2, 1, 3)
    ksa = jnp.concatenate([ks, jnp.broadcast_to(own_block, ks.shape[:2] + own_block.shape)], axis=-1)
    heads = lambda a: a.astype(BF16).reshape(b, t, KV_HEADS, HEAD_DIM).transpose(0, 2, 1, 3)
    kw = heads(kvw3[:, :, 4 * KV_WIDTH:5 * KV_WIDTH])
    ones_lane = jnp.zeros((b, KV_HEADS, t, HEAD_DIM), BF16).at[..., 0].set(1.0)
    va = jnp.stack([jnp.concatenate([heads(kvw3[:, :, c * KV_WIDTH:(c + 1) * KV_WIDTH]), ones_lane], axis=-1)
                    for c in (3, 5)], axis=1)
    gates_h = gates[:, :N_BRANCH * NSA_HEADS].reshape(b, t, NSA_HEADS, N_BRANCH).transpose(0, 2, 1, 3)
    o_nsa = _nsa_prompt(qh, ksa, kw, va, cvh, gates_h).transpose(0, 2, 1, 3).reshape(m, NSA_WIDTH)
    s0 = jnp.zeros((b, RWKV_HEADS, RWKV_HEAD, RWKV_HEAD), F32)
    pr3 = pr.reshape(b, t, RWKV_PROJ)
    o_rwkv, wkv = _rwkv(pr3, jnp.zeros((b, RWKV_PROJ), F32), s0, lw["rw"], chunk=64, tt=min(t, 512), n_valid=t)
    mixed = jnp.concatenate([o_nsa, o_rwkv.reshape(m, RWKV_WIDTH)], axis=1)
    d_ff = lw["fw"]["wd"].shape[0]
    tm = min(t, 512)
    y, tail = _ffn_seq(x2, mixed, lw["fw"], t, tm=tm, tf=d_ff // 2)
    tail = tail[t // tm - 1::t // tm]
    kv_rows = kvw3[:, :, :4 * KV_WIDTH].reshape(b, t, 4, KV_HEADS, HEAD_DIM)
    wn = min(WINDOW, t)
    win = kvw3[:, t - wn:, 4 * KV_WIDTH:].reshape(b, wn, 2, KV_HEADS, HEAD_DIM)
    return y.reshape(b, t, d), kv_rows, win, wkv, pr3[:, -1], tail[:, 8 - (CONV_WIDTH - 1):]


def _sample_layer(x, lw, tabs, l, cache_t, page_table, st_win, st_wkv, st_shift, st_conv):
    db, tn, d = x.shape
    past_len = page_table.shape[1] * cache_t.shape[3]
    x2 = x.reshape(db, d)
    q, kvw, gates, pr = _norm_proj(x2, lw["g_in"], lw["wq"], lw["wkv"], lw["wg"], lw["wr"], tabs, tm=db)
    qs = (q * (ATTN_SCALE * LOG2E)).astype(BF16).reshape(db, KV_HEADS, HEADS_PER_KV, HEAD_DIM)
    eye = jnp.eye(KV_HEADS, dtype=BF16)
    qz = jnp.einsum('bghd,gj->bghjd', qs, eye).reshape(db, KV_HEADS, HEADS_PER_KV, KV_WIDTH)
    qz = jnp.pad(qz, ((0, 0), (0, 0), (0, 8 - HEADS_PER_KV), (0, 0))).reshape(db, SROWS, KV_WIDTH)
    kvc = _compress_sample(cache_t, page_table, l, lw["cw"])
    n_cmp = past_len // CMP_STRIDE - 1
    n_sel = -(-(past_len + tn) // SEL_BLOCK)
    n_sel_pad = -(-n_sel // LANES) * LANES
    c2s = jnp.asarray(_cmp_to_sel(kvc.shape[1], n_cmp, n_sel_pad, n_sel))
    wb = st_win.shape[1]
    win = st_win.transpose(0, 2, 3, 4, 1).reshape(db, 2 * KV_WIDTH, wb)
    wnew = kvw[:, None, 4 * KV_WIDTH:]
    ocmp, owin, imp = _sample_cmp_win(past_len, n_cmp, qz, kvc, win, wnew, c2s)
    sel = _select(imp.reshape(db * 8, n_sel_pad), n_sel, past_len // SEL_BLOCK).reshape(db, 8, n_sel_pad)
    sel16 = jnp.broadcast_to(sel[:, :KV_HEADS, None, :], (db, KV_HEADS, 8, n_sel_pad)).reshape(db, SROWS, n_sel_pad)
    gz = gates[:, :N_BRANCH * NSA_HEADS].reshape(db, KV_HEADS, HEADS_PER_KV, N_BRANCH)
    gz = jnp.pad(gz, ((0, 0), (0, 0), (0, 8 - HEADS_PER_KV), (0, 0))).reshape(db, SROWS, N_BRANCH)
    knew = kvw[:, None, 2 * KV_WIDTH:4 * KV_WIDTH]
    o16 = _sample_sel(cache_t, page_table, l, qz, sel16, knew, ocmp, owin, gz)
    o4 = o16.reshape(db, KV_HEADS, 8, KV_HEADS, HEAD_DIM)[:, :, :HEADS_PER_KV]
    o_nsa = jnp.stack([o4[:, g, :, g] for g in range(KV_HEADS)], axis=1).reshape(db, NSA_WIDTH)
    pad_t = 8
    pr3 = jnp.pad(pr[:, None, :], ((0, 0), (0, pad_t - tn), (0, 0)))
    o_rwkv, wkv = _rwkv(pr3, st_shift, st_wkv, lw["rw"], chunk=pad_t, tt=pad_t, n_valid=tn)
    mixed = jnp.concatenate([o_nsa.astype(BF16), o_rwkv[:, 0]], axis=1)
    prev = st_conv.transpose(1, 0, 2)
    d_ff = lw["fw"]["wd"].shape[0]
    y, gate = _ffn_step(x2, mixed, lw["fw"], prev, tf=d_ff // 2)
    kv_rows = kvw[:, :4 * KV_WIDTH].reshape(db, tn, 4, KV_HEADS, HEAD_DIM)
    win_new = jnp.concatenate([st_win[:, tn:], kvw[:, 4 * KV_WIDTH:].reshape(db, tn, 2, KV_HEADS, HEAD_DIM)], axis=1)
    conv_new = jnp.concatenate([st_conv[:, 1:], gate[:, None, :]], axis=1)
    return y.reshape(db, tn, d), kv_rows, win_new, wkv, pr, conv_new


def kernel(x_prompt, x_sample, cache_kv, page_table, state_win, state_wkv, state_shift, state_conv, norm_mix_pre, norm_mix_post, norm_ffn_pre, norm_ffn_post, w_in, w_out, cmp_pe, cmp_w1, cmp_w2, rwkv_mu, rwkv_w0, rwkv_w_up, rwkv_a0, rwkv_a_up, rwkv_g_up, rwkv_k_k, rwkv_k_a, rwkv_r_k, rwkv_lnx_w, rwkv_lnx_b, ffn_w_up, ffn_conv_w, ffn_conv_b, ffn_w_down):
    p = dict(norm_mix_pre=norm_mix_pre, norm_mix_post=norm_mix_post, norm_ffn_pre=norm_ffn_pre,
             norm_ffn_post=norm_ffn_post, w_in=w_in, w_out=w_out, cmp_pe=cmp_pe, cmp_w1=cmp_w1, cmp_w2=cmp_w2,
             rwkv_mu=rwkv_mu, rwkv_w0=rwkv_w0, rwkv_w_up=rwkv_w_up, rwkv_a0=rwkv_a0, rwkv_a_up=rwkv_a_up,
             rwkv_g_up=rwkv_g_up, rwkv_k_k=rwkv_k_k, rwkv_k_a=rwkv_k_a, rwkv_r_k=rwkv_r_k,
             rwkv_lnx_w=rwkv_lnx_w, rwkv_lnx_b=rwkv_lnx_b, ffn_w_up=ffn_w_up, ffn_conv_w=ffn_conv_w,
             ffn_conv_b=ffn_conv_b, ffn_w_down=ffn_w_down)
    depth = w_in.shape[0]
    b, t, _ = x_prompt.shape
    db, tn, _ = x_sample.shape
    n_pool, page = cache_kv.shape[1], cache_kv.shape[2]
    past_len = page_table.shape[1] * page
    cache_t = cache_kv.transpose(0, 1, 3, 4, 5, 2).reshape(depth, n_pool, 2 * CMP_LANES, page)
    tabs_p = _rope_tables(jnp.arange(t, dtype=jnp.int32))
    tabs_s = _rope_tables(jnp.full((db,), past_len, jnp.int32))
    xp, xs = x_prompt, x_sample
    outs_p, outs_s = [], []
    for l in range(depth):
        lw = _layer_weights(l, p)
        xp, *st_p = _prompt_layer(xp, lw, tabs_p)
        xs, *st_s = _sample_layer(xs, lw, tabs_s, l, cache_t, page_table, state_win[l], state_wkv[l],
                                  state_shift[l], state_conv[l])
        outs_p.append(st_p)
        outs_s.append(st_s)
    stack = lambda outs, i: jnp.stack([o[i] for o in outs])
    return (xp, xs, stack(outs_p, 0), stack(outs_s, 0), stack(outs_p, 1), stack(outs_s, 1),
            stack(outs_p, 2), stack(outs_s, 2), stack(outs_p, 3), stack(outs_s, 3),
            stack(outs_p, 4), stack(outs_s, 4))
```

```python
import functools

import jax
import jax.numpy as jnp
import numpy as np
from jax import lax
from jax.experimental import pallas as pl
from jax.experimental.pallas import tpu as pltpu

F32, BF16 = jnp.float32, jnp.bfloat16
HIGHEST = lax.Precision.HIGHEST

LANES = 128
VMEM_LIMIT = 56 * 1024 * 1024

HEAD_DIM = 64
NSA_HEADS = 8
KV_HEADS = 2
HEADS_PER_KV = NSA_HEADS // KV_HEADS
NSA_WIDTH = NSA_HEADS * HEAD_DIM
KV_WIDTH = KV_HEADS * HEAD_DIM
N_BRANCH = 3
CMP_STRIDE = 16
CMP_BLOCK = 2 * CMP_STRIDE
SEL_BLOCK = 64
SEL_TOPN = 16
WINDOW = 512
ROPE_THETA = 500000.0
ROPE_DIM = HEAD_DIM // 4
RWKV_HEAD = 64
RWKV_HEADS = 8
RWKV_WIDTH = RWKV_HEADS * RWKV_HEAD
DECAY_LORA = 64
AAA_LORA = 64
GATE_LORA = 128
RWKV_PROJ = 3 * RWKV_WIDTH + DECAY_LORA + AAA_LORA + GATE_LORA
NSA_PROJ = NSA_WIDTH + 6 * KV_WIDTH + N_BRANCH * NSA_HEADS
CONV_WIDTH = 3
NORM_EPS = 1e-6
LNX_EPS = 64e-5
NEG = -1e30
FORCED_SCORE = 1e9
ATTN_SCALE = HEAD_DIM ** -0.5
LOG2E = 1.4426950408889634


def _dot(a, b, precision=None):
    return jnp.dot(a, b, preferred_element_type=F32, precision=precision)


def _dot_nt(a, b, precision=None):
    return lax.dot_general(a, b, (((1,), (1,)), ((), ())), preferred_element_type=F32, precision=precision)


def _dot_tn(a, b, precision=None):
    return lax.dot_general(a, b, (((0,), (0,)), ((), ())), preferred_element_type=F32, precision=precision)


def _rms(x, g):
    return x * lax.rsqrt(jnp.mean(x * x, axis=-1, keepdims=True) + NORM_EPS) * g


def _masked_softmax(s, mask):
    s = jnp.where(mask, s, NEG)
    m = jnp.max(s, axis=-1, keepdims=True)
    e = jnp.where(mask, jnp.exp2(s - m), 0.0)
    return e / jnp.maximum(jnp.sum(e, axis=-1, keepdims=True), 1e-30)


def _params(*sem):
    return pltpu.CompilerParams(dimension_semantics=sem, vmem_limit_bytes=VMEM_LIMIT)


def _rope(v, cos, sa, sb):
    return v * cos + pltpu.roll(v, LANES - ROPE_DIM // 2, 1) * sa + pltpu.roll(v, ROPE_DIM // 2, 1) * sb


def _project(x_ref, g_ref, wq_ref, wkv_ref, wg_ref, wr_ref, tabs, kvw_ref, gate_ref, pr_ref):
    h = _rms(x_ref[...], g_ref[...]).astype(BF16)
    q = _dot(h, wq_ref[...])
    q = jnp.concatenate([_rope(q[:, j * LANES:(j + 1) * LANES], *tabs) for j in range(NSA_WIDTH // LANES)], axis=1)
    kv = _dot(h, wkv_ref[...])
    blocks = []
    for j in range(6 * KV_WIDTH // LANES):
        blk = kv[:, j * LANES:(j + 1) * LANES]
        blocks.append(_rope(blk, *tabs) if j % 2 == 0 else blk)
        kvw_ref[:, j * LANES:(j + 1) * LANES] = blocks[j]
    gate_ref[...] = jax.nn.sigmoid(_dot(h, wg_ref[...]))
    pr_ref[...] = _dot(h, wr_ref[...])
    return q, blocks


def _norm_proj_kernel(x_ref, g_ref, wq_ref, wkv_ref, wg_ref, wr_ref, cos_ref, sa_ref, sb_ref,
                      q_ref, kvw_ref, gate_ref, pr_ref):
    tabs = (cos_ref[...], sa_ref[...], sb_ref[...])
    q_ref[...], _ = _project(x_ref, g_ref, wq_ref, wkv_ref, wg_ref, wr_ref, tabs, kvw_ref, gate_ref, pr_ref)


def _norm_proj_heads_kernel(x_ref, g_ref, wq_ref, wkv_ref, wg_ref, wr_ref, cos_ref, sa_ref, sb_ref, own_ref,
                            kvw_ref, gate_ref, pr_ref, qh_ref, ksa_ref, kwh_ref, va_ref):
    tabs = (cos_ref[...], sa_ref[...], sb_ref[...])
    q, kv = _project(x_ref, g_ref, wq_ref, wkv_ref, wg_ref, wr_ref, tabs, kvw_ref, gate_ref, pr_ref)
    tm = q.shape[0]
    lane = lax.broadcasted_iota(jnp.int32, (tm, LANES), 1)
    low = lane < HEAD_DIM
    ones_lane = jnp.where(lane == HEAD_DIM, 1.0, 0.0)
    halves = lambda blk: (blk, pltpu.roll(blk, HEAD_DIM, 1))
    for j in range(NSA_WIDTH // LANES):
        for half, x in enumerate(halves(q[:, j * LANES:(j + 1) * LANES] * (ATTN_SCALE * LOG2E))):
            qh_ref[2 * j + half] = jnp.where(low, x, 0.0).astype(BF16)
    own = own_ref[...].astype(F32)
    for g, x in enumerate(halves(kv[2])):
        ksa_ref[g] = jnp.where(low, x, own).astype(BF16)
    for g, x in enumerate(halves(kv[4])):
        kwh_ref[g] = x[:, :HEAD_DIM].astype(BF16)
    for c, col in enumerate((3, 5)):
        for g, x in enumerate(halves(kv[col])):
            va_ref[c, g] = jnp.where(low, x, ones_lane).astype(BF16)


def _norm_proj(x2, g, wq, wkv, wg, wr, tabs, tm, heads_of=None):
    m, d = x2.shape
    cos, sa, sb = tabs
    tab_blocks = cos.shape[0] // tm
    row = lambda i: (i, 0)
    full = lambda i: (0, 0)
    tab = lambda i: (i % tab_blocks, 0)
    in_specs = [pl.BlockSpec((tm, d), row), pl.BlockSpec((1, d), full),
                pl.BlockSpec(wq.shape, full), pl.BlockSpec(wkv.shape, full),
                pl.BlockSpec(wg.shape, full), pl.BlockSpec(wr.shape, full),
                pl.BlockSpec((tm, LANES), tab), pl.BlockSpec((tm, LANES), tab), pl.BlockSpec((tm, LANES), tab)]
    rows_out = [(6 * KV_WIDTH, F32), (LANES, F32), (RWKV_PROJ, F32)]
    if heads_of is None:
        rows_out = [(NSA_WIDTH, F32)] + rows_out
        return pl.pallas_call(
            _norm_proj_kernel, grid=(m // tm,), in_specs=in_specs,
            out_specs=[pl.BlockSpec((tm, w), row) for w, _ in rows_out],
            out_shape=[jax.ShapeDtypeStruct((m, w), dt) for w, dt in rows_out],
            compiler_params=_params("arbitrary"), name="norm_proj",
        )(x2, g, wq, wkv, wg, wr, cos, sa, sb)
    b, t, own = heads_of
    head_shapes = [(NSA_HEADS, LANES), (KV_HEADS, LANES), (KV_HEADS, HEAD_DIM), (2, KV_HEADS, LANES)]
    hspec = lambda lead, w: pl.BlockSpec((None,) + lead + (tm, w),
                                         lambda i: (i // tab_blocks,) + (0,) * len(lead) + (i % tab_blocks, 0))
    return pl.pallas_call(
        _norm_proj_heads_kernel, grid=(m // tm,), in_specs=in_specs + [pl.BlockSpec((tm, LANES), tab)],
        out_specs=[pl.BlockSpec((tm, w), row) for w, _ in rows_out] + [hspec(sh[:-1], sh[-1]) for sh in head_shapes],
        out_shape=[jax.ShapeDtypeStruct((m, w), dt) for w, dt in rows_out] +
                  [jax.ShapeDtypeStruct((b,) + sh[:-1] + (t, sh[-1]), BF16) for sh in head_shapes],
        compiler_params=_params("arbitrary"), name="norm_proj_heads",
    )(x2, g, wq, wkv, wg, wr, cos, sa, sb, own)


def _rope_tables(pos):
    half = ROPE_DIM // 2
    inv_freq = ROPE_THETA ** (-jnp.arange(half, dtype=F32) / half)
    ang = pos.astype(F32)[:, None] * inv_freq[None, :]
    c, s = jnp.cos(ang), jnp.sin(ang)
    n = pos.shape[0]
    one = jnp.ones((n, HEAD_DIM - ROPE_DIM), F32)
    zero = jnp.zeros((n, HEAD_DIM - ROPE_DIM), F32)
    zh = jnp.zeros((n, half), F32)
    cos = jnp.concatenate([c, c, one], axis=1)
    sa = jnp.concatenate([-s, zh, zero], axis=1)
    sb = jnp.concatenate([zh, s, zero], axis=1)
    rep = LANES // HEAD_DIM
    return tuple(jnp.tile(t, (1, rep)) for t in (cos, sa, sb))


CMP_LANES = 2 * KV_WIDTH


def _compress_accumulate(position_rows, n_chunk, pea_ref, peb_ref, wa_ref, wb_ref, acca_ref, accb_ref, row0):
    for c in range(2):
        lanes = slice(c * KV_WIDTH, (c + 1) * KV_WIDTH)
        da = db = None
        for l in range(CMP_STRIDE):
            x = position_rows(c, l)
            xa = (x + pea_ref[c, l:l + 1, :]).astype(BF16)
            xb = (x + peb_ref[c, l:l + 1, :]).astype(BF16)
            ta, tb = _dot(xa, wa_ref[c, l]), _dot(xb, wb_ref[c, l])
            da, db = (ta, tb) if l == 0 else (da + ta, db + tb)
        acca_ref[pl.ds(row0, n_chunk), lanes] = da
        accb_ref[pl.ds(row0, n_chunk), lanes] = db


def _compress_finish(acca_ref, accb_ref, w2_ref, out_ref):
    n = acca_ref.shape[0]
    rows = lax.broadcasted_iota(jnp.int32, (n, 1), 0)
    for c in range(2):
        lanes = slice(c * KV_WIDTH, (c + 1) * KV_WIDTH)
        nxt = pltpu.roll(accb_ref[:, lanes], n - 1, 0)
        h = jax.nn.gelu(acca_ref[:, lanes] + nxt).astype(BF16)
        out_ref[0, :, lanes] = jnp.where(rows < n - 1, _dot(h, w2_ref[c]), 0.0)


def _compress_prompt_kernel(k_ref, v_ref, pea_ref, peb_ref, wa_ref, wb_ref, w2_ref, out_ref, acca_ref, accb_ref):
    n = acca_ref.shape[0]
    refs = (k_ref.at[0], v_ref.at[0])
    strided = lambda c, l: refs[c][pl.ds(l, n, stride=CMP_STRIDE), :]
    _compress_accumulate(strided, n, pea_ref, peb_ref, wa_ref, wb_ref, acca_ref, accb_ref, 0)
    _compress_finish(acca_ref, accb_ref, w2_ref, out_ref)


_CW_NAMES = ("pea", "peb", "wa", "wb", "w2")


def _compress_prompt(kvw3, cw):
    b, t, _ = kvw3.shape
    n = t // CMP_STRIDE
    const = lambda a: pl.BlockSpec(a.shape, lambda i: (0,) * a.ndim)
    return pl.pallas_call(
        _compress_prompt_kernel, grid=(b,),
        in_specs=[pl.BlockSpec((1, t, KV_WIDTH), lambda i: (i, 0, 0)),
                  pl.BlockSpec((1, t, KV_WIDTH), lambda i: (i, 0, 1))] + [const(cw[k]) for k in _CW_NAMES],
        out_specs=pl.BlockSpec((1, n, CMP_LANES), lambda i: (i, 0, 0)),
        out_shape=jax.ShapeDtypeStruct((b, n, CMP_LANES), F32),
        scratch_shapes=[pltpu.VMEM((n, CMP_LANES), F32), pltpu.VMEM((n, CMP_LANES), F32)],
        compiler_params=_params("arbitrary"), name="compress_prompt",
    )(kvw3, kvw3, *[cw[k] for k in _CW_NAMES])


def _compress_weights(pe, w1, w2):
    eye = jnp.eye(KV_HEADS, dtype=F32)
    blockdiag = lambda w: jnp.einsum('...de,gj->...gdje', w, eye).reshape(w.shape[:-2] + (KV_WIDTH, KV_WIDTH))
    lanes = lambda p: jnp.tile(p, (1, 1, KV_HEADS))
    return {"pea": lanes(pe[:, :CMP_STRIDE]), "peb": lanes(pe[:, CMP_STRIDE:]),
            "wa": blockdiag(w1[:, :CMP_STRIDE]).astype(BF16), "wb": blockdiag(w1[:, CMP_STRIDE:]).astype(BF16),
            "w2": blockdiag(w2).astype(BF16)}


def _paged_fetch(page_copies, pages_per_step, steps_per_seq):
    step = pl.program_id(0) * steps_per_seq + pl.program_id(1)
    total = pl.num_programs(0) * steps_per_seq

    def start_all(s):
        def body(j, c):
            for cp in page_copies(s, j):
                cp.start()
            return c
        lax.fori_loop(0, pages_per_step, body, 0)

    @pl.when(step == 0)
    def _():
        start_all(step)

    @pl.when(step + 1 < total)
    def _():
        start_all(step + 1)

    def wait_body(j, c):
        for cp in page_copies(step, j):
            cp.wait()
        return c
    lax.fori_loop(0, pages_per_step, wait_body, 0)
    return step % 2


UNROLL_PAGES = 4


def _page_of(pt_ref, step, j, pages_per_step, steps_per_seq):
    return pt_ref[step // steps_per_seq, (step % steps_per_seq) * pages_per_step + j]


def _compress_sample_kernel(layer, pages_per_step, steps_per_seq, pt_ref, cache_ref, pea_ref, peb_ref, wa_ref,
                            wb_ref, w2_ref, out_ref, buf_ref, sem_ref, rows_ref, acca_ref, accb_ref):
    page_len = cache_ref.shape[3]

    def page_copies(step, j):
        page = _page_of(pt_ref, step, j, pages_per_step, steps_per_seq)
        return [pltpu.make_async_copy(cache_ref.at[layer, page, pl.ds(0, CMP_LANES), :], buf_ref.at[step % 2, j],
                                      sem_ref.at[step % 2])]

    slot = _paged_fetch(page_copies, pages_per_step, steps_per_seq)

    chunks = page_len // CMP_STRIDE

    def to_rows(j, carry):
        c0 = pl.multiple_of(j * chunks, chunks)
        for c in range(2):
            tok = buf_ref[slot, j, c * KV_WIDTH:(c + 1) * KV_WIDTH, :].T
            pos = jnp.swapaxes(tok.reshape(chunks, CMP_STRIDE, KV_WIDTH), 0, 1)
            rows_ref[c, :, pl.ds(c0, chunks), :] = pos
        return carry
    lax.fori_loop(0, pages_per_step, to_rows, 0, unroll=UNROLL_PAGES)

    part = pl.program_id(1)
    n = rows_ref.shape[2]
    row0 = pl.multiple_of(part * n, n)
    _compress_accumulate(lambda c, l: rows_ref[c, l], n, pea_ref, peb_ref, wa_ref, wb_ref, acca_ref, accb_ref, row0)

    @pl.when(part == steps_per_seq - 1)
    def _():
        _compress_finish(acca_ref, accb_ref, w2_ref, out_ref)


def _compress_sample(cache_t, page_table, layer, cw, steps_per_seq=2):
    db, n_pages = page_table.shape
    page = cache_t.shape[3]
    pps = n_pages // steps_per_seq
    n = n_pages * page // CMP_STRIDE
    const = lambda a: pl.BlockSpec(a.shape, lambda i, p, pt: (0,) * a.ndim)
    kern = functools.partial(_compress_sample_kernel, layer, pps, steps_per_seq)
    return pl.pallas_call(
        kern,
        grid_spec=pltpu.PrefetchScalarGridSpec(
            num_scalar_prefetch=1, grid=(db, steps_per_seq),
            in_specs=[pl.BlockSpec(memory_space=pl.ANY)] + [const(cw[k]) for k in _CW_NAMES],
            out_specs=pl.BlockSpec((1, n, CMP_LANES), lambda i, p, pt: (i, 0, 0)),
            scratch_shapes=[pltpu.VMEM((2, pps, CMP_LANES, page), F32), pltpu.SemaphoreType.DMA((2,)),
                            pltpu.VMEM((2, CMP_STRIDE, pps * page // CMP_STRIDE, KV_WIDTH), F32),
                            pltpu.VMEM((n, CMP_LANES), F32), pltpu.VMEM((n, CMP_LANES), F32)]),
        out_shape=jax.ShapeDtypeStruct((db, n, CMP_LANES), F32),
        compiler_params=_params("arbitrary", "arbitrary"), name="compress_sample",
    )(page_table, cache_t, *[cw[k] for k in _CW_NAMES])


def _cmp_to_sel(n_cmp_pad, n_cmp, n_sel_pad, n_sel):
    c0 = np.arange(n_cmp_pad)[:, None] * CMP_STRIDE
    s0 = np.arange(n_sel_pad)[None, :] * SEL_BLOCK
    hit = (c0 < s0 + SEL_BLOCK) & (c0 + CMP_BLOCK > s0)
    hit &= (np.arange(n_cmp_pad)[:, None] < n_cmp) & (np.arange(n_sel_pad)[None, :] < n_sel)
    return hit.astype(np.float32)


def _block_to_key(n_blk, n_key):
    return (np.arange(n_key)[None, :] // SEL_BLOCK == np.arange(n_blk)[:, None]).astype(np.float32)


def _nsa_prompt_kernel(qb, tk, q_ref, kc_ref, vc_ref, ks_ref, vs_ref, kw_ref, vw_ref, gate_ref, c2s_ref, hi_ref,
                       o_ref):
    i = pl.program_id(2)
    qs = pl.multiple_of(i * qb, qb)
    rows = HEADS_PER_KV * qb
    n_cmp = kc_ref.shape[0]
    n_sel = c2s_ref.shape[0]
    q_pad = q_ref[...].reshape(rows, 2 * HEAD_DIM)
    q = q_pad[:, :HEAD_DIM]
    qpos = qs + lax.broadcasted_iota(jnp.int32, (rows, 1), 0) % qb

    def biased(s, bias):
        return (s.reshape(HEADS_PER_KV, qb, s.shape[-1]) + bias[None]).reshape(s.shape)

    def weights(s, m):
        return jnp.exp2((s - m).astype(BF16))

    def normalised(acc):
        return acc[:, :HEAD_DIM] / jnp.maximum(acc[:, HEAD_DIM:HEAD_DIM + 1], 1e-30)

    s = _dot_nt(q, kc_ref[...])
    cmp_end = lax.broadcasted_iota(jnp.int32, (1, n_cmp), 1) * CMP_STRIDE + (CMP_BLOCK - 1)
    p = _masked_softmax(s, cmp_end <= qpos)
    o_cmp = _dot(p.astype(BF16), vc_ref[...])
    psum = p[0:qb]
    for h in range(1, HEADS_PER_KV):
        psum = psum + p[h * qb:(h + 1) * qb]
    imp_t = _dot_nt(c2s_ref[...], psum, precision=HIGHEST)

    blk = lax.broadcasted_iota(jnp.int32, (n_sel, qb), 0)
    qp = qs + lax.broadcasted_iota(jnp.int32, (n_sel, qb), 1)
    cur = qp // SEL_BLOCK
    forced = (blk == 0) | (blk == cur) | (blk == cur - 1)
    valid = blk <= cur
    score = jnp.where(valid, jnp.where(forced, FORCED_SCORE, imp_t), NEG)
    cnt = jnp.zeros((n_sel, qb), jnp.int32)
    for j in range(n_sel):
        sj = score[j:j + 1, :]
        ahead = (sj > score) | ((sj == score) & (j < blk))
        cnt = cnt + ahead.astype(jnp.int32)
    unsel_t = jnp.where((cnt < SEL_TOPN) & valid & (blk < qs // SEL_BLOCK), 0.0, 1.0).astype(BF16)

    q_sel = (q_pad.astype(F32) + jnp.concatenate([_dot_tn(unsel_t, hi_ref[...])] * HEADS_PER_KV, axis=0)).astype(BF16)
    qrow = lax.broadcasted_iota(jnp.int32, (qb, qb), 0)
    kcol = lax.broadcasted_iota(jnp.int32, (qb, qb), 1)
    s = biased(_dot_nt(q_pad, ks_ref[pl.ds(qs, qb), :]), jnp.where(kcol <= qrow, 0.0, NEG))
    m = jnp.max(s, axis=-1, keepdims=True)
    acc = _dot(weights(s, m), vs_ref[pl.ds(qs, qb), :])

    def sel_step(kt, carry):
        m, acc = carry
        k0 = pl.multiple_of(kt * tk, tk)
        s = _dot_nt(q_sel, ks_ref[pl.ds(k0, tk), :])
        m_new = jnp.maximum(m, jnp.max(s, axis=-1, keepdims=True))
        acc = jnp.exp2(m - m_new) * acc + _dot(weights(s, m_new), vs_ref[pl.ds(k0, tk), :])
        return m_new, acc

    _, acc = lax.fori_loop(0, (qs + tk - 1) // tk, sel_step, (m, acc))
    o_sel = normalised(acc)

    nw = WINDOW + qb
    w0 = pl.multiple_of(jnp.maximum(qs - WINDOW, 0), qb)
    rel = (qs + lax.broadcasted_iota(jnp.int32, (qb, nw), 0)) - (w0 + lax.broadcasted_iota(jnp.int32, (qb, nw), 1))
    s = biased(_dot_nt(q, kw_ref[pl.ds(w0, nw), :]), jnp.where((rel >= 0) & (rel < WINDOW), 0.0, NEG))
    o_win = normalised(_dot(weights(s, jnp.max(s, axis=-1, keepdims=True)), vw_ref[pl.ds(w0, nw), :]))

    g = gate_ref[...].reshape(rows, N_BRANCH)
    o = (g[:, 0:1] * o_cmp + g[:, 1:2] * o_sel + g[:, 2:3] * o_win).astype(o_ref.dtype)
    for h in range(HEADS_PER_KV):
        o_ref[:, h * HEAD_DIM:(h + 1) * HEAD_DIM] = o[h * qb:(h + 1) * qb]


def _nsa_prompt(qh, ksa, kw, va, cvh, gates_h, qb=128, tk=512):
    b, _, t, _ = qh.shape
    n_cmp = cvh.shape[3]
    n_sel = t // SEL_BLOCK
    assert qb % SEL_BLOCK == 0 and t % tk == 0 and WINDOW % qb == 0 and WINDOW + qb <= t and n_sel <= HEAD_DIM
    c2s = jnp.asarray(_cmp_to_sel(n_cmp, n_cmp - 1, n_sel, n_sel).T)
    hi = jnp.asarray(np.concatenate([np.zeros((n_sel, HEAD_DIM)), np.eye(n_sel, HEAD_DIM)], axis=1), dtype=BF16)
    vspec = lambda c: pl.BlockSpec((None, None, None, t, 2 * HEAD_DIM), lambda bi, g, i: (bi, c, g, 0, 0))
    ccomp = lambda c: pl.BlockSpec((None, None, None, n_cmp, HEAD_DIM), lambda bi, g, i: (bi, c, g, 0, 0))
    qspec = pl.BlockSpec((None, HEADS_PER_KV, qb, 2 * HEAD_DIM), lambda bi, g, i: (bi, g, i, 0))
    ospec = pl.BlockSpec((None, qb, HEADS_PER_KV * HEAD_DIM), lambda bi, g, i: (bi, i, g))
    kern = functools.partial(_nsa_prompt_kernel, qb, tk)
    return pl.pallas_call(
        kern, grid=(b, KV_HEADS, t // qb),
        in_specs=[qspec, ccomp(0), ccomp(1),
                  pl.BlockSpec((None, None, t, 2 * HEAD_DIM), lambda bi, g, i: (bi, g, 0, 0)), vspec(0),
                  pl.BlockSpec((None, None, t, HEAD_DIM), lambda bi, g, i: (bi, g, 0, 0)), vspec(1),
                  pl.BlockSpec((None, HEADS_PER_KV, qb, N_BRANCH), lambda bi, g, i: (bi, g, i, 0)),
                  pl.BlockSpec(c2s.shape, lambda bi, g, i: (0, 0)), pl.BlockSpec(hi.shape, lambda bi, g, i: (0, 0))],
        out_specs=ospec,
        out_shape=jax.ShapeDtypeStruct((b, t, NSA_WIDTH), BF16),
        compiler_params=_params("arbitrary", "arbitrary", "arbitrary"), name="nsa_prompt",
    )(qh, cvh, cvh, ksa, va, kw, va, gates_h, c2s, hi)


SROWS = 8 * KV_HEADS


def _sample_cmp_win_kernel(past_len, n_cmp, qz_ref, kvc_ref, win_ref, wnew_ref, c2s_ref, ocmp_ref, owin_ref,
                           imp_ref):
    qz = qz_ref[0]
    kvc = kvc_ref[0]
    n_pad = kvc.shape[0]
    kc, vc = kvc[:, :KV_WIDTH].astype(BF16), kvc[:, KV_WIDTH:].astype(BF16)
    s = _dot_nt(qz, kc)
    n = lax.broadcasted_iota(jnp.int32, (1, n_pad), 1)
    p = _masked_softmax(s, (n < n_cmp) & (n * CMP_STRIDE + (CMP_BLOCK - 1) <= past_len))
    ocmp_ref[0] = _dot(p.astype(BF16), vc)
    r = lax.broadcasted_iota(jnp.int32, (SROWS, 1), 0)
    p = jnp.where(r % 8 < HEADS_PER_KV, p, 0.0)
    psum = jnp.concatenate([jnp.sum(p[8 * g:8 * g + 8], axis=0, keepdims=True) for g in range(KV_HEADS)] +
                           [jnp.zeros((8 - KV_HEADS, n_pad), F32)], axis=0)
    imp_ref[0] = _dot(psum, c2s_ref[...], precision=HIGHEST)

    win = win_ref[0]
    wb = win.shape[1]
    kw, vw = win[:KV_WIDTH].astype(BF16), win[KV_WIDTH:].astype(BF16)
    wnew = wnew_ref[0]
    s = _dot(qz, kw)
    s_new = jnp.sum(qz.astype(F32) * wnew[:, :KV_WIDTH].astype(BF16).astype(F32), axis=-1, keepdims=True)
    rel = wb - lax.broadcasted_iota(jnp.int32, (1, wb), 1)
    msk = (rel >= 0) & (rel < WINDOW)
    s = jnp.where(msk, s, NEG)
    m = jnp.maximum(jnp.max(s, axis=-1, keepdims=True), s_new)
    e = jnp.where(msk, jnp.exp2(s - m), 0.0)
    e_new = jnp.exp2(s_new - m)
    den = jnp.maximum(jnp.sum(e, axis=-1, keepdims=True) + e_new, 1e-30)
    pv = _dot_nt((e / den).astype(BF16), vw)
    pn = (e_new / den).astype(BF16).astype(F32)
    owin_ref[0] = pv + pn * wnew[:, KV_WIDTH:].astype(BF16).astype(F32)


def _sample_cmp_win(past_len, n_cmp, qz, kvc, win, wnew, c2s):
    db = qz.shape[0]
    blk3 = lambda a: pl.BlockSpec((1,) + a.shape[1:], lambda i: (i, 0, 0))
    kern = functools.partial(_sample_cmp_win_kernel, past_len, n_cmp)
    outs = (jax.ShapeDtypeStruct((db, SROWS, KV_WIDTH), F32), jax.ShapeDtypeStruct((db, SROWS, KV_WIDTH), F32),
            jax.ShapeDtypeStruct((db, 8, c2s.shape[1]), F32))
    return pl.pallas_call(
        kern, grid=(db,),
        in_specs=[blk3(qz), blk3(kvc), blk3(win), blk3(wnew), pl.BlockSpec(c2s.shape, lambda i: (0, 0))],
        out_specs=[pl.BlockSpec((1, SROWS, KV_WIDTH), lambda i: (i, 0, 0)),
                   pl.BlockSpec((1, SROWS, KV_WIDTH), lambda i: (i, 0, 0)),
                   pl.BlockSpec((1, 8, c2s.shape[1]), lambda i: (i, 0, 0))],
        out_shape=outs, compiler_params=_params("arbitrary"), name="sample_cmp_win",
    )(qz, kvc, win, wnew, c2s)


def _select_kernel(n_sel, cur, imp_ref, sel_ref):
    imp = imp_ref[...]
    blk = lax.broadcasted_iota(jnp.int32, imp.shape, 1)
    forced = (blk == 0) | (blk == cur) | (blk == cur - 1)
    valid = blk <= cur
    score = jnp.where(valid, jnp.where(forced, FORCED_SCORE, imp), NEG)

    def body(j, cnt):
        sj = jnp.sum(jnp.where(blk == j, score, 0.0), axis=-1, keepdims=True)
        ahead = (sj > score) | ((sj == score) & (j < blk))
        return cnt + ahead.astype(jnp.int32)

    cnt = lax.fori_loop(0, n_sel, body, jnp.zeros(imp.shape, jnp.int32))
    sel_ref[...] = jnp.where((cnt < SEL_TOPN) & valid, 1.0, 0.0)


def _select(imp2, n_sel, cur):
    return pl.pallas_call(
        functools.partial(_select_kernel, n_sel, cur),
        out_shape=jax.ShapeDtypeStruct(imp2.shape, F32), name="sample_select",
        compiler_params=pltpu.CompilerParams(vmem_limit_bytes=VMEM_LIMIT),
    )(imp2)


def _sample_sel_kernel(layer, pages_per_step, steps_per_seq, pt_ref, cache_ref, qz_ref, sel_ref, selnew_ref,
                       knew_ref, b2k_ref, ocmp_ref, owin_ref, gate_ref, o_ref, buf_ref, sem_ref, m_ref, l_ref,
                       acc_ref):
    page_len = cache_ref.shape[3]

    def page_copies(step, j):
        page = _page_of(pt_ref, step, j, pages_per_step, steps_per_seq)
        return [pltpu.make_async_copy(cache_ref.at[layer, page, pl.ds(CMP_LANES, CMP_LANES), :],
                                      buf_ref.at[step % 2, :, pl.ds(pl.multiple_of(j * page_len, page_len), page_len)],
                                      sem_ref.at[step % 2])]

    slot = _paged_fetch(page_copies, pages_per_step, steps_per_seq)
    part = pl.program_id(1)
    qz = qz_ref[0]

    @pl.when(part == 0)
    def _():
        m_ref[...] = jnp.full(m_ref.shape, NEG, F32)
        l_ref[...] = jnp.zeros(l_ref.shape, F32)
        acc_ref[...] = jnp.zeros(acc_ref.shape, F32)

    kv = buf_ref[slot].astype(BF16)
    s = _dot(qz, kv[:KV_WIDTH])
    msk = _dot(sel_ref[0].astype(BF16), b2k_ref[...]) > 0.5
    s = jnp.where(msk, s, NEG)
    m_old = m_ref[...]
    m_new = jnp.maximum(m_old, jnp.max(s, axis=-1, keepdims=True))
    alpha = jnp.exp2(m_old - m_new)
    e = jnp.where(msk, jnp.exp2(s - m_new), 0.0)
    l_ref[...] = alpha * l_ref[...] + jnp.sum(e, axis=-1, keepdims=True)
    acc_ref[...] = alpha * acc_ref[...] + _dot_nt(e.astype(BF16), kv[KV_WIDTH:])
    m_ref[...] = m_new

    @pl.when(part == steps_per_seq - 1)
    def _():
        knew = knew_ref[0].astype(BF16).astype(F32)
        s_new = jnp.sum(qz.astype(F32) * knew[:, :KV_WIDTH], axis=-1, keepdims=True)
        on = selnew_ref[0][:, 0:1] > 0.5
        s_new = jnp.where(on, s_new, NEG)
        m_old = m_ref[...]
        m_new = jnp.maximum(m_old, s_new)
        alpha = jnp.exp2(m_old - m_new)
        e_new = jnp.where(on, jnp.exp2(s_new - m_new), 0.0)
        l = alpha * l_ref[...] + e_new
        acc = alpha * acc_ref[...] + e_new.astype(BF16).astype(F32) * knew[:, KV_WIDTH:]
        o_sel = acc / jnp.maximum(l, 1e-30)
        g = gate_ref[0]
        o_ref[0] = g[:, 0:1] * ocmp_ref[0] + g[:, 1:2] * o_sel + g[:, 2:3] * owin_ref[0]


def _sample_sel(cache_t, page_table, layer, qz, sel16, knew, ocmp, owin, gz, steps_per_seq=2):
    db, n_pages = page_table.shape
    page = cache_t.shape[3]
    pps = n_pages // steps_per_seq
    keys = pps * page
    blocks = keys // SEL_BLOCK
    b2k = jnp.asarray(_block_to_key(blocks, keys), dtype=BF16)
    blk3 = lambda a: pl.BlockSpec((1,) + a.shape[1:], lambda i, p, pt: (i, 0, 0))
    kern = functools.partial(_sample_sel_kernel, layer, pps, steps_per_seq)
    return pl.pallas_call(
        kern,
        grid_spec=pltpu.PrefetchScalarGridSpec(
            num_scalar_prefetch=1, grid=(db, steps_per_seq),
            in_specs=[pl.BlockSpec(memory_space=pl.ANY), blk3(qz),
                      pl.BlockSpec((1, SROWS, blocks), lambda i, p, pt: (i, 0, p)),
                      pl.BlockSpec((1, SROWS, LANES), lambda i, p, pt: (i, 0, steps_per_seq * blocks // LANES)),
                      blk3(knew),
                      pl.BlockSpec(b2k.shape, lambda i, p, pt: (0, 0)), blk3(ocmp), blk3(owin), blk3(gz)],
            out_specs=pl.BlockSpec((1, SROWS, KV_WIDTH), lambda i, p, pt: (i, 0, 0)),
            scratch_shapes=[pltpu.VMEM((2, CMP_LANES, keys), F32), pltpu.SemaphoreType.DMA((2,)),
                            pltpu.VMEM((SROWS, 1), F32), pltpu.VMEM((SROWS, 1), F32),
                            pltpu.VMEM((SROWS, KV_WIDTH), F32)]),
        out_shape=jax.ShapeDtypeStruct((db, SROWS, KV_WIDTH), F32),
        compiler_params=_params("arbitrary", "arbitrary"), name="sample_sel",
    )(page_table, cache_t, qz, sel16, sel16, knew, b2k, ocmp, owin, gz)


def _bdot(dot, a, b):
    return dot(a.astype(BF16), b.astype(BF16))


def _head_sum(x, hm):
    xh = x.astype(BF16)
    xl = (x - xh.astype(F32)).astype(BF16)
    return _dot(xh, hm) + _dot(xl, hm)


def _softplus(z):
    return jnp.maximum(z, 0.0) + jnp.log(1.0 + jnp.exp(-jnp.abs(z)))


CHUNK_GROUP = 4


def _rwkv_kernel(chunk, n_valid, pr_ref, prev_ref, s0_ref, mu_ref, w0_ref, wup_ref, a0_ref, aup_ref, gup_ref,
                 kk_ref, ka_ref, rk_ref, lnw_ref, lnb_ref, hm_ref, tri_ref, out_ref, sout_ref,
                 carry_ref, state_ref, r_ref, k_ref, v_ref, lw_ref, a_ref, b_ref, rh_ref, y_ref, m_ref, n_ref, et_ref):
    ti = pl.program_id(1)
    tt = pr_ref.shape[1]
    w_ = RWKV_WIDTH
    hd = RWKV_HEAD

    @pl.when(ti == 0)
    def _():
        carry_ref[...] = prev_ref[0]
        state_ref[...] = s0_ref[0]

    p = pr_ref[0]
    row = lax.broadcasted_iota(jnp.int32, (tt, 1), 0)
    shifted = jnp.where(row == 0, carry_ref[...], pltpu.roll(p, 1, 0))
    carry_ref[...] = p[tt - 1:tt]
    xs = p + (shifted - p) * mu_ref[...]
    r, k, v = xs[:, :w_], xs[:, w_:2 * w_], xs[:, 2 * w_:3 * w_]
    wa = xs[:, 3 * w_:3 * w_ + DECAY_LORA + AAA_LORA]
    gl = xs[:, 3 * w_ + DECAY_LORA + AAA_LORA:]
    hm = hm_ref[...]
    w = -_softplus(-(w0_ref[...] + _dot(jnp.tanh(wa).astype(BF16), wup_ref[...]))) - 0.5
    lw = -jnp.exp(w)
    a = jax.nn.sigmoid(a0_ref[...] + _dot(wa.astype(BF16), aup_ref[...]))
    g = _dot(jax.nn.sigmoid(gl).astype(BF16), gup_ref[...])
    kk = k * kk_ref[...]
    kk = kk / jnp.maximum(jnp.sqrt(_head_sum(kk * kk, hm)), 1e-12)
    k = k * (1.0 + (a - 1.0) * ka_ref[...])
    av, bv = -kk, kk * a
    if n_valid < tt:
        keep = row < n_valid
        lw, k, v, av, bv = (jnp.where(keep, t, 0.0) for t in (lw, k, v, av, bv))
    r_ref[...], k_ref[...], v_ref[...], lw_ref[...], a_ref[...], b_ref[...] = r, k, v, lw, av, bv

    ci = lax.broadcasted_iota(jnp.int32, (chunk, chunk), 0)
    cj = lax.broadcasted_iota(jnp.int32, (chunk, chunk), 1)
    strict, incl = ci > cj, ci >= cj
    dot = functools.partial(_bdot, _dot)
    dot_nt = functools.partial(_bdot, _dot_nt)
    dot_tn = functools.partial(_bdot, _dot_tn)
    group = min(CHUNK_GROUP, tt // chunk)

    def prepare(cg, carry):
        jobs, pre = [], []
        for ci in range(group):
            c = cg * group + ci
            rows = pl.ds(pl.multiple_of(c * chunk, chunk), chunk)
            lwc = lw_ref[rows, :]
            cs = _dot(tri_ref[...], lwc, precision=HIGHEST)
            e_in, e_out = jnp.exp(cs), jnp.exp(-cs)
            total = cs[chunk - 1:chunk]
            e_end = jnp.exp(total - cs)
            kc, bc = k_ref[rows, :], b_ref[rows, :]
            pre.append(dict(c=c, rows=rows, rt=r_ref[rows, :] * e_in, at=a_ref[rows, :] * jnp.exp(cs - lwc),
                            kt=kc * e_out, bt=bc * e_out, kh=kc * e_end, bh=bc * e_end, vc=v_ref[rows, :]))
            et_ref[c] = jnp.broadcast_to(jnp.exp(total), et_ref.shape[1:])
            jobs += [(ci, h, slice(h * hd, (h + 1) * hd)) for h in range(RWKV_HEADS)]
        nj = range(len(jobs))
        part = lambda name: [pre[ci][name][:, sl] for ci, _, sl in jobs]
        at, rt, bt, kt, bh, kh, vh = (part(n_) for n_ in ("at", "rt", "bt", "kt", "bh", "kh", "vc"))
        gm = [dot_nt(jnp.concatenate([at[j], rt[j]], axis=0), jnp.concatenate([bt[j], kt[j]], axis=0))
              for j in nj]
        lab = [jnp.where(strict, g_[:chunk, :chunk], 0.0) for g_ in gm]
        lak = [jnp.where(strict, g_[:chunk, chunk:], 0.0) for g_ in gm]
        mrb = [jnp.where(incl, g_[chunk:, :chunk], 0.0) for g_ in gm]
        mrk = [jnp.where(incl, g_[chunk:, chunk:], 0.0) for g_ in gm]
        wm = [jnp.concatenate([at[j], dot(lak[j], vh[j])], axis=1) for j in nj]
        x, n = lab, 1
        while 2 * n < chunk:
            xw = [dot(x[j], jnp.concatenate([wm[j], x[j]], axis=1)) for j in nj]
            x = [t[:, 2 * hd:] for t in xw]
            wm = [wm[j] + xw[j][:, :2 * hd] for j in nj]
            n *= 2
        wm = [wm[j] + dot(x[j], wm[j]) for j in nj]
        mw = [dot(mrb[j], wm[j]) for j in nj]
        y0 = [mw[j][:, hd:] + dot(mrk[j], vh[j]) for j in nj]
        wb = [dot_tn(wm[j], bh[j]) for j in nj]
        vk = [dot_tn(vh[j], kh[j]) for j in nj]
        for j, (ci, h, sl) in enumerate(jobs):
            c, rows = pre[ci]["c"], pre[ci]["rows"]
            rh_ref[rows, sl] = rt[j] + mw[j][:, :hd]
            y_ref[rows, sl] = y0[j]
            m_ref[c, h] = wb[j][:hd]
            n_ref[c, h] = wb[j][hd:] + vk[j]
        return carry

    def advance(c, carry):
        rows = pl.ds(pl.multiple_of(c * chunk, chunk), chunk)
        e_tot = et_ref[c]
        sls = [slice(h * hd, (h + 1) * hd) for h in range(RWKV_HEADS)]
        s0 = [state_ref[h] for h in range(RWKV_HEADS)]
        ys = [dot_nt(rh_ref[rows, sl], s0[h]) for h, sl in enumerate(sls)]
        sm = [dot(s0[h], m_ref[c, h]) for h in range(RWKV_HEADS)]
        for h, sl in enumerate(sls):
            y_ref[rows, sl] += ys[h]
            state_ref[h] = s0[h] * e_tot[0:1, sl] + sm[h] + n_ref[c, h]
        return carry

    lax.fori_loop(0, tt // chunk // group, prepare, 0)
    lax.fori_loop(0, tt // chunk, advance, 0)

    y = y_ref[...]
    inv_n = 1.0 / hd
    mean = _head_sum(y, hm) * inv_n
    d = y - mean
    var = _head_sum(d * d, hm) * inv_n
    y = d * lax.rsqrt(var + LNX_EPS) * lnw_ref[...] + lnb_ref[...]
    y = y + _head_sum(r_ref[...] * k_ref[...] * rk_ref[...], hm) * v_ref[...]
    out_ref[0] = (y * g).astype(out_ref.dtype)
    sout_ref[0] = state_ref[...]


def _rwkv(pr3, prev, s0, rw, chunk, tt, n_valid):
    b, t, _ = pr3.shape
    hm = jnp.asarray(np.kron(np.eye(RWKV_HEADS), np.ones((RWKV_HEAD, RWKV_HEAD))), dtype=BF16)
    tri = jnp.asarray(np.tril(np.ones((chunk, chunk))), dtype=F32)
    vec = lambda a: pl.BlockSpec(a.shape, lambda i, j: (0, 0))
    prev3 = prev[:, None, :]
    names = ("mu", "w0", "wup", "a0", "aup", "gup", "kk", "ka", "rk", "lnw", "lnb")
    kern = functools.partial(_rwkv_kernel, chunk, n_valid)
    per_chunk = (tt // chunk, RWKV_HEADS, RWKV_HEAD, RWKV_HEAD)
    return pl.pallas_call(
        kern, grid=(b, t // tt),
        in_specs=[pl.BlockSpec((1, tt, RWKV_PROJ), lambda i, j: (i, j, 0)),
                  pl.BlockSpec((1, 1, RWKV_PROJ), lambda i, j: (i, 0, 0)),
                  pl.BlockSpec((1, RWKV_HEADS, RWKV_HEAD, RWKV_HEAD), lambda i, j: (i, 0, 0, 0))] +
                 [vec(rw[n]) for n in names] + [vec(hm), vec(tri)],
        out_specs=[pl.BlockSpec((1, tt, RWKV_WIDTH), lambda i, j: (i, j, 0)),
                   pl.BlockSpec((1, RWKV_HEADS, RWKV_HEAD, RWKV_HEAD), lambda i, j: (i, 0, 0, 0))],
        out_shape=(jax.ShapeDtypeStruct((b, t, RWKV_WIDTH), BF16),
                   jax.ShapeDtypeStruct((b, RWKV_HEADS, RWKV_HEAD, RWKV_HEAD), F32)),
        scratch_shapes=[pltpu.VMEM((1, RWKV_PROJ), F32), pltpu.VMEM((RWKV_HEADS, RWKV_HEAD, RWKV_HEAD), F32)] +
                       [pltpu.VMEM((tt, RWKV_WIDTH), F32) for _ in range(8)] +
                       [pltpu.VMEM(per_chunk, F32), pltpu.VMEM(per_chunk, F32),
                        pltpu.VMEM((tt // chunk, 8, RWKV_WIDTH), F32)],
        compiler_params=_params("arbitrary", "arbitrary"), name="rwkv",
    )(pr3, prev3, s0, *[rw[n] for n in names], hm, tri)


def _ffn_head(x_ref, nsa_ref, rwkv_ref, wo_ref, gpost_ref, gpre_ref, x1_ref, h2_ref, acc_ref):
    n = nsa_ref.shape[1]
    mixed = _dot(nsa_ref[...], wo_ref[:n, :]) + _dot(rwkv_ref[...], wo_ref[n:, :])
    x1 = x_ref[...] + _rms(mixed, gpost_ref[...])
    x1_ref[...] = x1
    h2_ref[...] = _rms(x1, gpre_ref[...]).astype(BF16)
    acc_ref[...] = jnp.zeros(acc_ref.shape, F32)


def _ffn_tail(gate2, gate1, gate, val, cw_ref, cb_ref, wd_ref, acc_ref):
    conv = cb_ref[...] + (gate2 * cw_ref[0:1, :] + gate1 * cw_ref[1:2, :] + gate * cw_ref[2:3, :])
    acc_ref[...] += _dot((jax.nn.gelu(conv) * val).astype(BF16), wd_ref[...])


def _ffn_seq_kernel(tiles_per_seq, x_ref, nsa_ref, rwkv_ref, wo_ref, gpost_ref, gpre_ref, gfpost_ref, wg_ref, wv_ref, cw_ref,
                    cb_ref, wd_ref, out_ref, tail_ref, x1_ref, h2_ref, acc_ref, carry_ref):
    i, j = pl.program_id(0), pl.program_id(1)
    tm = x_ref.shape[0]

    @pl.when(j == 0)
    def _():
        _ffn_head(x_ref, nsa_ref, rwkv_ref, wo_ref, gpost_ref, gpre_ref, x1_ref, h2_ref, acc_ref)

    h2 = h2_ref[...]
    gate, val = _dot(h2, wg_ref[...]), _dot(h2, wv_ref[...])
    @pl.when(i % tiles_per_seq == 0)
    def _():
        carry_ref[j] = jnp.zeros(carry_ref.shape[1:], F32)

    prev = carry_ref[j]
    row = lax.broadcasted_iota(jnp.int32, (tm, 1), 0)
    gate1 = jnp.where(row == 0, prev[7:8], pltpu.roll(gate, 1, 0))
    gate2 = jnp.where(row == 0, prev[6:7], jnp.where(row == 1, prev[7:8], pltpu.roll(gate, 2, 0)))
    carry_ref[j] = gate[tm - 8:tm]
    tail_ref[0] = gate[tm - 8:tm]
    _ffn_tail(gate2, gate1, gate, val, cw_ref, cb_ref, wd_ref, acc_ref)

    @pl.when(j == pl.num_programs(1) - 1)
    def _():
        out_ref[...] = x1_ref[...] + _rms(acc_ref[...], gfpost_ref[...])


def _ffn_step_kernel(x_ref, nsa_ref, rwkv_ref, wo_ref, gpost_ref, gpre_ref, gfpost_ref, wg_ref, wv_ref, cw_ref, cb_ref,
                     wd_ref, prev_ref, out_ref, gate_ref, x1_ref, h2_ref, acc_ref):
    j = pl.program_id(0)

    @pl.when(j == 0)
    def _():
        _ffn_head(x_ref, nsa_ref, rwkv_ref, wo_ref, gpost_ref, gpre_ref, x1_ref, h2_ref, acc_ref)

    h2 = h2_ref[...]
    gate, val = _dot(h2, wg_ref[...]), _dot(h2, wv_ref[...])
    gate_ref[...] = gate
    _ffn_tail(prev_ref[0], prev_ref[1], gate, val, cw_ref, cb_ref, wd_ref, acc_ref)

    @pl.when(j == pl.num_programs(0) - 1)
    def _():
        out_ref[...] = x1_ref[...] + _rms(acc_ref[...], gfpost_ref[...])


def _ffn_seq(x2, nsa2, rwkv2, fw, seq_len, tm, tf):
    m, d = x2.shape
    d_ff = fw["wd"].shape[0]
    nf = d_ff // tf
    row = lambda i, j: (i, 0)
    full = lambda i, j: (0, 0)
    tiles_per_seq = seq_len // tm
    scratch = [pltpu.VMEM((tm, d), F32), pltpu.VMEM((tm, d), BF16), pltpu.VMEM((tm, d), F32),
               pltpu.VMEM((nf, 8, tf), F32)]
    return pl.pallas_call(
        functools.partial(_ffn_seq_kernel, tiles_per_seq), grid=(m // tm, nf),
        in_specs=[pl.BlockSpec((tm, d), row), pl.BlockSpec((tm, nsa2.shape[1]), row),
                  pl.BlockSpec((tm, rwkv2.shape[1]), row), pl.BlockSpec((d, d), full),
                  pl.BlockSpec((1, d), full), pl.BlockSpec((1, d), full), pl.BlockSpec((1, d), full),
                  pl.BlockSpec((d, tf), lambda i, j: (0, j)), pl.BlockSpec((d, tf), lambda i, j: (0, nf + j)),
                  pl.BlockSpec((CONV_WIDTH, tf), lambda i, j: (0, j)), pl.BlockSpec((1, tf), lambda i, j: (0, j)),
                  pl.BlockSpec((tf, d), lambda i, j: (j, 0))],
        out_specs=[pl.BlockSpec((tm, d), row), pl.BlockSpec((1, 8, tf), lambda i, j: (i, 0, j))],
        out_shape=(jax.ShapeDtypeStruct((m, d), F32), jax.ShapeDtypeStruct((m // tm, 8, d_ff), F32)),
        scratch_shapes=scratch, compiler_params=_params("arbitrary", "arbitrary"), name="ffn_seq",
    )(x2, nsa2, rwkv2, fw["wo"], fw["gpost"], fw["gpre"], fw["gfpost"], fw["wup"], fw["wup"], fw["cw"], fw["cb"],
      fw["wd"])


def _ffn_step(x2, nsa2, rwkv2, fw, prev, tf):
    m, d = x2.shape
    d_ff = fw["wd"].shape[0]
    nf = d_ff // tf
    full = lambda j: (0, 0)
    scratch = [pltpu.VMEM((m, d), F32), pltpu.VMEM((m, d), BF16), pltpu.VMEM((m, d), F32)]
    return pl.pallas_call(
        _ffn_step_kernel, grid=(nf,),
        in_specs=[pl.BlockSpec((m, d), full), pl.BlockSpec(nsa2.shape, full), pl.BlockSpec(rwkv2.shape, full),
                  pl.BlockSpec((d, d), full),
                  pl.BlockSpec((1, d), full), pl.BlockSpec((1, d), full), pl.BlockSpec((1, d), full),
                  pl.BlockSpec((d, tf), lambda j: (0, j)), pl.BlockSpec((d, tf), lambda j: (0, nf + j)),
                  pl.BlockSpec((CONV_WIDTH, tf), lambda j: (0, j)), pl.BlockSpec((1, tf), lambda j: (0, j)),
                  pl.BlockSpec((tf, d), lambda j: (j, 0)),
                  pl.BlockSpec((CONV_WIDTH - 1, m, tf), lambda j: (0, 0, j))],
        out_specs=[pl.BlockSpec((m, d), full), pl.BlockSpec((m, tf), lambda j: (0, j))],
        out_shape=(jax.ShapeDtypeStruct((m, d), F32), jax.ShapeDtypeStruct((m, d_ff), F32)),
        scratch_shapes=scratch, compiler_params=_params("arbitrary"), name="ffn_step",
    )(x2, nsa2, rwkv2, fw["wo"], fw["gpost"], fw["gpre"], fw["gfpost"], fw["wup"], fw["wup"], fw["cw"], fw["cb"],
      fw["wd"], prev)


def _layer_weights(l, p):
    w_in = p["w_in"][l]
    o = NSA_WIDTH + 6 * KV_WIDTH
    wg = jnp.pad(w_in[:, o:NSA_PROJ], ((0, 0), (0, LANES - N_BRANCH * NSA_HEADS)))
    row = lambda a: a[l][None, :].astype(F32)
    zpad = jnp.zeros((AAA_LORA, RWKV_WIDTH), F32)
    rw = {"mu": row(p["rwkv_mu"]), "w0": row(p["rwkv_w0"]),
          "wup": jnp.concatenate([p["rwkv_w_up"][l], zpad], axis=0).astype(BF16),
          "a0": row(p["rwkv_a0"]), "aup": jnp.concatenate([zpad, p["rwkv_a_up"][l]], axis=0).astype(BF16),
          "gup": p["rwkv_g_up"][l].astype(BF16), "kk": row(p["rwkv_k_k"]), "ka": row(p["rwkv_k_a"]),
          "rk": p["rwkv_r_k"][l].reshape(1, RWKV_WIDTH), "lnw": row(p["rwkv_lnx_w"]), "lnb": row(p["rwkv_lnx_b"])}
    fw = {"wo": p["w_out"][l].astype(BF16), "gpost": row(p["norm_mix_post"]), "gpre": row(p["norm_ffn_pre"]),
          "gfpost": row(p["norm_ffn_post"]), "wup": p["ffn_w_up"][l].astype(BF16), "cw": p["ffn_conv_w"][l],
          "cb": row(p["ffn_conv_b"]), "wd": p["ffn_w_down"][l].astype(BF16)}
    return {"g_in": row(p["norm_mix_pre"]), "wq": w_in[:, :NSA_WIDTH].astype(BF16),
            "wkv": w_in[:, NSA_WIDTH:o].astype(BF16), "wg": wg.astype(BF16), "wr": w_in[:, NSA_PROJ:].astype(BF16),
            "cw": _compress_weights(p["cmp_pe"][l], p["cmp_w1"][l], p["cmp_w2"][l]), "rw": rw, "fw": fw}


def _prompt_layer(x, lw, tabs):
    b, t, d = x.shape
    m = b * t
    x2 = x.reshape(m, d)
    own_block = jnp.asarray(np.concatenate([np.zeros((t, HEAD_DIM)), NEG * _block_to_key(HEAD_DIM, t).T], axis=1),
                            dtype=BF16)
    kvw, gates, pr, qh, ksa, kw, va = _norm_proj(x2, lw["g_in"], lw["wq"], lw["wkv"], lw["wg"], lw["wr"], tabs,
                                                 tm=256, heads_of=(b, t, own_block))
    kvw3 = kvw.reshape(b, t, 6 * KV_WIDTH)
    kvc = _compress_prompt(kvw3, lw["cw"])
    cvh = kvc.astype(BF16).reshape(b, -1, 2, KV_HEADS, HEAD_DIM).transpose(0, 2, 3, 1, 4)
    gates_h = gates[:, :N_BRANCH * NSA_HEADS].reshape(b, t, NSA_HEADS, N_BRANCH).transpose(0, 2, 1, 3)
    o_nsa = _nsa_prompt(qh, ksa, kw, va, cvh, gates_h).reshape(m, NSA_WIDTH)
    s0 = jnp.zeros((b, RWKV_HEADS, RWKV_HEAD, RWKV_HEAD), F32)
    pr3 = pr.reshape(b, t, RWKV_PROJ)
    o_rwkv, wkv = _rwkv(pr3, jnp.zeros((b, RWKV_PROJ), F32), s0, lw["rw"], chunk=64, tt=min(t, 512), n_valid=t)
    d_ff = lw["fw"]["wd"].shape[0]
    tm = min(t, 512)
    y, tail = _ffn_seq(x2, o_nsa, o_rwkv.reshape(m, RWKV_WIDTH), lw["fw"], t, tm=tm, tf=d_ff // 2)
    tail = tail[t // tm - 1::t // tm]
    kv_rows = kvw3[:, :, :4 * KV_WIDTH].reshape(b, t, 4, KV_HEADS, HEAD_DIM)
    wn = min(WINDOW, t)
    win = kvw3[:, t - wn:, 4 * KV_WIDTH:].reshape(b, wn, 2, KV_HEADS, HEAD_DIM)
    return y.reshape(b, t, d), kv_rows, win, wkv, pr3[:, -1], tail[:, 8 - (CONV_WIDTH - 1):]


def _sample_layer(x, lw, tabs, l, cache_t, page_table, st_win, st_wkv, st_shift, st_conv):
    db, tn, d = x.shape
    past_len = page_table.shape[1] * cache_t.shape[3]
    x2 = x.reshape(db, d)
    q, kvw, gates, pr = _norm_proj(x2, lw["g_in"], lw["wq"], lw["wkv"], lw["wg"], lw["wr"], tabs, tm=db)
    qs = (q * (ATTN_SCALE * LOG2E)).astype(BF16).reshape(db, KV_HEADS, HEADS_PER_KV, HEAD_DIM)
    eye = jnp.eye(KV_HEADS, dtype=BF16)
    qz = jnp.einsum('bghd,gj->bghjd', qs, eye).reshape(db, KV_HEADS, HEADS_PER_KV, KV_WIDTH)
    qz = jnp.pad(qz, ((0, 0), (0, 0), (0, 8 - HEADS_PER_KV), (0, 0))).reshape(db, SROWS, KV_WIDTH)
    kvc = _compress_sample(cache_t, page_table, l, lw["cw"])
    n_cmp = past_len // CMP_STRIDE - 1
    n_sel = -(-(past_len + tn) // SEL_BLOCK)
    n_sel_pad = -(-n_sel // LANES) * LANES
    c2s = jnp.asarray(_cmp_to_sel(kvc.shape[1], n_cmp, n_sel_pad, n_sel))
    wb = st_win.shape[1]
    win = st_win.transpose(0, 2, 3, 4, 1).reshape(db, 2 * KV_WIDTH, wb)
    wnew = kvw[:, None, 4 * KV_WIDTH:]
    ocmp, owin, imp = _sample_cmp_win(past_len, n_cmp, qz, kvc, win, wnew, c2s)
    sel = _select(imp.reshape(db * 8, n_sel_pad), n_sel, past_len // SEL_BLOCK).reshape(db, 8, n_sel_pad)
    sel16 = jnp.broadcast_to(sel[:, :KV_HEADS, None, :], (db, KV_HEADS, 8, n_sel_pad)).reshape(db, SROWS, n_sel_pad)
    gz = gates[:, :N_BRANCH * NSA_HEADS].reshape(db, KV_HEADS, HEADS_PER_KV, N_BRANCH)
    gz = jnp.pad(gz, ((0, 0), (0, 0), (0, 8 - HEADS_PER_KV), (0, 0))).reshape(db, SROWS, N_BRANCH)
    knew = kvw[:, None, 2 * KV_WIDTH:4 * KV_WIDTH]
    o16 = _sample_sel(cache_t, page_table, l, qz, sel16, knew, ocmp, owin, gz)
    o4 = o16.reshape(db, KV_HEADS, 8, KV_HEADS, HEAD_DIM)[:, :, :HEADS_PER_KV]
    o_nsa = jnp.stack([o4[:, g, :, g] for g in range(KV_HEADS)], axis=1).reshape(db, NSA_WIDTH)
    pad_t = 8
    pr3 = jnp.pad(pr[:, None, :], ((0, 0), (0, pad_t - tn), (0, 0)))
    o_rwkv, wkv = _rwkv(pr3, st_shift, st_wkv, lw["rw"], chunk=pad_t, tt=pad_t, n_valid=tn)
    prev = st_conv.transpose(1, 0, 2)
    d_ff = lw["fw"]["wd"].shape[0]
    y, gate = _ffn_step(x2, o_nsa.astype(BF16), o_rwkv[:, 0], lw["fw"], prev, tf=d_ff // 2)
    kv_rows = kvw[:, :4 * KV_WIDTH].reshape(db, tn, 4, KV_HEADS, HEAD_DIM)
    win_new = jnp.concatenate([st_win[:, tn:], kvw[:, 4 * KV_WIDTH:].reshape(db, tn, 2, KV_HEADS, HEAD_DIM)], axis=1)
    conv_new = jnp.concatenate([st_conv[:, 1:], gate[:, None, :]], axis=1)
    return y.reshape(db, tn, d), kv_rows, win_new, wkv, pr, conv_new


def kernel(x_prompt, x_sample, cache_kv, page_table, state_win, state_wkv, state_shift, state_conv, norm_mix_pre, norm_mix_post, norm_ffn_pre, norm_ffn_post, w_in, w_out, cmp_pe, cmp_w1, cmp_w2, rwkv_mu, rwkv_w0, rwkv_w_up, rwkv_a0, rwkv_a_up, rwkv_g_up, rwkv_k_k, rwkv_k_a, rwkv_r_k, rwkv_lnx_w, rwkv_lnx_b, ffn_w_up, ffn_conv_w, ffn_conv_b, ffn_w_down):
    p = dict(norm_mix_pre=norm_mix_pre, norm_mix_post=norm_mix_post, norm_ffn_pre=norm_ffn_pre,
             norm_ffn_post=norm_ffn_post, w_in=w_in, w_out=w_out, cmp_pe=cmp_pe, cmp_w1=cmp_w1, cmp_w2=cmp_w2,
             rwkv_mu=rwkv_mu, rwkv_w0=rwkv_w0, rwkv_w_up=rwkv_w_up, rwkv_a0=rwkv_a0, rwkv_a_up=rwkv_a_up,
             rwkv_g_up=rwkv_g_up, rwkv_k_k=rwkv_k_k, rwkv_k_a=rwkv_k_a, rwkv_r_k=rwkv_r_k,
             rwkv_lnx_w=rwkv_lnx_w, rwkv_lnx_b=rwkv_lnx_b, ffn_w_up=ffn_w_up, ffn_conv_w=ffn_conv_w,
             ffn_conv_b=ffn_conv_b, ffn_w_down=ffn_w_down)
    depth = w_in.shape[0]
    b, t, _ = x_prompt.shape
    db, tn, _ = x_sample.shape
    n_pool, page = cache_kv.shape[1], cache_kv.shape[2]
    past_len = page_table.shape[1] * page
    cache_t = cache_kv.transpose(0, 1, 3, 4, 5, 2).reshape(depth, n_pool, 2 * CMP_LANES, page)
    tabs_p = _rope_tables(jnp.arange(t, dtype=jnp.int32))
    tabs_s = _rope_tables(jnp.full((db,), past_len, jnp.int32))
    xp, xs = x_prompt, x_sample
    outs_p, outs_s = [], []
    for l in range(depth):
        lw = _layer_weights(l, p)
        xp, *st_p = _prompt_layer(xp, lw, tabs_p)
        xs, *st_s = _sample_layer(xs, lw, tabs_s, l, cache_t, page_table, state_win[l], state_wkv[l],
                                  state_shift[l], state_conv[l])
        outs_p.append(st_p)
        outs_s.append(st_s)
    stack = lambda outs, i: jnp.stack([o[i] for o in outs])
    return (xp, xs, stack(outs_p, 0), stack(outs_s, 0), stack(outs_p, 1), stack(outs_s, 1),
            stack(outs_p, 2), stack(outs_s, 2), stack(outs_p, 3), stack(outs_s, 3),
            stack(outs_p, 4), stack(outs_s, 4))
```

```python
import functools

import jax
import jax.numpy as jnp
import numpy as np
from jax import lax
from jax.experimental import pallas as pl
from jax.experimental.pallas import tpu as pltpu

F32, BF16 = jnp.float32, jnp.bfloat16
HIGHEST = lax.Precision.HIGHEST

LANES = 128
VMEM_LIMIT = 56 * 1024 * 1024

HEAD_DIM = 64
NSA_HEADS = 8
KV_HEADS = 2
HEADS_PER_KV = NSA_HEADS // KV_HEADS
NSA_WIDTH = NSA_HEADS * HEAD_DIM
KV_WIDTH = KV_HEADS * HEAD_DIM
N_BRANCH = 3
CMP_STRIDE = 16
CMP_BLOCK = 2 * CMP_STRIDE
SEL_BLOCK = 64
SEL_TOPN = 16
WINDOW = 512
ROPE_THETA = 500000.0
ROPE_DIM = HEAD_DIM // 4
RWKV_HEAD = 64
RWKV_HEADS = 8
RWKV_WIDTH = RWKV_HEADS * RWKV_HEAD
DECAY_LORA = 64
AAA_LORA = 64
GATE_LORA = 128
RWKV_PROJ = 3 * RWKV_WIDTH + DECAY_LORA + AAA_LORA + GATE_LORA
NSA_PROJ = NSA_WIDTH + 6 * KV_WIDTH + N_BRANCH * NSA_HEADS
CONV_WIDTH = 3
NORM_EPS = 1e-6
LNX_EPS = 64e-5
NEG = -1e30
FORCED_SCORE = 1e9
ATTN_SCALE = HEAD_DIM ** -0.5
LOG2E = 1.4426950408889634


def _dot(a, b, precision=None):
    return jnp.dot(a, b, preferred_element_type=F32, precision=precision)


def _dot_nt(a, b, precision=None):
    return lax.dot_general(a, b, (((1,), (1,)), ((), ())), preferred_element_type=F32, precision=precision)


def _dot_tn(a, b, precision=None):
    return lax.dot_general(a, b, (((0,), (0,)), ((), ())), preferred_element_type=F32, precision=precision)


def _rms(x, g):
    return x * lax.rsqrt(jnp.mean(x * x, axis=-1, keepdims=True) + NORM_EPS) * g


def _masked_softmax(s, mask):
    s = jnp.where(mask, s, NEG)
    m = jnp.max(s, axis=-1, keepdims=True)
    e = jnp.where(mask, jnp.exp2(s - m), 0.0)
    return e / jnp.maximum(jnp.sum(e, axis=-1, keepdims=True), 1e-30)


def _params(*sem):
    return pltpu.CompilerParams(dimension_semantics=sem, vmem_limit_bytes=VMEM_LIMIT)


def _rope(v, cos, sa, sb):
    return v * cos + pltpu.roll(v, LANES - ROPE_DIM // 2, 1) * sa + pltpu.roll(v, ROPE_DIM // 2, 1) * sb


def _project(x_ref, g_ref, wq_ref, wkv_ref, wg_ref, wr_ref, tabs, kvw_ref, gate_ref, pr_ref):
    h = _rms(x_ref[...], g_ref[...]).astype(BF16)
    q = _dot(h, wq_ref[...])
    q = jnp.concatenate([_rope(q[:, j * LANES:(j + 1) * LANES], *tabs) for j in range(NSA_WIDTH // LANES)], axis=1)
    kv = _dot(h, wkv_ref[...])
    blocks = []
    for j in range(6 * KV_WIDTH // LANES):
        blk = kv[:, j * LANES:(j + 1) * LANES]
        blocks.append(_rope(blk, *tabs) if j % 2 == 0 else blk)
        kvw_ref[:, j * LANES:(j + 1) * LANES] = blocks[j]
    gate_ref[...] = jax.nn.sigmoid(_dot(h, wg_ref[...]))
    pr_ref[...] = _dot(h, wr_ref[...])
    return q, blocks


def _norm_proj_kernel(x_ref, g_ref, wq_ref, wkv_ref, wg_ref, wr_ref, cos_ref, sa_ref, sb_ref,
                      q_ref, kvw_ref, gate_ref, pr_ref):
    tabs = (cos_ref[...], sa_ref[...], sb_ref[...])
    q_ref[...], _ = _project(x_ref, g_ref, wq_ref, wkv_ref, wg_ref, wr_ref, tabs, kvw_ref, gate_ref, pr_ref)


def _norm_proj_heads_kernel(x_ref, g_ref, wq_ref, wkv_ref, wg_ref, wr_ref, cos_ref, sa_ref, sb_ref, own_ref,
                            kvw_ref, gate_ref, pr_ref, qh_ref, ksa_ref, kwh_ref, va_ref):
    tabs = (cos_ref[...], sa_ref[...], sb_ref[...])
    q, kv = _project(x_ref, g_ref, wq_ref, wkv_ref, wg_ref, wr_ref, tabs, kvw_ref, gate_ref, pr_ref)
    tm = q.shape[0]
    lane = lax.broadcasted_iota(jnp.int32, (tm, LANES), 1)
    low = lane < HEAD_DIM
    ones_lane = jnp.where(lane == HEAD_DIM, 1.0, 0.0)
    halves = lambda blk: (blk, pltpu.roll(blk, HEAD_DIM, 1))
    for j in range(NSA_WIDTH // LANES):
        for half, x in enumerate(halves(q[:, j * LANES:(j + 1) * LANES] * (ATTN_SCALE * LOG2E))):
            qh_ref[2 * j + half] = jnp.where(low, x, 0.0).astype(BF16)
    own = own_ref[...].astype(F32)
    for g, x in enumerate(halves(kv[2])):
        ksa_ref[g] = jnp.where(low, x, own).astype(BF16)
    for g, x in enumerate(halves(kv[4])):
        kwh_ref[g] = x[:, :HEAD_DIM].astype(BF16)
    for c, col in enumerate((3, 5)):
        for g, x in enumerate(halves(kv[col])):
            va_ref[c, g] = jnp.where(low, x, ones_lane).astype(BF16)


def _norm_proj(x2, g, wq, wkv, wg, wr, tabs, tm, heads_of=None):
    m, d = x2.shape
    cos, sa, sb = tabs
    tab_blocks = cos.shape[0] // tm
    row = lambda i: (i, 0)
    full = lambda i: (0, 0)
    tab = lambda i: (i % tab_blocks, 0)
    in_specs = [pl.BlockSpec((tm, d), row), pl.BlockSpec((1, d), full),
                pl.BlockSpec(wq.shape, full), pl.BlockSpec(wkv.shape, full),
                pl.BlockSpec(wg.shape, full), pl.BlockSpec(wr.shape, full),
                pl.BlockSpec((tm, LANES), tab), pl.BlockSpec((tm, LANES), tab), pl.BlockSpec((tm, LANES), tab)]
    rows_out = [(6 * KV_WIDTH, F32), (LANES, F32), (RWKV_PROJ, F32)]
    if heads_of is None:
        rows_out = [(NSA_WIDTH, F32)] + rows_out
        return pl.pallas_call(
            _norm_proj_kernel, grid=(m // tm,), in_specs=in_specs,
            out_specs=[pl.BlockSpec((tm, w), row) for w, _ in rows_out],
            out_shape=[jax.ShapeDtypeStruct((m, w), dt) for w, dt in rows_out],
            compiler_params=_params("arbitrary"), name="norm_proj",
        )(x2, g, wq, wkv, wg, wr, cos, sa, sb)
    b, t, own = heads_of
    head_shapes = [(NSA_HEADS, LANES), (KV_HEADS, LANES), (KV_HEADS, HEAD_DIM), (2, KV_HEADS, LANES)]
    hspec = lambda lead, w: pl.BlockSpec((None,) + lead + (tm, w),
                                         lambda i: (i // tab_blocks,) + (0,) * len(lead) + (i % tab_blocks, 0))
    return pl.pallas_call(
        _norm_proj_heads_kernel, grid=(m // tm,), in_specs=in_specs + [pl.BlockSpec((tm, LANES), tab)],
        out_specs=[pl.BlockSpec((tm, w), row) for w, _ in rows_out] + [hspec(sh[:-1], sh[-1]) for sh in head_shapes],
        out_shape=[jax.ShapeDtypeStruct((m, w), dt) for w, dt in rows_out] +
                  [jax.ShapeDtypeStruct((b,) + sh[:-1] + (t, sh[-1]), BF16) for sh in head_shapes],
        compiler_params=_params("arbitrary"), name="norm_proj_heads",
    )(x2, g, wq, wkv, wg, wr, cos, sa, sb, own)


def _rope_tables(pos):
    half = ROPE_DIM // 2
    inv_freq = ROPE_THETA ** (-jnp.arange(half, dtype=F32) / half)
    ang = pos.astype(F32)[:, None] * inv_freq[None, :]
    c, s = jnp.cos(ang), jnp.sin(ang)
    n = pos.shape[0]
    one = jnp.ones((n, HEAD_DIM - ROPE_DIM), F32)
    zero = jnp.zeros((n, HEAD_DIM - ROPE_DIM), F32)
    zh = jnp.zeros((n, half), F32)
    cos = jnp.concatenate([c, c, one], axis=1)
    sa = jnp.concatenate([-s, zh, zero], axis=1)
    sb = jnp.concatenate([zh, s, zero], axis=1)
    rep = LANES // HEAD_DIM
    return tuple(jnp.tile(t, (1, rep)) for t in (cos, sa, sb))


CMP_LANES = 2 * KV_WIDTH


def _compress_accumulate(position_rows, n_chunk, pea_ref, peb_ref, wa_ref, wb_ref, acca_ref, accb_ref, row0):
    for c in range(2):
        lanes = slice(c * KV_WIDTH, (c + 1) * KV_WIDTH)
        da = db = None
        for l in range(CMP_STRIDE):
            x = position_rows(c, l)
            xa = (x + pea_ref[c, l:l + 1, :]).astype(BF16)
            xb = (x + peb_ref[c, l:l + 1, :]).astype(BF16)
            ta, tb = _dot(xa, wa_ref[c, l]), _dot(xb, wb_ref[c, l])
            da, db = (ta, tb) if l == 0 else (da + ta, db + tb)
        acca_ref[pl.ds(row0, n_chunk), lanes] = da
        accb_ref[pl.ds(row0, n_chunk), lanes] = db


def _compress_finish(acca_ref, accb_ref, w2_ref, out_ref):
    n = acca_ref.shape[0]
    rows = lax.broadcasted_iota(jnp.int32, (n, 1), 0)
    for c in range(2):
        lanes = slice(c * KV_WIDTH, (c + 1) * KV_WIDTH)
        nxt = pltpu.roll(accb_ref[:, lanes], n - 1, 0)
        h = jax.nn.gelu(acca_ref[:, lanes] + nxt).astype(BF16)
        out_ref[0, :, lanes] = jnp.where(rows < n - 1, _dot(h, w2_ref[c]), 0.0)


def _compress_prompt_kernel(k_ref, v_ref, pea_ref, peb_ref, wa_ref, wb_ref, w2_ref, out_ref, acca_ref, accb_ref):
    n = acca_ref.shape[0]
    refs = (k_ref.at[0], v_ref.at[0])
    strided = lambda c, l: refs[c][pl.ds(l, n, stride=CMP_STRIDE), :]
    _compress_accumulate(strided, n, pea_ref, peb_ref, wa_ref, wb_ref, acca_ref, accb_ref, 0)
    _compress_finish(acca_ref, accb_ref, w2_ref, out_ref)


_CW_NAMES = ("pea", "peb", "wa", "wb", "w2")


def _compress_prompt(kvw3, cw):
    b, t, _ = kvw3.shape
    n = t // CMP_STRIDE
    const = lambda a: pl.BlockSpec(a.shape, lambda i: (0,) * a.ndim)
    return pl.pallas_call(
        _compress_prompt_kernel, grid=(b,),
        in_specs=[pl.BlockSpec((1, t, KV_WIDTH), lambda i: (i, 0, 0)),
                  pl.BlockSpec((1, t, KV_WIDTH), lambda i: (i, 0, 1))] + [const(cw[k]) for k in _CW_NAMES],
        out_specs=pl.BlockSpec((1, n, CMP_LANES), lambda i: (i, 0, 0)),
        out_shape=jax.ShapeDtypeStruct((b, n, CMP_LANES), F32),
        scratch_shapes=[pltpu.VMEM((n, CMP_LANES), F32), pltpu.VMEM((n, CMP_LANES), F32)],
        compiler_params=_params("arbitrary"), name="compress_prompt",
    )(kvw3, kvw3, *[cw[k] for k in _CW_NAMES])


def _compress_weights(pe, w1, w2):
    eye = jnp.eye(KV_HEADS, dtype=F32)
    blockdiag = lambda w: jnp.einsum('...de,gj->...gdje', w, eye).reshape(w.shape[:-2] + (KV_WIDTH, KV_WIDTH))
    lanes = lambda p: jnp.tile(p, (1, 1, KV_HEADS))
    return {"pea": lanes(pe[:, :CMP_STRIDE]), "peb": lanes(pe[:, CMP_STRIDE:]),
            "wa": blockdiag(w1[:, :CMP_STRIDE]).astype(BF16), "wb": blockdiag(w1[:, CMP_STRIDE:]).astype(BF16),
            "w2": blockdiag(w2).astype(BF16)}


def _paged_fetch(page_copies, pages_per_step, steps_per_seq):
    step = pl.program_id(0) * steps_per_seq + pl.program_id(1)
    total = pl.num_programs(0) * steps_per_seq

    def start_all(s):
        def body(j, c):
            for cp in page_copies(s, j):
                cp.start()
            return c
        lax.fori_loop(0, pages_per_step, body, 0)

    @pl.when(step == 0)
    def _():
        start_all(step)

    @pl.when(step + 1 < total)
    def _():
        start_all(step + 1)

    def wait_body(j, c):
        for cp in page_copies(step, j):
            cp.wait()
        return c
    lax.fori_loop(0, pages_per_step, wait_body, 0)
    return step % 2


UNROLL_PAGES = 4


def _page_of(pt_ref, step, j, pages_per_step, steps_per_seq):
    return pt_ref[step // steps_per_seq, (step % steps_per_seq) * pages_per_step + j]


def _compress_sample_kernel(layer, pages_per_step, steps_per_seq, pt_ref, cache_ref, pea_ref, peb_ref, wa_ref,
                            wb_ref, w2_ref, out_ref, buf_ref, sem_ref, rows_ref, acca_ref, accb_ref):
    page_len = cache_ref.shape[3]

    def page_copies(step, j):
        page = _page_of(pt_ref, step, j, pages_per_step, steps_per_seq)
        return [pltpu.make_async_copy(cache_ref.at[layer, page, pl.ds(0, CMP_LANES), :], buf_ref.at[step % 2, j],
                                      sem_ref.at[step % 2])]

    slot = _paged_fetch(page_copies, pages_per_step, steps_per_seq)

    chunks = page_len // CMP_STRIDE

    def to_rows(j, carry):
        c0 = pl.multiple_of(j * chunks, chunks)
        for c in range(2):
            tok = buf_ref[slot, j, c * KV_WIDTH:(c + 1) * KV_WIDTH, :].T
            pos = jnp.swapaxes(tok.reshape(chunks, CMP_STRIDE, KV_WIDTH), 0, 1)
            rows_ref[c, :, pl.ds(c0, chunks), :] = pos
        return carry
    lax.fori_loop(0, pages_per_step, to_rows, 0, unroll=UNROLL_PAGES)

    part = pl.program_id(1)
    n = rows_ref.shape[2]
    row0 = pl.multiple_of(part * n, n)
    _compress_accumulate(lambda c, l: rows_ref[c, l], n, pea_ref, peb_ref, wa_ref, wb_ref, acca_ref, accb_ref, row0)

    @pl.when(part == steps_per_seq - 1)
    def _():
        _compress_finish(acca_ref, accb_ref, w2_ref, out_ref)


def _compress_sample(cache_t, page_table, layer, cw, steps_per_seq=2):
    db, n_pages = page_table.shape
    page = cache_t.shape[3]
    pps = n_pages // steps_per_seq
    n = n_pages * page // CMP_STRIDE
    const = lambda a: pl.BlockSpec(a.shape, lambda i, p, pt: (0,) * a.ndim)
    kern = functools.partial(_compress_sample_kernel, layer, pps, steps_per_seq)
    return pl.pallas_call(
        kern,
        grid_spec=pltpu.PrefetchScalarGridSpec(
            num_scalar_prefetch=1, grid=(db, steps_per_seq),
            in_specs=[pl.BlockSpec(memory_space=pl.ANY)] + [const(cw[k]) for k in _CW_NAMES],
            out_specs=pl.BlockSpec((1, n, CMP_LANES), lambda i, p, pt: (i, 0, 0)),
            scratch_shapes=[pltpu.VMEM((2, pps, CMP_LANES, page), F32), pltpu.SemaphoreType.DMA((2,)),
                            pltpu.VMEM((2, CMP_STRIDE, pps * page // CMP_STRIDE, KV_WIDTH), F32),
                            pltpu.VMEM((n, CMP_LANES), F32), pltpu.VMEM((n, CMP_LANES), F32)]),
        out_shape=jax.ShapeDtypeStruct((db, n, CMP_LANES), F32),
        compiler_params=_params("arbitrary", "arbitrary"), name="compress_sample",
    )(page_table, cache_t, *[cw[k] for k in _CW_NAMES])


def _cmp_to_sel(n_cmp_pad, n_cmp, n_sel_pad, n_sel):
    c0 = np.arange(n_cmp_pad)[:, None] * CMP_STRIDE
    s0 = np.arange(n_sel_pad)[None, :] * SEL_BLOCK
    hit = (c0 < s0 + SEL_BLOCK) & (c0 + CMP_BLOCK > s0)
    hit &= (np.arange(n_cmp_pad)[:, None] < n_cmp) & (np.arange(n_sel_pad)[None, :] < n_sel)
    return hit.astype(np.float32)


def _block_to_key(n_blk, n_key):
    return (np.arange(n_key)[None, :] // SEL_BLOCK == np.arange(n_blk)[:, None]).astype(np.float32)


def _nsa_prompt_kernel(qb, tk, q_ref, kc_ref, vc_ref, ks_ref, vs_ref, kw_ref, vw_ref, gate_ref, c2s_ref, hi_ref,
                       o_ref):
    i = pl.program_id(2)
    qs = pl.multiple_of(i * qb, qb)
    rows = HEADS_PER_KV * qb
    n_cmp = kc_ref.shape[0]
    n_sel = c2s_ref.shape[0]
    q_pad = q_ref[...].reshape(rows, 2 * HEAD_DIM)
    q = q_pad[:, :HEAD_DIM]
    qpos = qs + lax.broadcasted_iota(jnp.int32, (rows, 1), 0) % qb

    def biased(s, bias):
        return (s.reshape(HEADS_PER_KV, qb, s.shape[-1]) + bias[None]).reshape(s.shape)

    def weights(s, m):
        return jnp.exp2((s - m).astype(BF16))

    def normalised(acc):
        return acc[:, :HEAD_DIM] / jnp.maximum(acc[:, HEAD_DIM:HEAD_DIM + 1], 1e-30)

    s_cmp = _dot_nt(q, kc_ref[...])
    nw = WINDOW + qb
    w0 = pl.multiple_of(jnp.maximum(qs - WINDOW, 0), qb)
    s_win = _dot_nt(q, kw_ref[pl.ds(w0, nw), :])
    s_dia = _dot_nt(q_pad, ks_ref[pl.ds(qs, qb), :])

    cmp_end = lax.broadcasted_iota(jnp.int32, (1, n_cmp), 1) * CMP_STRIDE + (CMP_BLOCK - 1)
    p = _masked_softmax(s_cmp, cmp_end <= qpos)
    o_cmp = _dot(p.astype(BF16), vc_ref[...])
    psum = p[0:qb]
    for h in range(1, HEADS_PER_KV):
        psum = psum + p[h * qb:(h + 1) * qb]
    imp_t = _dot_nt(c2s_ref[...], psum, precision=HIGHEST)

    rel = (qs + lax.broadcasted_iota(jnp.int32, (qb, nw), 0)) - (w0 + lax.broadcasted_iota(jnp.int32, (qb, nw), 1))
    s = biased(s_win, jnp.where((rel >= 0) & (rel < WINDOW), 0.0, NEG))
    o_win = normalised(_dot(weights(s, jnp.max(s, axis=-1, keepdims=True)), vw_ref[pl.ds(w0, nw), :]))

    qrow = lax.broadcasted_iota(jnp.int32, (qb, qb), 0)
    kcol = lax.broadcasted_iota(jnp.int32, (qb, qb), 1)
    s = biased(s_dia, jnp.where(kcol <= qrow, 0.0, NEG))
    m = jnp.max(s, axis=-1, keepdims=True)
    acc = _dot(weights(s, m), vs_ref[pl.ds(qs, qb), :])

    blk = lax.broadcasted_iota(jnp.int32, (n_sel, qb), 0)
    qp = qs + lax.broadcasted_iota(jnp.int32, (n_sel, qb), 1)
    cur = qp // SEL_BLOCK
    forced = (blk == 0) | (blk == cur) | (blk == cur - 1)
    valid = blk <= cur
    score = jnp.where(valid, jnp.where(forced, FORCED_SCORE, imp_t), NEG)
    cnt = jnp.zeros((n_sel, qb), jnp.int32)
    for j in range(n_sel):
        sj = score[j:j + 1, :]
        ahead = (sj > score) | ((sj == score) & (j < blk))
        cnt = cnt + ahead.astype(jnp.int32)
    unsel_t = jnp.where((cnt < SEL_TOPN) & valid & (blk < qs // SEL_BLOCK), 0.0, 1.0).astype(BF16)
    q_sel = (q_pad.astype(F32) + jnp.concatenate([_dot_tn(unsel_t, hi_ref[...])] * HEADS_PER_KV, axis=0)).astype(BF16)

    def sel_scores(kt):
        return _dot_nt(q_sel, ks_ref[pl.ds(pl.multiple_of(kt * tk, tk), tk), :])

    def sel_update(kt, s, m, acc):
        m_new = jnp.maximum(m, jnp.max(s, axis=-1, keepdims=True))
        acc = jnp.exp2(m - m_new) * acc + _dot(weights(s, m_new), vs_ref[pl.ds(pl.multiple_of(kt * tk, tk), tk), :])
        return m_new, acc

    def sel_pair(kp, carry):
        s0, s1 = sel_scores(2 * kp), sel_scores(2 * kp + 1)
        return sel_update(2 * kp + 1, s1, *sel_update(2 * kp, s0, *carry))

    def sel_single(kt, carry):
        return sel_update(kt, sel_scores(kt), *carry)

    n_kt = (qs + tk - 1) // tk
    carry = lax.fori_loop(0, n_kt // 2, sel_pair, (m, acc))
    _, acc = lax.fori_loop(n_kt // 2 * 2, n_kt, sel_single, carry)
    o_sel = normalised(acc)

    g = gate_ref[...].reshape(rows, N_BRANCH)
    o = (g[:, 0:1] * o_cmp + g[:, 1:2] * o_sel + g[:, 2:3] * o_win).astype(o_ref.dtype)
    for h in range(HEADS_PER_KV):
        o_ref[:, h * HEAD_DIM:(h + 1) * HEAD_DIM] = o[h * qb:(h + 1) * qb]


def _nsa_prompt(qh, ksa, kw, va, cvh, gates_h, qb=128, tk=512):
    b, _, t, _ = qh.shape
    n_cmp = cvh.shape[3]
    n_sel = t // SEL_BLOCK
    assert qb % SEL_BLOCK == 0 and t % tk == 0 and WINDOW % qb == 0 and WINDOW + qb <= t and n_sel <= HEAD_DIM
    c2s = jnp.asarray(_cmp_to_sel(n_cmp, n_cmp - 1, n_sel, n_sel).T)
    hi = jnp.asarray(np.concatenate([np.zeros((n_sel, HEAD_DIM)), np.eye(n_sel, HEAD_DIM)], axis=1), dtype=BF16)
    vspec = lambda c: pl.BlockSpec((None, None, None, t, 2 * HEAD_DIM), lambda bi, g, i: (bi, c, g, 0, 0))
    ccomp = lambda c: pl.BlockSpec((None, None, None, n_cmp, HEAD_DIM), lambda bi, g, i: (bi, c, g, 0, 0))
    qspec = pl.BlockSpec((None, HEADS_PER_KV, qb, 2 * HEAD_DIM), lambda bi, g, i: (bi, g, i, 0))
    ospec = pl.BlockSpec((None, qb, HEADS_PER_KV * HEAD_DIM), lambda bi, g, i: (bi, i, g))
    kern = functools.partial(_nsa_prompt_kernel, qb, tk)
    return pl.pallas_call(
        kern, grid=(b, KV_HEADS, t // qb),
        in_specs=[qspec, ccomp(0), ccomp(1),
                  pl.BlockSpec((None, None, t, 2 * HEAD_DIM), lambda bi, g, i: (bi, g, 0, 0)), vspec(0),
                  pl.BlockSpec((None, None, t, HEAD_DIM), lambda bi, g, i: (bi, g, 0, 0)), vspec(1),
                  pl.BlockSpec((None, HEADS_PER_KV, qb, N_BRANCH), lambda bi, g, i: (bi, g, i, 0)),
                  pl.BlockSpec(c2s.shape, lambda bi, g, i: (0, 0)), pl.BlockSpec(hi.shape, lambda bi, g, i: (0, 0))],
        out_specs=ospec,
        out_shape=jax.ShapeDtypeStruct((b, t, NSA_WIDTH), BF16),
        compiler_params=_params("arbitrary", "arbitrary", "arbitrary"), name="nsa_prompt",
    )(qh, cvh, cvh, ksa, va, kw, va, gates_h, c2s, hi)


SROWS = 8 * KV_HEADS


def _sample_cmp_win_kernel(past_len, n_cmp, qz_ref, kvc_ref, win_ref, wnew_ref, c2s_ref, ocmp_ref, owin_ref,
                           imp_ref):
    qz = qz_ref[0]
    kvc = kvc_ref[0]
    n_pad = kvc.shape[0]
    kc, vc = kvc[:, :KV_WIDTH].astype(BF16), kvc[:, KV_WIDTH:].astype(BF16)
    s = _dot_nt(qz, kc)
    n = lax.broadcasted_iota(jnp.int32, (1, n_pad), 1)
    p = _masked_softmax(s, (n < n_cmp) & (n * CMP_STRIDE + (CMP_BLOCK - 1) <= past_len))
    ocmp_ref[0] = _dot(p.astype(BF16), vc)
    r = lax.broadcasted_iota(jnp.int32, (SROWS, 1), 0)
    p = jnp.where(r % 8 < HEADS_PER_KV, p, 0.0)
    psum = jnp.concatenate([jnp.sum(p[8 * g:8 * g + 8], axis=0, keepdims=True) for g in range(KV_HEADS)] +
                           [jnp.zeros((8 - KV_HEADS, n_pad), F32)], axis=0)
    imp_ref[0] = _dot(psum, c2s_ref[...], precision=HIGHEST)

    win = win_ref[0]
    wb = win.shape[1]
    kw, vw = win[:KV_WIDTH].astype(BF16), win[KV_WIDTH:].astype(BF16)
    wnew = wnew_ref[0]
    s = _dot(qz, kw)
    s_new = jnp.sum(qz.astype(F32) * wnew[:, :KV_WIDTH].astype(BF16).astype(F32), axis=-1, keepdims=True)
    rel = wb - lax.broadcasted_iota(jnp.int32, (1, wb), 1)
    msk = (rel >= 0) & (rel < WINDOW)
    s = jnp.where(msk, s, NEG)
    m = jnp.maximum(jnp.max(s, axis=-1, keepdims=True), s_new)
    e = jnp.where(msk, jnp.exp2(s - m), 0.0)
    e_new = jnp.exp2(s_new - m)
    den = jnp.maximum(jnp.sum(e, axis=-1, keepdims=True) + e_new, 1e-30)
    pv = _dot_nt((e / den).astype(BF16), vw)
    pn = (e_new / den).astype(BF16).astype(F32)
    owin_ref[0] = pv + pn * wnew[:, KV_WIDTH:].astype(BF16).astype(F32)


def _sample_cmp_win(past_len, n_cmp, qz, kvc, win, wnew, c2s):
    db = qz.shape[0]
    blk3 = lambda a: pl.BlockSpec((1,) + a.shape[1:], lambda i: (i, 0, 0))
    kern = functools.partial(_sample_cmp_win_kernel, past_len, n_cmp)
    outs = (jax.ShapeDtypeStruct((db, SROWS, KV_WIDTH), F32), jax.ShapeDtypeStruct((db, SROWS, KV_WIDTH), F32),
            jax.ShapeDtypeStruct((db, 8, c2s.shape[1]), F32))
    return pl.pallas_call(
        kern, grid=(db,),
        in_specs=[blk3(qz), blk3(kvc), blk3(win), blk3(wnew), pl.BlockSpec(c2s.shape, lambda i: (0, 0))],
        out_specs=[pl.BlockSpec((1, SROWS, KV_WIDTH), lambda i: (i, 0, 0)),
                   pl.BlockSpec((1, SROWS, KV_WIDTH), lambda i: (i, 0, 0)),
                   pl.BlockSpec((1, 8, c2s.shape[1]), lambda i: (i, 0, 0))],
        out_shape=outs, compiler_params=_params("arbitrary"), name="sample_cmp_win",
    )(qz, kvc, win, wnew, c2s)


def _select_kernel(n_sel, cur, imp_ref, sel_ref):
    imp = imp_ref[...]
    blk = lax.broadcasted_iota(jnp.int32, imp.shape, 1)
    forced = (blk == 0) | (blk == cur) | (blk == cur - 1)
    valid = blk <= cur
    score = jnp.where(valid, jnp.where(forced, FORCED_SCORE, imp), NEG)

    def body(j, cnt):
        sj = jnp.sum(jnp.where(blk == j, score, 0.0), axis=-1, keepdims=True)
        ahead = (sj > score) | ((sj == score) & (j < blk))
        return cnt + ahead.astype(jnp.int32)

    cnt = lax.fori_loop(0, n_sel, body, jnp.zeros(imp.shape, jnp.int32))
    sel_ref[...] = jnp.where((cnt < SEL_TOPN) & valid, 1.0, 0.0)


def _select(imp2, n_sel, cur):
    return pl.pallas_call(
        functools.partial(_select_kernel, n_sel, cur),
        out_shape=jax.ShapeDtypeStruct(imp2.shape, F32), name="sample_select",
        compiler_params=pltpu.CompilerParams(vmem_limit_bytes=VMEM_LIMIT),
    )(imp2)


def _sample_sel_kernel(layer, pages_per_step, steps_per_seq, pt_ref, cache_ref, qz_ref, sel_ref, selnew_ref,
                       knew_ref, b2k_ref, ocmp_ref, owin_ref, gate_ref, o_ref, buf_ref, sem_ref, m_ref, l_ref,
                       acc_ref):
    page_len = cache_ref.shape[3]

    def page_copies(step, j):
        page = _page_of(pt_ref, step, j, pages_per_step, steps_per_seq)
        return [pltpu.make_async_copy(cache_ref.at[layer, page, pl.ds(CMP_LANES, CMP_LANES), :],
                                      buf_ref.at[step % 2, :, pl.ds(pl.multiple_of(j * page_len, page_len), page_len)],
                                      sem_ref.at[step % 2])]

    slot = _paged_fetch(page_copies, pages_per_step, steps_per_seq)
    part = pl.program_id(1)
    qz = qz_ref[0]

    @pl.when(part == 0)
    def _():
        m_ref[...] = jnp.full(m_ref.shape, NEG, F32)
        l_ref[...] = jnp.zeros(l_ref.shape, F32)
        acc_ref[...] = jnp.zeros(acc_ref.shape, F32)

    kv = buf_ref[slot].astype(BF16)
    s = _dot(qz, kv[:KV_WIDTH])
    msk = _dot(sel_ref[0].astype(BF16), b2k_ref[...]) > 0.5
    s = jnp.where(msk, s, NEG)
    m_old = m_ref[...]
    m_new = jnp.maximum(m_old, jnp.max(s, axis=-1, keepdims=True))
    alpha = jnp.exp2(m_old - m_new)
    e = jnp.where(msk, jnp.exp2(s - m_new), 0.0)
    l_ref[...] = alpha * l_ref[...] + jnp.sum(e, axis=-1, keepdims=True)
    acc_ref[...] = alpha * acc_ref[...] + _dot_nt(e.astype(BF16), kv[KV_WIDTH:])
    m_ref[...] = m_new

    @pl.when(part == steps_per_seq - 1)
    def _():
        knew = knew_ref[0].astype(BF16).astype(F32)
        s_new = jnp.sum(qz.astype(F32) * knew[:, :KV_WIDTH], axis=-1, keepdims=True)
        on = selnew_ref[0][:, 0:1] > 0.5
        s_new = jnp.where(on, s_new, NEG)
        m_old = m_ref[...]
        m_new = jnp.maximum(m_old, s_new)
        alpha = jnp.exp2(m_old - m_new)
        e_new = jnp.where(on, jnp.exp2(s_new - m_new), 0.0)
        l = alpha * l_ref[...] + e_new
        acc = alpha * acc_ref[...] + e_new.astype(BF16).astype(F32) * knew[:, KV_WIDTH:]
        o_sel = acc / jnp.maximum(l, 1e-30)
        g = gate_ref[0]
        o_ref[0] = g[:, 0:1] * ocmp_ref[0] + g[:, 1:2] * o_sel + g[:, 2:3] * owin_ref[0]


def _sample_sel(cache_t, page_table, layer, qz, sel16, knew, ocmp, owin, gz, steps_per_seq=2):
    db, n_pages = page_table.shape
    page = cache_t.shape[3]
    pps = n_pages // steps_per_seq
    keys = pps * page
    blocks = keys // SEL_BLOCK
    b2k = jnp.asarray(_block_to_key(blocks, keys), dtype=BF16)
    blk3 = lambda a: pl.BlockSpec((1,) + a.shape[1:], lambda i, p, pt: (i, 0, 0))
    kern = functools.partial(_sample_sel_kernel, layer, pps, steps_per_seq)
    return pl.pallas_call(
        kern,
        grid_spec=pltpu.PrefetchScalarGridSpec(
            num_scalar_prefetch=1, grid=(db, steps_per_seq),
            in_specs=[pl.BlockSpec(memory_space=pl.ANY), blk3(qz),
                      pl.BlockSpec((1, SROWS, blocks), lambda i, p, pt: (i, 0, p)),
                      pl.BlockSpec((1, SROWS, LANES), lambda i, p, pt: (i, 0, steps_per_seq * blocks // LANES)),
                      blk3(knew),
                      pl.BlockSpec(b2k.shape, lambda i, p, pt: (0, 0)), blk3(ocmp), blk3(owin), blk3(gz)],
            out_specs=pl.BlockSpec((1, SROWS, KV_WIDTH), lambda i, p, pt: (i, 0, 0)),
            scratch_shapes=[pltpu.VMEM((2, CMP_LANES, keys), F32), pltpu.SemaphoreType.DMA((2,)),
                            pltpu.VMEM((SROWS, 1), F32), pltpu.VMEM((SROWS, 1), F32),
                            pltpu.VMEM((SROWS, KV_WIDTH), F32)]),
        out_shape=jax.ShapeDtypeStruct((db, SROWS, KV_WIDTH), F32),
        compiler_params=_params("arbitrary", "arbitrary"), name="sample_sel",
    )(page_table, cache_t, qz, sel16, sel16, knew, b2k, ocmp, owin, gz)


def _bdot(dot, a, b):
    return dot(a.astype(BF16), b.astype(BF16))


def _head_sum(x, hm):
    xh = x.astype(BF16)
    xl = (x - xh.astype(F32)).astype(BF16)
    return _dot(xh, hm) + _dot(xl, hm)


def _softplus(z):
    return jnp.maximum(z, 0.0) + jnp.log(1.0 + jnp.exp(-jnp.abs(z)))


CHUNK_GROUP = 4


def _rwkv_kernel(chunk, n_valid, pr_ref, prev_ref, s0_ref, mu_ref, w0_ref, wup_ref, a0_ref, aup_ref, gup_ref,
                 kk_ref, ka_ref, rk_ref, lnw_ref, lnb_ref, hm_ref, tri_ref, out_ref, sout_ref,
                 carry_ref, state_ref, r_ref, k_ref, v_ref, lw_ref, a_ref, b_ref, rh_ref, y_ref, m_ref, n_ref, et_ref):
    ti = pl.program_id(1)
    tt = pr_ref.shape[1]
    w_ = RWKV_WIDTH
    hd = RWKV_HEAD

    @pl.when(ti == 0)
    def _():
        carry_ref[...] = prev_ref[0]
        state_ref[...] = s0_ref[0]

    p = pr_ref[0]
    row = lax.broadcasted_iota(jnp.int32, (tt, 1), 0)
    shifted = jnp.where(row == 0, carry_ref[...], pltpu.roll(p, 1, 0))
    carry_ref[...] = p[tt - 1:tt]
    xs = p + (shifted - p) * mu_ref[...]
    r, k, v = xs[:, :w_], xs[:, w_:2 * w_], xs[:, 2 * w_:3 * w_]
    wa = xs[:, 3 * w_:3 * w_ + DECAY_LORA + AAA_LORA]
    gl = xs[:, 3 * w_ + DECAY_LORA + AAA_LORA:]
    hm = hm_ref[...]
    w = -_softplus(-(w0_ref[...] + _dot(jnp.tanh(wa).astype(BF16), wup_ref[...]))) - 0.5
    lw = -jnp.exp(w)
    a = jax.nn.sigmoid(a0_ref[...] + _dot(wa.astype(BF16), aup_ref[...]))
    g = _dot(jax.nn.sigmoid(gl).astype(BF16), gup_ref[...])
    kk = k * kk_ref[...]
    kk = kk / jnp.maximum(jnp.sqrt(_head_sum(kk * kk, hm)), 1e-12)
    k = k * (1.0 + (a - 1.0) * ka_ref[...])
    av, bv = -kk, kk * a
    if n_valid < tt:
        keep = row < n_valid
        lw, k, v, av, bv = (jnp.where(keep, t, 0.0) for t in (lw, k, v, av, bv))
    r_ref[...], k_ref[...], v_ref[...], lw_ref[...], a_ref[...], b_ref[...] = r, k, v, lw, av, bv

    ci = lax.broadcasted_iota(jnp.int32, (chunk, chunk), 0)
    cj = lax.broadcasted_iota(jnp.int32, (chunk, chunk), 1)
    strict, incl = ci > cj, ci >= cj
    dot = functools.partial(_bdot, _dot)
    dot_nt = functools.partial(_bdot, _dot_nt)
    dot_tn = functools.partial(_bdot, _dot_tn)
    group = min(CHUNK_GROUP, tt // chunk)

    def prepare(cg, carry):
        jobs, pre = [], []
        for ci in range(group):
            c = cg * group + ci
            rows = pl.ds(pl.multiple_of(c * chunk, chunk), chunk)
            lwc = lw_ref[rows, :]
            cs = _dot(tri_ref[...], lwc, precision=HIGHEST)
            e_in, e_out = jnp.exp(cs), jnp.exp(-cs)
            total = cs[chunk - 1:chunk]
            e_end = jnp.exp(total - cs)
            kc, bc = k_ref[rows, :], b_ref[rows, :]
            pre.append(dict(c=c, rows=rows, rt=r_ref[rows, :] * e_in, at=a_ref[rows, :] * jnp.exp(cs - lwc),
                            kt=kc * e_out, bt=bc * e_out, kh=kc * e_end, bh=bc * e_end, vc=v_ref[rows, :]))
            et_ref[c] = jnp.broadcast_to(jnp.exp(total), et_ref.shape[1:])
            jobs += [(ci, h, slice(h * hd, (h + 1) * hd)) for h in range(RWKV_HEADS)]
        nj = range(len(jobs))
        part = lambda name: [pre[ci][name][:, sl] for ci, _, sl in jobs]
        at, rt, bt, kt, bh, kh, vh = (part(n_) for n_ in ("at", "rt", "bt", "kt", "bh", "kh", "vc"))
        gm = [dot_nt(jnp.concatenate([at[j], rt[j]], axis=0), jnp.concatenate([bt[j], kt[j]], axis=0))
              for j in nj]
        lab = [jnp.where(strict, g_[:chunk, :chunk], 0.0) for g_ in gm]
        lak = [jnp.where(strict, g_[:chunk, chunk:], 0.0) for g_ in gm]
        mrb = [jnp.where(incl, g_[chunk:, :chunk], 0.0) for g_ in gm]
        mrk = [jnp.where(incl, g_[chunk:, chunk:], 0.0) for g_ in gm]
        wm = [jnp.concatenate([at[j], dot(lak[j], vh[j])], axis=1) for j in nj]
        x, n = lab, 1
        while 2 * n < chunk:
            xw = [dot(x[j], jnp.concatenate([wm[j], x[j]], axis=1)) for j in nj]
            x = [t[:, 2 * hd:] for t in xw]
            wm = [wm[j] + xw[j][:, :2 * hd] for j in nj]
            n *= 2
        wm = [wm[j] + dot(x[j], wm[j]) for j in nj]
        mw = [dot(mrb[j], wm[j]) for j in nj]
        y0 = [mw[j][:, hd:] + dot(mrk[j], vh[j]) for j in nj]
        wb = [dot_tn(wm[j], bh[j]) for j in nj]
        vk = [dot_tn(vh[j], kh[j]) for j in nj]
        for j, (ci, h, sl) in enumerate(jobs):
            c, rows = pre[ci]["c"], pre[ci]["rows"]
            rh_ref[rows, sl] = rt[j] + mw[j][:, :hd]
            y_ref[rows, sl] = y0[j]
            m_ref[c, h] = wb[j][:hd]
            n_ref[c, h] = wb[j][hd:] + vk[j]
        return carry

    def advance(c, carry):
        rows = pl.ds(pl.multiple_of(c * chunk, chunk), chunk)
        e_tot = et_ref[c]
        sls = [slice(h * hd, (h + 1) * hd) for h in range(RWKV_HEADS)]
        s0 = [state_ref[h] for h in range(RWKV_HEADS)]
        ys = [dot_nt(rh_ref[rows, sl], s0[h]) for h, sl in enumerate(sls)]
        sm = [dot(s0[h], m_ref[c, h]) for h in range(RWKV_HEADS)]
        for h, sl in enumerate(sls):
            y_ref[rows, sl] += ys[h]
            state_ref[h] = s0[h] * e_tot[0:1, sl] + sm[h] + n_ref[c, h]
        return carry

    lax.fori_loop(0, tt // chunk // group, prepare, 0)
    lax.fori_loop(0, tt // chunk, advance, 0)

    y = y_ref[...]
    inv_n = 1.0 / hd
    mean = _head_sum(y, hm) * inv_n
    d = y - mean
    var = _head_sum(d * d, hm) * inv_n
    y = d * lax.rsqrt(var + LNX_EPS) * lnw_ref[...] + lnb_ref[...]
    y = y + _head_sum(r_ref[...] * k_ref[...] * rk_ref[...], hm) * v_ref[...]
    out_ref[0] = (y * g).astype(out_ref.dtype)
    sout_ref[0] = state_ref[...]


def _rwkv(pr3, prev, s0, rw, chunk, tt, n_valid):
    b, t, _ = pr3.shape
    hm = jnp.asarray(np.kron(np.eye(RWKV_HEADS), np.ones((RWKV_HEAD, RWKV_HEAD))), dtype=BF16)
    tri = jnp.asarray(np.tril(np.ones((chunk, chunk))), dtype=F32)
    vec = lambda a: pl.BlockSpec(a.shape, lambda i, j: (0, 0))
    prev3 = prev[:, None, :]
    names = ("mu", "w0", "wup", "a0", "aup", "gup", "kk", "ka", "rk", "lnw", "lnb")
    kern = functools.partial(_rwkv_kernel, chunk, n_valid)
    per_chunk = (tt // chunk, RWKV_HEADS, RWKV_HEAD, RWKV_HEAD)
    return pl.pallas_call(
        kern, grid=(b, t // tt),
        in_specs=[pl.BlockSpec((1, tt, RWKV_PROJ), lambda i, j: (i, j, 0)),
                  pl.BlockSpec((1, 1, RWKV_PROJ), lambda i, j: (i, 0, 0)),
                  pl.BlockSpec((1, RWKV_HEADS, RWKV_HEAD, RWKV_HEAD), lambda i, j: (i, 0, 0, 0))] +
                 [vec(rw[n]) for n in names] + [vec(hm), vec(tri)],
        out_specs=[pl.BlockSpec((1, tt, RWKV_WIDTH), lambda i, j: (i, j, 0)),
                   pl.BlockSpec((1, RWKV_HEADS, RWKV_HEAD, RWKV_HEAD), lambda i, j: (i, 0, 0, 0))],
        out_shape=(jax.ShapeDtypeStruct((b, t, RWKV_WIDTH), BF16),
                   jax.ShapeDtypeStruct((b, RWKV_HEADS, RWKV_HEAD, RWKV_HEAD), F32)),
        scratch_shapes=[pltpu.VMEM((1, RWKV_PROJ), F32), pltpu.VMEM((RWKV_HEADS, RWKV_HEAD, RWKV_HEAD), F32)] +
                       [pltpu.VMEM((tt, RWKV_WIDTH), F32) for _ in range(8)] +
                       [pltpu.VMEM(per_chunk, F32), pltpu.VMEM(per_chunk, F32),
                        pltpu.VMEM((tt // chunk, 8, RWKV_WIDTH), F32)],
        compiler_params=_params("arbitrary", "arbitrary"), name="rwkv",
    )(pr3, prev3, s0, *[rw[n] for n in names], hm, tri)


def _ffn_head(x_ref, nsa_ref, rwkv_ref, wo_ref, gpost_ref, gpre_ref, x1_ref, h2_ref, acc_ref):
    n = nsa_ref.shape[1]
    mixed = _dot(nsa_ref[...], wo_ref[:n, :]) + _dot(rwkv_ref[...], wo_ref[n:, :])
    x1 = x_ref[...] + _rms(mixed, gpost_ref[...])
    x1_ref[...] = x1
    h2_ref[...] = _rms(x1, gpre_ref[...]).astype(BF16)
    acc_ref[...] = jnp.zeros(acc_ref.shape, F32)


def _ffn_tail(gate2, gate1, gate, val, cw_ref, cb_ref, wd_ref, acc_ref):
    conv = cb_ref[...] + (gate2 * cw_ref[0:1, :] + gate1 * cw_ref[1:2, :] + gate * cw_ref[2:3, :])
    acc_ref[...] += _dot((jax.nn.gelu(conv) * val).astype(BF16), wd_ref[...])


FFN_COLS = 256


def _ffn_seq_kernel(tiles_per_seq, x_ref, nsa_ref, rwkv_ref, wo_ref, gpost_ref, gpre_ref, gfpost_ref, wg_ref, wv_ref, cw_ref,
                    cb_ref, wd_ref, out_ref, tail_ref, x1_ref, h2_ref, acc_ref, carry_ref):
    i, j = pl.program_id(0), pl.program_id(1)
    tm = x_ref.shape[0]

    @pl.when(j == 0)
    def _():
        _ffn_head(x_ref, nsa_ref, rwkv_ref, wo_ref, gpost_ref, gpre_ref, x1_ref, h2_ref, acc_ref)

    @pl.when(i % tiles_per_seq == 0)
    def _():
        carry_ref[j] = jnp.zeros(carry_ref.shape[1:], F32)

    h2 = h2_ref[...]
    row = lax.broadcasted_iota(jnp.int32, (tm, 1), 0)
    tf = wg_ref.shape[1]
    cols = [slice(c0, min(c0 + FFN_COLS, tf)) for c0 in range(0, tf, FFN_COLS)]
    up = lambda c: (_dot(h2, wg_ref[:, c]), _dot(h2, wv_ref[:, c]))
    nxt = up(cols[0])
    for k, c in enumerate(cols):
        gate, val = nxt
        if k + 1 < len(cols):
            nxt = up(cols[k + 1])
        prev = carry_ref[j, :, c]
        gate1 = jnp.where(row == 0, prev[7:8], pltpu.roll(gate, 1, 0))
        gate2 = jnp.where(row == 0, prev[6:7], jnp.where(row == 1, prev[7:8], pltpu.roll(gate, 2, 0)))
        carry_ref[j, :, c] = gate[tm - 8:tm]
        tail_ref[0, :, c] = gate[tm - 8:tm]
        conv = cb_ref[:, c] + (gate2 * cw_ref[0:1, c] + gate1 * cw_ref[1:2, c] + gate * cw_ref[2:3, c])
        acc_ref[...] += _dot((jax.nn.gelu(conv) * val).astype(BF16), wd_ref[c, :])

    @pl.when(j == pl.num_programs(1) - 1)
    def _():
        out_ref[...] = x1_ref[...] + _rms(acc_ref[...], gfpost_ref[...])


def _ffn_step_kernel(x_ref, nsa_ref, rwkv_ref, wo_ref, gpost_ref, gpre_ref, gfpost_ref, wg_ref, wv_ref, cw_ref, cb_ref,
                     wd_ref, prev_ref, out_ref, gate_ref, x1_ref, h2_ref, acc_ref):
    j = pl.program_id(0)

    @pl.when(j == 0)
    def _():
        _ffn_head(x_ref, nsa_ref, rwkv_ref, wo_ref, gpost_ref, gpre_ref, x1_ref, h2_ref, acc_ref)

    h2 = h2_ref[...]
    gate, val = _dot(h2, wg_ref[...]), _dot(h2, wv_ref[...])
    gate_ref[...] = gate
    _ffn_tail(prev_ref[0], prev_ref[1], gate, val, cw_ref, cb_ref, wd_ref, acc_ref)

    @pl.when(j == pl.num_programs(0) - 1)
    def _():
        out_ref[...] = x1_ref[...] + _rms(acc_ref[...], gfpost_ref[...])


def _ffn_seq(x2, nsa2, rwkv2, fw, seq_len, tm, tf):
    m, d = x2.shape
    d_ff = fw["wd"].shape[0]
    nf = d_ff // tf
    row = lambda i, j: (i, 0)
    full = lambda i, j: (0, 0)
    tiles_per_seq = seq_len // tm
    scratch = [pltpu.VMEM((tm, d), F32), pltpu.VMEM((tm, d), BF16), pltpu.VMEM((tm, d), F32),
               pltpu.VMEM((nf, 8, tf), F32)]
    return pl.pallas_call(
        functools.partial(_ffn_seq_kernel, tiles_per_seq), grid=(m // tm, nf),
        in_specs=[pl.BlockSpec((tm, d), row), pl.BlockSpec((tm, nsa2.shape[1]), row),
                  pl.BlockSpec((tm, rwkv2.shape[1]), row), pl.BlockSpec((d, d), full),
                  pl.BlockSpec((1, d), full), pl.BlockSpec((1, d), full), pl.BlockSpec((1, d), full),
                  pl.BlockSpec((d, tf), lambda i, j: (0, j)), pl.BlockSpec((d, tf), lambda i, j: (0, nf + j)),
                  pl.BlockSpec((CONV_WIDTH, tf), lambda i, j: (0, j)), pl.BlockSpec((1, tf), lambda i, j: (0, j)),
                  pl.BlockSpec((tf, d), lambda i, j: (j, 0))],
        out_specs=[pl.BlockSpec((tm, d), row), pl.BlockSpec((1, 8, tf), lambda i, j: (i, 0, j))],
        out_shape=(jax.ShapeDtypeStruct((m, d), F32), jax.ShapeDtypeStruct((m // tm, 8, d_ff), F32)),
        scratch_shapes=scratch, compiler_params=_params("arbitrary", "arbitrary"), name="ffn_seq",
    )(x2, nsa2, rwkv2, fw["wo"], fw["gpost"], fw["gpre"], fw["gfpost"], fw["wup"], fw["wup"], fw["cw"], fw["cb"],
      fw["wd"])


def _ffn_step(x2, nsa2, rwkv2, fw, prev, tf):
    m, d = x2.shape
    d_ff = fw["wd"].shape[0]
    nf = d_ff // tf
    full = lambda j: (0, 0)
    scratch = [pltpu.VMEM((m, d), F32), pltpu.VMEM((m, d), BF16), pltpu.VMEM((m, d), F32)]
    return pl.pallas_call(
        _ffn_step_kernel, grid=(nf,),
        in_specs=[pl.BlockSpec((m, d), full), pl.BlockSpec(nsa2.shape, full), pl.BlockSpec(rwkv2.shape, full),
                  pl.BlockSpec((d, d), full),
                  pl.BlockSpec((1, d), full), pl.BlockSpec((1, d), full), pl.BlockSpec((1, d), full),
                  pl.BlockSpec((d, tf), lambda j: (0, j)), pl.BlockSpec((d, tf), lambda j: (0, nf + j)),
                  pl.BlockSpec((CONV_WIDTH, tf), lambda j: (0, j)), pl.BlockSpec((1, tf), lambda j: (0, j)),
                  pl.BlockSpec((tf, d), lambda j: (j, 0)),
                  pl.BlockSpec((CONV_WIDTH - 1, m, tf), lambda j: (0, 0, j))],
        out_specs=[pl.BlockSpec((m, d), full), pl.BlockSpec((m, tf), lambda j: (0, j))],
        out_shape=(jax.ShapeDtypeStruct((m, d), F32), jax.ShapeDtypeStruct((m, d_ff), F32)),
        scratch_shapes=scratch, compiler_params=_params("arbitrary"), name="ffn_step",
    )(x2, nsa2, rwkv2, fw["wo"], fw["gpost"], fw["gpre"], fw["gfpost"], fw["wup"], fw["wup"], fw["cw"], fw["cb"],
      fw["wd"], prev)


def _layer_weights(l, p):
    w_in = p["w_in"][l]
    o = NSA_WIDTH + 6 * KV_WIDTH
    wg = jnp.pad(w_in[:, o:NSA_PROJ], ((0, 0), (0, LANES - N_BRANCH * NSA_HEADS)))
    row = lambda a: a[l][None, :].astype(F32)
    zpad = jnp.zeros((AAA_LORA, RWKV_WIDTH), F32)
    rw = {"mu": row(p["rwkv_mu"]), "w0": row(p["rwkv_w0"]),
          "wup": jnp.concatenate([p["rwkv_w_up"][l], zpad], axis=0).astype(BF16),
          "a0": row(p["rwkv_a0"]), "aup": jnp.concatenate([zpad, p["rwkv_a_up"][l]], axis=0).astype(BF16),
          "gup": p["rwkv_g_up"][l].astype(BF16), "kk": row(p["rwkv_k_k"]), "ka": row(p["rwkv_k_a"]),
          "rk": p["rwkv_r_k"][l].reshape(1, RWKV_WIDTH), "lnw": row(p["rwkv_lnx_w"]), "lnb": row(p["rwkv_lnx_b"])}
    fw = {"wo": p["w_out"][l].astype(BF16), "gpost": row(p["norm_mix_post"]), "gpre": row(p["norm_ffn_pre"]),
          "gfpost": row(p["norm_ffn_post"]), "wup": p["ffn_w_up"][l].astype(BF16), "cw": p["ffn_conv_w"][l],
          "cb": row(p["ffn_conv_b"]), "wd": p["ffn_w_down"][l].astype(BF16)}
    return {"g_in": row(p["norm_mix_pre"]), "wq": w_in[:, :NSA_WIDTH].astype(BF16),
            "wkv": w_in[:, NSA_WIDTH:o].astype(BF16), "wg": wg.astype(BF16), "wr": w_in[:, NSA_PROJ:].astype(BF16),
            "cw": _compress_weights(p["cmp_pe"][l], p["cmp_w1"][l], p["cmp_w2"][l]), "rw": rw, "fw": fw}


def _prompt_layer(x, lw, tabs):
    b, t, d = x.shape
    m = b * t
    x2 = x.reshape(m, d)
    own_block = jnp.asarray(np.concatenate([np.zeros((t, HEAD_DIM)), NEG * _block_to_key(HEAD_DIM, t).T], axis=1),
                            dtype=BF16)
    kvw, gates, pr, qh, ksa, kw, va = _norm_proj(x2, lw["g_in"], lw["wq"], lw["wkv"], lw["wg"], lw["wr"], tabs,
                                                 tm=256, heads_of=(b, t, own_block))
    kvw3 = kvw.reshape(b, t, 6 * KV_WIDTH)
    kvc = _compress_prompt(kvw3, lw["cw"])
    cvh = kvc.astype(BF16).reshape(b, -1, 2, KV_HEADS, HEAD_DIM).transpose(0, 2, 3, 1, 4)
    gates_h = gates[:, :N_BRANCH * NSA_HEADS].reshape(b, t, NSA_HEADS, N_BRANCH).transpose(0, 2, 1, 3)
    o_nsa = _nsa_prompt(qh, ksa, kw, va, cvh, gates_h).reshape(m, NSA_WIDTH)
    s0 = jnp.zeros((b, RWKV_HEADS, RWKV_HEAD, RWKV_HEAD), F32)
    pr3 = pr.reshape(b, t, RWKV_PROJ)
    o_rwkv, wkv = _rwkv(pr3, jnp.zeros((b, RWKV_PROJ), F32), s0, lw["rw"], chunk=64, tt=min(t, 512), n_valid=t)
    d_ff = lw["fw"]["wd"].shape[0]
    tm = min(t, 512)
    y, tail = _ffn_seq(x2, o_nsa, o_rwkv.reshape(m, RWKV_WIDTH), lw["fw"], t, tm=tm, tf=d_ff // 2)
    tail = tail[t // tm - 1::t // tm]
    kv_rows = kvw3[:, :, :4 * KV_WIDTH].reshape(b, t, 4, KV_HEADS, HEAD_DIM)
    wn = min(WINDOW, t)
    win = kvw3[:, t - wn:, 4 * KV_WIDTH:].reshape(b, wn, 2, KV_HEADS, HEAD_DIM)
    return y.reshape(b, t, d), kv_rows, win, wkv, pr3[:, -1], tail[:, 8 - (CONV_WIDTH - 1):]


def _sample_layer(x, lw, tabs, l, cache_t, page_table, st_win, st_wkv, st_shift, st_conv):
    db, tn, d = x.shape
    past_len = page_table.shape[1] * cache_t.shape[3]
    x2 = x.reshape(db, d)
    q, kvw, gates, pr = _norm_proj(x2, lw["g_in"], lw["wq"], lw["wkv"], lw["wg"], lw["wr"], tabs, tm=db)
    qs = (q * (ATTN_SCALE * LOG2E)).astype(BF16).reshape(db, KV_HEADS, HEADS_PER_KV, HEAD_DIM)
    eye = jnp.eye(KV_HEADS, dtype=BF16)
    qz = jnp.einsum('bghd,gj->bghjd', qs, eye).reshape(db, KV_HEADS, HEADS_PER_KV, KV_WIDTH)
    qz = jnp.pad(qz, ((0, 0), (0, 0), (0, 8 - HEADS_PER_KV), (0, 0))).reshape(db, SROWS, KV_WIDTH)
    kvc = _compress_sample(cache_t, page_table, l, lw["cw"])
    n_cmp = past_len // CMP_STRIDE - 1
    n_sel = -(-(past_len + tn) // SEL_BLOCK)
    n_sel_pad = -(-n_sel // LANES) * LANES
    c2s = jnp.asarray(_cmp_to_sel(kvc.shape[1], n_cmp, n_sel_pad, n_sel))
    wb = st_win.shape[1]
    win = st_win.transpose(0, 2, 3, 4, 1).reshape(db, 2 * KV_WIDTH, wb)
    wnew = kvw[:, None, 4 * KV_WIDTH:]
    ocmp, owin, imp = _sample_cmp_win(past_len, n_cmp, qz, kvc, win, wnew, c2s)
    sel = _select(imp.reshape(db * 8, n_sel_pad), n_sel, past_len // SEL_BLOCK).reshape(db, 8, n_sel_pad)
    sel16 = jnp.broadcast_to(sel[:, :KV_HEADS, None, :], (db, KV_HEADS, 8, n_sel_pad)).reshape(db, SROWS, n_sel_pad)
    gz = gates[:, :N_BRANCH * NSA_HEADS].reshape(db, KV_HEADS, HEADS_PER_KV, N_BRANCH)
    gz = jnp.pad(gz, ((0, 0), (0, 0), (0, 8 - HEADS_PER_KV), (0, 0))).reshape(db, SROWS, N_BRANCH)
    knew = kvw[:, None, 2 * KV_WIDTH:4 * KV_WIDTH]
    o16 = _sample_sel(cache_t, page_table, l, qz, sel16, knew, ocmp, owin, gz)
    o4 = o16.reshape(db, KV_HEADS, 8, KV_HEADS, HEAD_DIM)[:, :, :HEADS_PER_KV]
    o_nsa = jnp.stack([o4[:, g, :, g] for g in range(KV_HEADS)], axis=1).reshape(db, NSA_WIDTH)
    pad_t = 8
    pr3 = jnp.pad(pr[:, None, :], ((0, 0), (0, pad_t - tn), (0, 0)))
    o_rwkv, wkv = _rwkv(pr3, st_shift, st_wkv, lw["rw"], chunk=pad_t, tt=pad_t, n_valid=tn)
    prev = st_conv.transpose(1, 0, 2)
    d_ff = lw["fw"]["wd"].shape[0]
    y, gate = _ffn_step(x2, o_nsa.astype(BF16), o_rwkv[:, 0], lw["fw"], prev, tf=d_ff // 2)
    kv_rows = kvw[:, :4 * KV_WIDTH].reshape(db, tn, 4, KV_HEADS, HEAD_DIM)
    win_new = jnp.concatenate([st_win[:, tn:], kvw[:, 4 * KV_WIDTH:].reshape(db, tn, 2, KV_HEADS, HEAD_DIM)], axis=1)
    conv_new = jnp.concatenate([st_conv[:, 1:], gate[:, None, :]], axis=1)
    return y.reshape(db, tn, d), kv_rows, win_new, wkv, pr, conv_new


def kernel(x_prompt, x_sample, cache_kv, page_table, state_win, state_wkv, state_shift, state_conv, norm_mix_pre, norm_mix_post, norm_ffn_pre, norm_ffn_post, w_in, w_out, cmp_pe, cmp_w1, cmp_w2, rwkv_mu, rwkv_w0, rwkv_w_up, rwkv_a0, rwkv_a_up, rwkv_g_up, rwkv_k_k, rwkv_k_a, rwkv_r_k, rwkv_lnx_w, rwkv_lnx_b, ffn_w_up, ffn_conv_w, ffn_conv_b, ffn_w_down):
    p = dict(norm_mix_pre=norm_mix_pre, norm_mix_post=norm_mix_post, norm_ffn_pre=norm_ffn_pre,
             norm_ffn_post=norm_ffn_post, w_in=w_in, w_out=w_out, cmp_pe=cmp_pe, cmp_w1=cmp_w1, cmp_w2=cmp_w2,
             rwkv_mu=rwkv_mu, rwkv_w0=rwkv_w0, rwkv_w_up=rwkv_w_up, rwkv_a0=rwkv_a0, rwkv_a_up=rwkv_a_up,
             rwkv_g_up=rwkv_g_up, rwkv_k_k=rwkv_k_k, rwkv_k_a=rwkv_k_a, rwkv_r_k=rwkv_r_k,
             rwkv_lnx_w=rwkv_lnx_w, rwkv_lnx_b=rwkv_lnx_b, ffn_w_up=ffn_w_up, ffn_conv_w=ffn_conv_w,
             ffn_conv_b=ffn_conv_b, ffn_w_down=ffn_w_down)
    depth = w_in.shape[0]
    b, t, _ = x_prompt.shape
    db, tn, _ = x_sample.shape
    n_pool, page = cache_kv.shape[1], cache_kv.shape[2]
    past_len = page_table.shape[1] * page
    cache_t = cache_kv.transpose(0, 1, 3, 4, 5, 2).reshape(depth, n_pool, 2 * CMP_LANES, page)
    tabs_p = _rope_tables(jnp.arange(t, dtype=jnp.int32))
    tabs_s = _rope_tables(jnp.full((db,), past_len, jnp.int32))
    xp, xs = x_prompt, x_sample
    outs_p, outs_s = [], []
    for l in range(depth):
        lw = _layer_weights(l, p)
        xp, *st_p = _prompt_layer(xp, lw, tabs_p)
        xs, *st_s = _sample_layer(xs, lw, tabs_s, l, cache_t, page_table, state_win[l], state_wkv[l],
                                  state_shift[l], state_conv[l])
        outs_p.append(st_p)
        outs_s.append(st_s)
    stack = lambda outs, i: jnp.stack([o[i] for o in outs])
    return (xp, xs, stack(outs_p, 0), stack(outs_s, 0), stack(outs_p, 1), stack(outs_s, 1),
            stack(outs_p, 2), stack(outs_s, 2), stack(outs_p, 3), stack(outs_s, 3),
            stack(outs_p, 4), stack(outs_s, 4))
```

```python
import functools

import jax
import jax.numpy as jnp
import numpy as np
from jax import lax
from jax.experimental import pallas as pl
from jax.experimental.pallas import tpu as pltpu

F32, BF16 = jnp.float32, jnp.bfloat16
HIGHEST = lax.Precision.HIGHEST

LANES = 128
VMEM_LIMIT = 56 * 1024 * 1024

HEAD_DIM = 64
NSA_HEADS = 8
KV_HEADS = 2
HEADS_PER_KV = NSA_HEADS // KV_HEADS
NSA_WIDTH = NSA_HEADS * HEAD_DIM
KV_WIDTH = KV_HEADS * HEAD_DIM
N_BRANCH = 3
CMP_STRIDE = 16
CMP_BLOCK = 2 * CMP_STRIDE
SEL_BLOCK = 64
SEL_TOPN = 16
WINDOW = 512
ROPE_THETA = 500000.0
ROPE_DIM = HEAD_DIM // 4
RWKV_HEAD = 64
RWKV_HEADS = 8
RWKV_WIDTH = RWKV_HEADS * RWKV_HEAD
DECAY_LORA = 64
AAA_LORA = 64
GATE_LORA = 128
RWKV_PROJ = 3 * RWKV_WIDTH + DECAY_LORA + AAA_LORA + GATE_LORA
NSA_PROJ = NSA_WIDTH + 6 * KV_WIDTH + N_BRANCH * NSA_HEADS
CONV_WIDTH = 3
NORM_EPS = 1e-6
LNX_EPS = 64e-5
NEG = -1e30
FORCED_SCORE = 1e9
ATTN_SCALE = HEAD_DIM ** -0.5
LOG2E = 1.4426950408889634


def _dot(a, b, precision=None):
    return jnp.dot(a, b, preferred_element_type=F32, precision=precision)


def _dot_nt(a, b, precision=None):
    return lax.dot_general(a, b, (((1,), (1,)), ((), ())), preferred_element_type=F32, precision=precision)


def _dot_tn(a, b, precision=None):
    return lax.dot_general(a, b, (((0,), (0,)), ((), ())), preferred_element_type=F32, precision=precision)


def _rms(x, g):
    return x * lax.rsqrt(jnp.mean(x * x, axis=-1, keepdims=True) + NORM_EPS) * g


def _masked_softmax(s, mask):
    s = jnp.where(mask, s, NEG)
    m = jnp.max(s, axis=-1, keepdims=True)
    e = jnp.where(mask, jnp.exp2(s - m), 0.0)
    return e / jnp.maximum(jnp.sum(e, axis=-1, keepdims=True), 1e-30)


def _params(*sem):
    return pltpu.CompilerParams(dimension_semantics=sem, vmem_limit_bytes=VMEM_LIMIT)


def _rope(v, cos, sa, sb):
    return v * cos + pltpu.roll(v, LANES - ROPE_DIM // 2, 1) * sa + pltpu.roll(v, ROPE_DIM // 2, 1) * sb


def _project(x_ref, g_ref, wq_ref, wkv_ref, wg_ref, wr_ref, tabs, kvw_ref, gate_ref, pr_ref):
    h = _rms(x_ref[...], g_ref[...]).astype(BF16)
    q = _dot(h, wq_ref[...])
    q = jnp.concatenate([_rope(q[:, j * LANES:(j + 1) * LANES], *tabs) for j in range(NSA_WIDTH // LANES)], axis=1)
    kv = _dot(h, wkv_ref[...])
    blocks = []
    for j in range(6 * KV_WIDTH // LANES):
        blk = kv[:, j * LANES:(j + 1) * LANES]
        blocks.append(_rope(blk, *tabs) if j % 2 == 0 else blk)
        kvw_ref[:, j * LANES:(j + 1) * LANES] = blocks[j]
    gate_ref[...] = jax.nn.sigmoid(_dot(h, wg_ref[...]))
    pr_ref[...] = _dot(h, wr_ref[...])
    return q, blocks


def _norm_proj_kernel(x_ref, g_ref, wq_ref, wkv_ref, wg_ref, wr_ref, cos_ref, sa_ref, sb_ref,
                      q_ref, kvw_ref, gate_ref, pr_ref):
    tabs = (cos_ref[...], sa_ref[...], sb_ref[...])
    q_ref[...], _ = _project(x_ref, g_ref, wq_ref, wkv_ref, wg_ref, wr_ref, tabs, kvw_ref, gate_ref, pr_ref)


def _norm_proj_heads_kernel(x_ref, g_ref, wq_ref, wkv_ref, wg_ref, wr_ref, cos_ref, sa_ref, sb_ref, own_ref,
                            kvw_ref, gate_ref, pr_ref, qh_ref, ksa_ref, kwh_ref, va_ref):
    tabs = (cos_ref[...], sa_ref[...], sb_ref[...])
    q, kv = _project(x_ref, g_ref, wq_ref, wkv_ref, wg_ref, wr_ref, tabs, kvw_ref, gate_ref, pr_ref)
    tm = q.shape[0]
    lane = lax.broadcasted_iota(jnp.int32, (tm, LANES), 1)
    low = lane < HEAD_DIM
    ones_lane = jnp.where(lane == HEAD_DIM, 1.0, 0.0)
    halves = lambda blk: (blk, pltpu.roll(blk, HEAD_DIM, 1))
    for j in range(NSA_WIDTH // LANES):
        for half, x in enumerate(halves(q[:, j * LANES:(j + 1) * LANES] * (ATTN_SCALE * LOG2E))):
            qh_ref[2 * j + half] = jnp.where(low, x, 0.0).astype(BF16)
    own = own_ref[...].astype(F32)
    for g, x in enumerate(halves(kv[2])):
        ksa_ref[g] = jnp.where(low, x, own).astype(BF16)
    for g, x in enumerate(halves(kv[4])):
        kwh_ref[g] = x[:, :HEAD_DIM].astype(BF16)
    for c, col in enumerate((3, 5)):
        for g, x in enumerate(halves(kv[col])):
            va_ref[c, g] = jnp.where(low, x, ones_lane).astype(BF16)


def _norm_proj(x2, g, wq, wkv, wg, wr, tabs, tm, heads_of=None):
    m, d = x2.shape
    cos, sa, sb = tabs
    tab_blocks = cos.shape[0] // tm
    row = lambda i: (i, 0)
    full = lambda i: (0, 0)
    tab = lambda i: (i % tab_blocks, 0)
    in_specs = [pl.BlockSpec((tm, d), row), pl.BlockSpec((1, d), full),
                pl.BlockSpec(wq.shape, full), pl.BlockSpec(wkv.shape, full),
                pl.BlockSpec(wg.shape, full), pl.BlockSpec(wr.shape, full),
                pl.BlockSpec((tm, LANES), tab), pl.BlockSpec((tm, LANES), tab), pl.BlockSpec((tm, LANES), tab)]
    rows_out = [(6 * KV_WIDTH, F32), (LANES, F32), (RWKV_PROJ, F32)]
    if heads_of is None:
        rows_out = [(NSA_WIDTH, F32)] + rows_out
        return pl.pallas_call(
            _norm_proj_kernel, grid=(m // tm,), in_specs=in_specs,
            out_specs=[pl.BlockSpec((tm, w), row) for w, _ in rows_out],
            out_shape=[jax.ShapeDtypeStruct((m, w), dt) for w, dt in rows_out],
            compiler_params=_params("arbitrary"), name="norm_proj",
        )(x2, g, wq, wkv, wg, wr, cos, sa, sb)
    b, t, own = heads_of
    head_shapes = [(NSA_HEADS, LANES), (KV_HEADS, LANES), (KV_HEADS, HEAD_DIM), (2, KV_HEADS, LANES)]
    hspec = lambda lead, w: pl.BlockSpec((None,) + lead + (tm, w),
                                         lambda i: (i // tab_blocks,) + (0,) * len(lead) + (i % tab_blocks, 0))
    return pl.pallas_call(
        _norm_proj_heads_kernel, grid=(m // tm,), in_specs=in_specs + [pl.BlockSpec((tm, LANES), tab)],
        out_specs=[pl.BlockSpec((tm, w), row) for w, _ in rows_out] + [hspec(sh[:-1], sh[-1]) for sh in head_shapes],
        out_shape=[jax.ShapeDtypeStruct((m, w), dt) for w, dt in rows_out] +
                  [jax.ShapeDtypeStruct((b,) + sh[:-1] + (t, sh[-1]), BF16) for sh in head_shapes],
        compiler_params=_params("arbitrary"), name="norm_proj_heads",
    )(x2, g, wq, wkv, wg, wr, cos, sa, sb, own)


def _rope_tables(pos):
    half = ROPE_DIM // 2
    inv_freq = ROPE_THETA ** (-jnp.arange(half, dtype=F32) / half)
    ang = pos.astype(F32)[:, None] * inv_freq[None, :]
    c, s = jnp.cos(ang), jnp.sin(ang)
    n = pos.shape[0]
    one = jnp.ones((n, HEAD_DIM - ROPE_DIM), F32)
    zero = jnp.zeros((n, HEAD_DIM - ROPE_DIM), F32)
    zh = jnp.zeros((n, half), F32)
    cos = jnp.concatenate([c, c, one], axis=1)
    sa = jnp.concatenate([-s, zh, zero], axis=1)
    sb = jnp.concatenate([zh, s, zero], axis=1)
    rep = LANES // HEAD_DIM
    return tuple(jnp.tile(t, (1, rep)) for t in (cos, sa, sb))


CMP_LANES = 2 * KV_WIDTH


def _compress_accumulate(position_rows, n_chunk, pea_ref, peb_ref, wa_ref, wb_ref, acca_ref, accb_ref, row0,
                         alongside=None):
    for c in range(2):
        lanes = slice(c * KV_WIDTH, (c + 1) * KV_WIDTH)
        da = db = None
        for l in range(CMP_STRIDE):
            if c == 0 and alongside is not None:
                alongside(l)
            x = position_rows(c, l)
            xa = (x + pea_ref[c, l:l + 1, :]).astype(BF16)
            xb = (x + peb_ref[c, l:l + 1, :]).astype(BF16)
            ta, tb = _dot(xa, wa_ref[c, l]), _dot(xb, wb_ref[c, l])
            da, db = (ta, tb) if l == 0 else (da + ta, db + tb)
        acca_ref[pl.ds(row0, n_chunk), lanes] = da
        accb_ref[pl.ds(row0, n_chunk), lanes] = db


def _compress_finish(acca_ref, accb_ref, w2_ref, out_ref):
    n = acca_ref.shape[0]
    rows = lax.broadcasted_iota(jnp.int32, (n, 1), 0)
    for c in range(2):
        lanes = slice(c * KV_WIDTH, (c + 1) * KV_WIDTH)
        nxt = pltpu.roll(accb_ref[:, lanes], n - 1, 0)
        h = jax.nn.gelu(acca_ref[:, lanes] + nxt).astype(BF16)
        out_ref[0, :, lanes] = jnp.where(rows < n - 1, _dot(h, w2_ref[c]), 0.0)


def _compress_prompt_kernel(k_ref, v_ref, pea_ref, peb_ref, wa_ref, wb_ref, w2_ref, out_ref, acca_ref, accb_ref):
    n = acca_ref.shape[0]
    refs = (k_ref.at[0], v_ref.at[0])
    strided = lambda c, l: refs[c][pl.ds(l, n, stride=CMP_STRIDE), :]
    _compress_accumulate(strided, n, pea_ref, peb_ref, wa_ref, wb_ref, acca_ref, accb_ref, 0)
    _compress_finish(acca_ref, accb_ref, w2_ref, out_ref)


_CW_NAMES = ("pea", "peb", "wa", "wb", "w2")


def _compress_prompt(kvw3, cw):
    b, t, _ = kvw3.shape
    n = t // CMP_STRIDE
    const = lambda a: pl.BlockSpec(a.shape, lambda i: (0,) * a.ndim)
    return pl.pallas_call(
        _compress_prompt_kernel, grid=(b,),
        in_specs=[pl.BlockSpec((1, t, KV_WIDTH), lambda i: (i, 0, 0)),
                  pl.BlockSpec((1, t, KV_WIDTH), lambda i: (i, 0, 1))] + [const(cw[k]) for k in _CW_NAMES],
        out_specs=pl.BlockSpec((1, n, CMP_LANES), lambda i: (i, 0, 0)),
        out_shape=jax.ShapeDtypeStruct((b, n, CMP_LANES), F32),
        scratch_shapes=[pltpu.VMEM((n, CMP_LANES), F32), pltpu.VMEM((n, CMP_LANES), F32)],
        compiler_params=_params("arbitrary"), name="compress_prompt",
    )(kvw3, kvw3, *[cw[k] for k in _CW_NAMES])


def _compress_weights(pe, w1, w2):
    eye = jnp.eye(KV_HEADS, dtype=F32)
    blockdiag = lambda w: jnp.einsum('...de,gj->...gdje', w, eye).reshape(w.shape[:-2] + (KV_WIDTH, KV_WIDTH))
    lanes = lambda p: jnp.tile(p, (1, 1, KV_HEADS))
    return {"pea": lanes(pe[:, :CMP_STRIDE]), "peb": lanes(pe[:, CMP_STRIDE:]),
            "wa": blockdiag(w1[:, :CMP_STRIDE]).astype(BF16), "wb": blockdiag(w1[:, CMP_STRIDE:]).astype(BF16),
            "w2": blockdiag(w2).astype(BF16)}


def _paged_fetch(page_copies, pages_per_step, steps_per_seq):
    step = pl.program_id(0) * steps_per_seq + pl.program_id(1)
    total = pl.num_programs(0) * steps_per_seq

    def start_all(s):
        def body(j, c):
            for cp in page_copies(s, j):
                cp.start()
            return c
        lax.fori_loop(0, pages_per_step, body, 0)

    @pl.when(step == 0)
    def _():
        start_all(step)

    @pl.when(step + 1 < total)
    def _():
        start_all(step + 1)

    def wait_body(j, c):
        for cp in page_copies(step, j):
            cp.wait()
        return c
    lax.fori_loop(0, pages_per_step, wait_body, 0)
    return step % 2


UNROLL_PAGES = 8


def _page_of(pt_ref, step, j, pages_per_step, steps_per_seq):
    return pt_ref[step // steps_per_seq, (step % steps_per_seq) * pages_per_step + j]


def _compress_sample_kernel(layer, pages_per_step, steps_per_seq, pt_ref, cache_ref, pea_ref, peb_ref, wa_ref,
                            wb_ref, w2_ref, out_ref, buf_ref, sem_ref, rows_ref, acca_ref, accb_ref):
    page_len = cache_ref.shape[3]

    def page_copies(step, j):
        page = _page_of(pt_ref, step, j, pages_per_step, steps_per_seq)
        return [pltpu.make_async_copy(cache_ref.at[layer, page, pl.ds(0, CMP_LANES), :], buf_ref.at[step % 2, j],
                                      sem_ref.at[step % 2])]

    slot = _paged_fetch(page_copies, pages_per_step, steps_per_seq)

    chunks = page_len // CMP_STRIDE

    def to_rows(c, j):
        tok = buf_ref[slot, j, c * KV_WIDTH:(c + 1) * KV_WIDTH, :].T
        pos = jnp.swapaxes(tok.reshape(chunks, CMP_STRIDE, KV_WIDTH), 0, 1)
        rows_ref[c, :, pl.ds(pl.multiple_of(j * chunks, chunks), chunks), :] = pos

    def key_rows(j, carry):
        to_rows(0, j)
        return carry
    lax.fori_loop(0, pages_per_step, key_rows, 0, unroll=UNROLL_PAGES)

    per_l = pages_per_step // CMP_STRIDE

    def value_rows(l):
        for j in range(l * per_l, (l + 1) * per_l):
            to_rows(1, j)

    part = pl.program_id(1)
    n = rows_ref.shape[2]
    row0 = pl.multiple_of(part * n, n)
    _compress_accumulate(lambda c, l: rows_ref[c, l], n, pea_ref, peb_ref, wa_ref, wb_ref, acca_ref, accb_ref, row0,
                         alongside=value_rows)

    @pl.when(part == steps_per_seq - 1)
    def _():
        _compress_finish(acca_ref, accb_ref, w2_ref, out_ref)


def _compress_sample(cache_t, page_table, layer, cw, steps_per_seq=2):
    db, n_pages = page_table.shape
    page = cache_t.shape[3]
    pps = n_pages // steps_per_seq
    n = n_pages * page // CMP_STRIDE
    const = lambda a: pl.BlockSpec(a.shape, lambda i, p, pt: (0,) * a.ndim)
    assert pps % CMP_STRIDE == 0
    kern = functools.partial(_compress_sample_kernel, layer, pps, steps_per_seq)
    return pl.pallas_call(
        kern,
        grid_spec=pltpu.PrefetchScalarGridSpec(
            num_scalar_prefetch=1, grid=(db, steps_per_seq),
            in_specs=[pl.BlockSpec(memory_space=pl.ANY)] + [const(cw[k]) for k in _CW_NAMES],
            out_specs=pl.BlockSpec((1, n, CMP_LANES), lambda i, p, pt: (i, 0, 0)),
            scratch_shapes=[pltpu.VMEM((2, pps, CMP_LANES, page), F32), pltpu.SemaphoreType.DMA((2,)),
                            pltpu.VMEM((2, CMP_STRIDE, pps * page // CMP_STRIDE, KV_WIDTH), F32),
                            pltpu.VMEM((n, CMP_LANES), F32), pltpu.VMEM((n, CMP_LANES), F32)]),
        out_shape=jax.ShapeDtypeStruct((db, n, CMP_LANES), F32),
        compiler_params=_params("arbitrary", "arbitrary"), name="compress_sample",
    )(page_table, cache_t, *[cw[k] for k in _CW_NAMES])


def _cmp_to_sel(n_cmp_pad, n_cmp, n_sel_pad, n_sel):
    c0 = np.arange(n_cmp_pad)[:, None] * CMP_STRIDE
    s0 = np.arange(n_sel_pad)[None, :] * SEL_BLOCK
    hit = (c0 < s0 + SEL_BLOCK) & (c0 + CMP_BLOCK > s0)
    hit &= (np.arange(n_cmp_pad)[:, None] < n_cmp) & (np.arange(n_sel_pad)[None, :] < n_sel)
    return hit.astype(np.float32)


def _block_to_key(n_blk, n_key):
    return (np.arange(n_key)[None, :] // SEL_BLOCK == np.arange(n_blk)[:, None]).astype(np.float32)


def _nsa_prompt_kernel(qb, tk, q_ref, kc_ref, vc_ref, ks_ref, vs_ref, kw_ref, vw_ref, gate_ref, c2s_ref, hi_ref,
                       o_ref):
    i = pl.program_id(2)
    qs = pl.multiple_of(i * qb, qb)
    rows = HEADS_PER_KV * qb
    n_cmp = kc_ref.shape[0]
    n_sel = c2s_ref.shape[0]
    q_pad = q_ref[...].reshape(rows, 2 * HEAD_DIM)
    q = q_pad[:, :HEAD_DIM]
    qpos = qs + lax.broadcasted_iota(jnp.int32, (rows, 1), 0) % qb

    def biased(s, bias):
        return (s.reshape(HEADS_PER_KV, qb, s.shape[-1]) + bias[None]).reshape(s.shape)

    def weights(s, m):
        return jnp.exp2((s - m).astype(BF16))

    def normalised(acc):
        return acc[:, :HEAD_DIM] / jnp.maximum(acc[:, HEAD_DIM:HEAD_DIM + 1], 1e-30)

    s_cmp = _dot_nt(q, kc_ref[...])
    nw = WINDOW + qb
    w0 = pl.multiple_of(jnp.maximum(qs - WINDOW, 0), qb)
    s_win = _dot_nt(q, kw_ref[pl.ds(w0, nw), :])

    cmp_end = lax.broadcasted_iota(jnp.int32, (1, n_cmp), 1) * CMP_STRIDE + (CMP_BLOCK - 1)
    p = _masked_softmax(s_cmp, cmp_end <= qpos)
    o_cmp = _dot(p.astype(BF16), vc_ref[...])
    psum = p[0:qb]
    for h in range(1, HEADS_PER_KV):
        psum = psum + p[h * qb:(h + 1) * qb]
    imp_t = _dot_nt(c2s_ref[...], psum, precision=HIGHEST)

    rel = (qs + lax.broadcasted_iota(jnp.int32, (qb, nw), 0)) - (w0 + lax.broadcasted_iota(jnp.int32, (qb, nw), 1))
    s = biased(s_win, jnp.where((rel >= 0) & (rel < WINDOW), 0.0, NEG))
    o_win = normalised(_dot(weights(s, jnp.max(s, axis=-1, keepdims=True)), vw_ref[pl.ds(w0, nw), :]))

    blk = lax.broadcasted_iota(jnp.int32, (n_sel, qb), 0)
    qp = qs + lax.broadcasted_iota(jnp.int32, (n_sel, qb), 1)
    cur = qp // SEL_BLOCK
    forced = (blk == 0) | (blk == cur) | (blk == cur - 1)
    valid = blk <= cur
    score = jnp.where(valid, jnp.where(forced, FORCED_SCORE, imp_t), NEG)
    cnt = jnp.zeros((n_sel, qb), jnp.int32)
    for j in range(n_sel):
        sj = score[j:j + 1, :]
        ahead = (sj > score) | ((sj == score) & (j < blk))
        cnt = cnt + ahead.astype(jnp.int32)
    def masked_queries(unsel):
        spread = _dot_tn(unsel.astype(BF16), hi_ref[...])
        return (q_pad.astype(F32) + jnp.concatenate([spread] * HEADS_PER_KV, axis=0)).astype(BF16)

    unsel = jnp.where((cnt < SEL_TOPN) & valid, 0.0, 1.0)
    q_all = masked_queries(unsel)
    q_sel = masked_queries(jnp.where(blk < qs // SEL_BLOCK, unsel, 1.0))

    qrow = lax.broadcasted_iota(jnp.int32, (qb, qb), 0)
    kcol = lax.broadcasted_iota(jnp.int32, (qb, qb), 1)
    s = biased(_dot_nt(q_all, ks_ref[pl.ds(qs, qb), :]), jnp.where(kcol <= qrow, 0.0, NEG))
    m = jnp.max(s, axis=-1, keepdims=True)
    acc = _dot(weights(s, m), vs_ref[pl.ds(qs, qb), :])

    def sel_scores(kt):
        return _dot_nt(q_sel, ks_ref[pl.ds(pl.multiple_of(kt * tk, tk), tk), :])

    def sel_update(kt, s, m, acc):
        m_new = jnp.maximum(m, jnp.max(s, axis=-1, keepdims=True))
        acc = jnp.exp2(m - m_new) * acc + _dot(weights(s, m_new), vs_ref[pl.ds(pl.multiple_of(kt * tk, tk), tk), :])
        return m_new, acc

    def sel_pair(kp, carry):
        s0, s1 = sel_scores(2 * kp), sel_scores(2 * kp + 1)
        return sel_update(2 * kp + 1, s1, *sel_update(2 * kp, s0, *carry))

    def sel_single(kt, carry):
        return sel_update(kt, sel_scores(kt), *carry)

    n_kt = (qs + tk - 1) // tk
    carry = lax.fori_loop(0, n_kt // 2, sel_pair, (m, acc))
    _, acc = lax.fori_loop(n_kt // 2 * 2, n_kt, sel_single, carry)
    o_sel = normalised(acc)

    g = gate_ref[...].reshape(rows, N_BRANCH)
    o = (g[:, 0:1] * o_cmp + g[:, 1:2] * o_sel + g[:, 2:3] * o_win).astype(o_ref.dtype)
    for h in range(HEADS_PER_KV):
        o_ref[:, h * HEAD_DIM:(h + 1) * HEAD_DIM] = o[h * qb:(h + 1) * qb]


def _nsa_prompt(qh, ksa, kw, va, cvh, gates_h, qb=256, tk=512):
    b, _, t, _ = qh.shape
    n_cmp = cvh.shape[3]
    n_sel = t // SEL_BLOCK
    assert qb % SEL_BLOCK == 0 and t % tk == 0 and WINDOW % qb == 0 and WINDOW + qb <= t and n_sel <= HEAD_DIM
    c2s = jnp.asarray(_cmp_to_sel(n_cmp, n_cmp - 1, n_sel, n_sel).T)
    hi = jnp.asarray(np.concatenate([np.zeros((n_sel, HEAD_DIM)), np.eye(n_sel, HEAD_DIM)], axis=1), dtype=BF16)
    vspec = lambda c: pl.BlockSpec((None, None, None, t, 2 * HEAD_DIM), lambda bi, g, i: (bi, c, g, 0, 0))
    ccomp = lambda c: pl.BlockSpec((None, None, None, n_cmp, HEAD_DIM), lambda bi, g, i: (bi, c, g, 0, 0))
    qspec = pl.BlockSpec((None, HEADS_PER_KV, qb, 2 * HEAD_DIM), lambda bi, g, i: (bi, g, i, 0))
    ospec = pl.BlockSpec((None, qb, HEADS_PER_KV * HEAD_DIM), lambda bi, g, i: (bi, i, g))
    kern = functools.partial(_nsa_prompt_kernel, qb, tk)
    return pl.pallas_call(
        kern, grid=(b, KV_HEADS, t // qb),
        in_specs=[qspec, ccomp(0), ccomp(1),
                  pl.BlockSpec((None, None, t, 2 * HEAD_DIM), lambda bi, g, i: (bi, g, 0, 0)), vspec(0),
                  pl.BlockSpec((None, None, t, HEAD_DIM), lambda bi, g, i: (bi, g, 0, 0)), vspec(1),
                  pl.BlockSpec((None, HEADS_PER_KV, qb, N_BRANCH), lambda bi, g, i: (bi, g, i, 0)),
                  pl.BlockSpec(c2s.shape, lambda bi, g, i: (0, 0)), pl.BlockSpec(hi.shape, lambda bi, g, i: (0, 0))],
        out_specs=ospec,
        out_shape=jax.ShapeDtypeStruct((b, t, NSA_WIDTH), BF16),
        compiler_params=_params("arbitrary", "arbitrary", "arbitrary"), name="nsa_prompt",
    )(qh, cvh, cvh, ksa, va, kw, va, gates_h, c2s, hi)


SROWS = 8 * KV_HEADS


def _sample_cmp_win_kernel(past_len, n_cmp, qz_ref, kvc_ref, win_ref, wnew_ref, c2s_ref, ocmp_ref, owin_ref,
                           imp_ref):
    qz = qz_ref[0]
    kvc = kvc_ref[0]
    n_pad = kvc.shape[0]
    kc, vc = kvc[:, :KV_WIDTH].astype(BF16), kvc[:, KV_WIDTH:].astype(BF16)
    s = _dot_nt(qz, kc)
    n = lax.broadcasted_iota(jnp.int32, (1, n_pad), 1)
    p = _masked_softmax(s, (n < n_cmp) & (n * CMP_STRIDE + (CMP_BLOCK - 1) <= past_len))
    ocmp_ref[0] = _dot(p.astype(BF16), vc)
    r = lax.broadcasted_iota(jnp.int32, (SROWS, 1), 0)
    p = jnp.where(r % 8 < HEADS_PER_KV, p, 0.0)
    psum = jnp.concatenate([jnp.sum(p[8 * g:8 * g + 8], axis=0, keepdims=True) for g in range(KV_HEADS)] +
                           [jnp.zeros((8 - KV_HEADS, n_pad), F32)], axis=0)
    imp_ref[0] = _dot(psum, c2s_ref[...], precision=HIGHEST)

    win = win_ref[0]
    wb = win.shape[1]
    kw, vw = win[:KV_WIDTH].astype(BF16), win[KV_WIDTH:].astype(BF16)
    wnew = wnew_ref[0]
    s = _dot(qz, kw)
    s_new = jnp.sum(qz.astype(F32) * wnew[:, :KV_WIDTH].astype(BF16).astype(F32), axis=-1, keepdims=True)
    rel = wb - lax.broadcasted_iota(jnp.int32, (1, wb), 1)
    msk = (rel >= 0) & (rel < WINDOW)
    s = jnp.where(msk, s, NEG)
    m = jnp.maximum(jnp.max(s, axis=-1, keepdims=True), s_new)
    e = jnp.where(msk, jnp.exp2(s - m), 0.0)
    e_new = jnp.exp2(s_new - m)
    den = jnp.maximum(jnp.sum(e, axis=-1, keepdims=True) + e_new, 1e-30)
    pv = _dot_nt((e / den).astype(BF16), vw)
    pn = (e_new / den).astype(BF16).astype(F32)
    owin_ref[0] = pv + pn * wnew[:, KV_WIDTH:].astype(BF16).astype(F32)


def _sample_cmp_win(past_len, n_cmp, qz, kvc, win, wnew, c2s):
    db = qz.shape[0]
    blk3 = lambda a: pl.BlockSpec((1,) + a.shape[1:], lambda i: (i, 0, 0))
    kern = functools.partial(_sample_cmp_win_kernel, past_len, n_cmp)
    outs = (jax.ShapeDtypeStruct((db, SROWS, KV_WIDTH), F32), jax.ShapeDtypeStruct((db, SROWS, KV_WIDTH), F32),
            jax.ShapeDtypeStruct((db, 8, c2s.shape[1]), F32))
    return pl.pallas_call(
        kern, grid=(db,),
        in_specs=[blk3(qz), blk3(kvc), blk3(win), blk3(wnew), pl.BlockSpec(c2s.shape, lambda i: (0, 0))],
        out_specs=[pl.BlockSpec((1, SROWS, KV_WIDTH), lambda i: (i, 0, 0)),
                   pl.BlockSpec((1, SROWS, KV_WIDTH), lambda i: (i, 0, 0)),
                   pl.BlockSpec((1, 8, c2s.shape[1]), lambda i: (i, 0, 0))],
        out_shape=outs, compiler_params=_params("arbitrary"), name="sample_cmp_win",
    )(qz, kvc, win, wnew, c2s)


def _select_kernel(n_sel, cur, imp_ref, sel_ref):
    imp = imp_ref[...]
    blk = lax.broadcasted_iota(jnp.int32, imp.shape, 1)
    forced = (blk == 0) | (blk == cur) | (blk == cur - 1)
    valid = blk <= cur
    score = jnp.where(valid, jnp.where(forced, FORCED_SCORE, imp), NEG)

    def body(j, cnt):
        sj = jnp.sum(jnp.where(blk == j, score, 0.0), axis=-1, keepdims=True)
        ahead = (sj > score) | ((sj == score) & (j < blk))
        return cnt + ahead.astype(jnp.int32)

    cnt = lax.fori_loop(0, n_sel, body, jnp.zeros(imp.shape, jnp.int32))
    sel_ref[...] = jnp.where((cnt < SEL_TOPN) & valid, 1.0, 0.0)


def _select(imp2, n_sel, cur):
    return pl.pallas_call(
        functools.partial(_select_kernel, n_sel, cur),
        out_shape=jax.ShapeDtypeStruct(imp2.shape, F32), name="sample_select",
        compiler_params=pltpu.CompilerParams(vmem_limit_bytes=VMEM_LIMIT),
    )(imp2)


def _sample_sel_kernel(layer, pages_per_step, steps_per_seq, pt_ref, cache_ref, qz_ref, sel_ref, selnew_ref,
                       knew_ref, b2k_ref, ocmp_ref, owin_ref, gate_ref, o_ref, buf_ref, sem_ref, m_ref, l_ref,
                       acc_ref):
    page_len = cache_ref.shape[3]

    def page_copies(step, j):
        page = _page_of(pt_ref, step, j, pages_per_step, steps_per_seq)
        return [pltpu.make_async_copy(cache_ref.at[layer, page, pl.ds(CMP_LANES, CMP_LANES), :],
                                      buf_ref.at[step % 2, :, pl.ds(pl.multiple_of(j * page_len, page_len), page_len)],
                                      sem_ref.at[step % 2])]

    slot = _paged_fetch(page_copies, pages_per_step, steps_per_seq)
    part = pl.program_id(1)
    qz = qz_ref[0]

    @pl.when(part == 0)
    def _():
        m_ref[...] = jnp.full(m_ref.shape, NEG, F32)
        l_ref[...] = jnp.zeros(l_ref.shape, F32)
        acc_ref[...] = jnp.zeros(acc_ref.shape, F32)

    kv = buf_ref[slot].astype(BF16)
    s = _dot(qz, kv[:KV_WIDTH])
    msk = _dot(sel_ref[0].astype(BF16), b2k_ref[...]) > 0.5
    s = jnp.where(msk, s, NEG)
    m_old = m_ref[...]
    m_new = jnp.maximum(m_old, jnp.max(s, axis=-1, keepdims=True))
    alpha = jnp.exp2(m_old - m_new)
    e = jnp.where(msk, jnp.exp2(s - m_new), 0.0)
    l_ref[...] = alpha * l_ref[...] + jnp.sum(e, axis=-1, keepdims=True)
    acc_ref[...] = alpha * acc_ref[...] + _dot_nt(e.astype(BF16), kv[KV_WIDTH:])
    m_ref[...] = m_new

    @pl.when(part == steps_per_seq - 1)
    def _():
        knew = knew_ref[0].astype(BF16).astype(F32)
        s_new = jnp.sum(qz.astype(F32) * knew[:, :KV_WIDTH], axis=-1, keepdims=True)
        on = selnew_ref[0][:, 0:1] > 0.5
        s_new = jnp.where(on, s_new, NEG)
        m_old = m_ref[...]
        m_new = jnp.maximum(m_old, s_new)
        alpha = jnp.exp2(m_old - m_new)
        e_new = jnp.where(on, jnp.exp2(s_new - m_new), 0.0)
        l = alpha * l_ref[...] + e_new
        acc = alpha * acc_ref[...] + e_new.astype(BF16).astype(F32) * knew[:, KV_WIDTH:]
        o_sel = acc / jnp.maximum(l, 1e-30)
        g = gate_ref[0]
        o_ref[0] = g[:, 0:1] * ocmp_ref[0] + g[:, 1:2] * o_sel + g[:, 2:3] * owin_ref[0]


def _sample_sel(cache_t, page_table, layer, qz, sel16, knew, ocmp, owin, gz, steps_per_seq=2):
    db, n_pages = page_table.shape
    page = cache_t.shape[3]
    pps = n_pages // steps_per_seq
    keys = pps * page
    blocks = keys // SEL_BLOCK
    b2k = jnp.asarray(_block_to_key(blocks, keys), dtype=BF16)
    blk3 = lambda a: pl.BlockSpec((1,) + a.shape[1:], lambda i, p, pt: (i, 0, 0))
    kern = functools.partial(_sample_sel_kernel, layer, pps, steps_per_seq)
    return pl.pallas_call(
        kern,
        grid_spec=pltpu.PrefetchScalarGridSpec(
            num_scalar_prefetch=1, grid=(db, steps_per_seq),
            in_specs=[pl.BlockSpec(memory_space=pl.ANY), blk3(qz),
                      pl.BlockSpec((1, SROWS, blocks), lambda i, p, pt: (i, 0, p)),
                      pl.BlockSpec((1, SROWS, LANES), lambda i, p, pt: (i, 0, steps_per_seq * blocks // LANES)),
                      blk3(knew),
                      pl.BlockSpec(b2k.shape, lambda i, p, pt: (0, 0)), blk3(ocmp), blk3(owin), blk3(gz)],
            out_specs=pl.BlockSpec((1, SROWS, KV_WIDTH), lambda i, p, pt: (i, 0, 0)),
            scratch_shapes=[pltpu.VMEM((2, CMP_LANES, keys), F32), pltpu.SemaphoreType.DMA((2,)),
                            pltpu.VMEM((SROWS, 1), F32), pltpu.VMEM((SROWS, 1), F32),
                            pltpu.VMEM((SROWS, KV_WIDTH), F32)]),
        out_shape=jax.ShapeDtypeStruct((db, SROWS, KV_WIDTH), F32),
        compiler_params=_params("arbitrary", "arbitrary"), name="sample_sel",
    )(page_table, cache_t, qz, sel16, sel16, knew, b2k, ocmp, owin, gz)


def _bdot(dot, a, b):
    return dot(a.astype(BF16), b.astype(BF16))


def _head_sum(x, hm):
    xh = x.astype(BF16)
    xl = (x - xh.astype(F32)).astype(BF16)
    return _dot(xh, hm) + _dot(xl, hm)


def _softplus(z):
    return jnp.maximum(z, 0.0) + jnp.log(1.0 + jnp.exp(-jnp.abs(z)))


CHUNK_GROUP = 4


def _rwkv_kernel(chunk, n_valid, pr_ref, prev_ref, s0_ref, mu_ref, w0_ref, wup_ref, a0_ref, aup_ref, gup_ref,
                 kk_ref, ka_ref, rk_ref, lnw_ref, lnb_ref, hm_ref, tri_ref, out_ref, sout_ref,
                 carry_ref, state_ref, r_ref, k_ref, v_ref, lw_ref, a_ref, b_ref, rh_ref, y_ref, m_ref, n_ref, et_ref):
    ti = pl.program_id(1)
    tt = pr_ref.shape[1]
    w_ = RWKV_WIDTH
    hd = RWKV_HEAD

    @pl.when(ti == 0)
    def _():
        carry_ref[...] = prev_ref[0]
        state_ref[...] = s0_ref[0]

    p = pr_ref[0]
    row = lax.broadcasted_iota(jnp.int32, (tt, 1), 0)
    shifted = jnp.where(row == 0, carry_ref[...], pltpu.roll(p, 1, 0))
    carry_ref[...] = p[tt - 1:tt]
    xs = p + (shifted - p) * mu_ref[...]
    r, k, v = xs[:, :w_], xs[:, w_:2 * w_], xs[:, 2 * w_:3 * w_]
    wa = xs[:, 3 * w_:3 * w_ + DECAY_LORA + AAA_LORA]
    gl = xs[:, 3 * w_ + DECAY_LORA + AAA_LORA:]
    hm = hm_ref[...]
    w = -_softplus(-(w0_ref[...] + _dot(jnp.tanh(wa).astype(BF16), wup_ref[...]))) - 0.5
    lw = -jnp.exp(w)
    a = jax.nn.sigmoid(a0_ref[...] + _dot(wa.astype(BF16), aup_ref[...]))
    g = _dot(jax.nn.sigmoid(gl).astype(BF16), gup_ref[...])
    kk = k * kk_ref[...]
    kk = kk / jnp.maximum(jnp.sqrt(_head_sum(kk * kk, hm)), 1e-12)
    k = k * (1.0 + (a - 1.0) * ka_ref[...])
    av, bv = -kk, kk * a
    if n_valid < tt:
        keep = row < n_valid
        lw, k, v, av, bv = (jnp.where(keep, t, 0.0) for t in (lw, k, v, av, bv))
    r_ref[...], k_ref[...], v_ref[...], lw_ref[...], a_ref[...], b_ref[...] = r, k, v, lw, av, bv

    ci = lax.broadcasted_iota(jnp.int32, (chunk, chunk), 0)
    cj = lax.broadcasted_iota(jnp.int32, (chunk, chunk), 1)
    strict, incl = ci > cj, ci >= cj
    dot = functools.partial(_bdot, _dot)
    dot_nt = functools.partial(_bdot, _dot_nt)
    dot_tn = functools.partial(_bdot, _dot_tn)
    group = min(CHUNK_GROUP, tt // chunk)

    def prepare(cg, carry):
        jobs, pre = [], []
        for ci in range(group):
            c = cg * group + ci
            rows = pl.ds(pl.multiple_of(c * chunk, chunk), chunk)
            lwc = lw_ref[rows, :]
            cs = _dot(tri_ref[...], lwc, precision=HIGHEST)
            e_in, e_out = jnp.exp(cs), jnp.exp(-cs)
            total = cs[chunk - 1:chunk]
            e_end = jnp.exp(total - cs)
            kc, bc = k_ref[rows, :], b_ref[rows, :]
            pre.append(dict(c=c, rows=rows, rt=r_ref[rows, :] * e_in, at=a_ref[rows, :] * jnp.exp(cs - lwc),
                            kt=kc * e_out, bt=bc * e_out, kh=kc * e_end, bh=bc * e_end, vc=v_ref[rows, :]))
            et_ref[c] = jnp.broadcast_to(jnp.exp(total), et_ref.shape[1:])
            jobs += [(ci, h, slice(h * hd, (h + 1) * hd)) for h in range(RWKV_HEADS)]
        nj = range(len(jobs))
        part = lambda name: [pre[ci][name][:, sl] for ci, _, sl in jobs]
        at, rt, bt, kt, bh, kh, vh = (part(n_) for n_ in ("at", "rt", "bt", "kt", "bh", "kh", "vc"))
        gm = [dot_nt(jnp.concatenate([at[j], rt[j]], axis=0), jnp.concatenate([bt[j], kt[j]], axis=0))
              for j in nj]
        lab = [jnp.where(strict, g_[:chunk, :chunk], 0.0) for g_ in gm]
        lak = [jnp.where(strict, g_[:chunk, chunk:], 0.0) for g_ in gm]
        mrb = [jnp.where(incl, g_[chunk:, :chunk], 0.0) for g_ in gm]
        mrk = [jnp.where(incl, g_[chunk:, chunk:], 0.0) for g_ in gm]
        wm = [jnp.concatenate([at[j], dot(lak[j], vh[j])], axis=1) for j in nj]
        x, n = lab, 1
        while 2 * n < chunk:
            xw = [dot(x[j], jnp.concatenate([wm[j], x[j]], axis=1)) for j in nj]
            x = [t[:, 2 * hd:] for t in xw]
            wm = [wm[j] + xw[j][:, :2 * hd] for j in nj]
            n *= 2
        wm = [wm[j] + dot(x[j], wm[j]) for j in nj]
        mw = [dot(mrb[j], wm[j]) for j in nj]
        y0 = [mw[j][:, hd:] + dot(mrk[j], vh[j]) for j in nj]
        wb = [dot_tn(wm[j], bh[j]) for j in nj]
        vk = [dot_tn(vh[j], kh[j]) for j in nj]
        for j, (ci, h, sl) in enumerate(jobs):
            c, rows = pre[ci]["c"], pre[ci]["rows"]
            rh_ref[rows, sl] = rt[j] + mw[j][:, :hd]
            y_ref[rows, sl] = y0[j]
            m_ref[c, h] = wb[j][:hd]
            n_ref[c, h] = wb[j][hd:] + vk[j]
        return carry

    def advance(c, carry):
        rows = pl.ds(pl.multiple_of(c * chunk, chunk), chunk)
        e_tot = et_ref[c]
        sls = [slice(h * hd, (h + 1) * hd) for h in range(RWKV_HEADS)]
        s0 = [state_ref[h] for h in range(RWKV_HEADS)]
        ys = [dot_nt(rh_ref[rows, sl], s0[h]) for h, sl in enumerate(sls)]
        sm = [dot(s0[h], m_ref[c, h]) for h in range(RWKV_HEADS)]
        for h, sl in enumerate(sls):
            y_ref[rows, sl] += ys[h]
            state_ref[h] = s0[h] * e_tot[0:1, sl] + sm[h] + n_ref[c, h]
        return carry

    lax.fori_loop(0, tt // chunk // group, prepare, 0)
    lax.fori_loop(0, tt // chunk, advance, 0)

    y = y_ref[...]
    inv_n = 1.0 / hd
    mean = _head_sum(y, hm) * inv_n
    d = y - mean
    var = _head_sum(d * d, hm) * inv_n
    y = d * lax.rsqrt(var + LNX_EPS) * lnw_ref[...] + lnb_ref[...]
    y = y + _head_sum(r_ref[...] * k_ref[...] * rk_ref[...], hm) * v_ref[...]
    out_ref[0] = (y * g).astype(out_ref.dtype)
    sout_ref[0] = state_ref[...]


def _rwkv(pr3, prev, s0, rw, chunk, tt, n_valid):
    b, t, _ = pr3.shape
    hm = jnp.asarray(np.kron(np.eye(RWKV_HEADS), np.ones((RWKV_HEAD, RWKV_HEAD))), dtype=BF16)
    tri = jnp.asarray(np.tril(np.ones((chunk, chunk))), dtype=F32)
    vec = lambda a: pl.BlockSpec(a.shape, lambda i, j: (0, 0))
    prev3 = prev[:, None, :]
    names = ("mu", "w0", "wup", "a0", "aup", "gup", "kk", "ka", "rk", "lnw", "lnb")
    kern = functools.partial(_rwkv_kernel, chunk, n_valid)
    per_chunk = (tt // chunk, RWKV_HEADS, RWKV_HEAD, RWKV_HEAD)
    return pl.pallas_call(
        kern, grid=(b, t // tt),
        in_specs=[pl.BlockSpec((1, tt, RWKV_PROJ), lambda i, j: (i, j, 0)),
                  pl.BlockSpec((1, 1, RWKV_PROJ), lambda i, j: (i, 0, 0)),
                  pl.BlockSpec((1, RWKV_HEADS, RWKV_HEAD, RWKV_HEAD), lambda i, j: (i, 0, 0, 0))] +
                 [vec(rw[n]) for n in names] + [vec(hm), vec(tri)],
        out_specs=[pl.BlockSpec((1, tt, RWKV_WIDTH), lambda i, j: (i, j, 0)),
                   pl.BlockSpec((1, RWKV_HEADS, RWKV_HEAD, RWKV_HEAD), lambda i, j: (i, 0, 0, 0))],
        out_shape=(jax.ShapeDtypeStruct((b, t, RWKV_WIDTH), BF16),
                   jax.ShapeDtypeStruct((b, RWKV_HEADS, RWKV_HEAD, RWKV_HEAD), F32)),
        scratch_shapes=[pltpu.VMEM((1, RWKV_PROJ), F32), pltpu.VMEM((RWKV_HEADS, RWKV_HEAD, RWKV_HEAD), F32)] +
                       [pltpu.VMEM((tt, RWKV_WIDTH), F32) for _ in range(8)] +
                       [pltpu.VMEM(per_chunk, F32), pltpu.VMEM(per_chunk, F32),
                        pltpu.VMEM((tt // chunk, 8, RWKV_WIDTH), F32)],
        compiler_params=_params("arbitrary", "arbitrary"), name="rwkv",
    )(pr3, prev3, s0, *[rw[n] for n in names], hm, tri)


def _ffn_head(x_ref, nsa_ref, rwkv_ref, wo_ref, gpost_ref, gpre_ref, x1_ref, h2_ref, acc_ref):
    n = nsa_ref.shape[1]
    mixed = _dot(nsa_ref[...], wo_ref[:n, :]) + _dot(rwkv_ref[...], wo_ref[n:, :])
    x1 = x_ref[...] + _rms(mixed, gpost_ref[...])
    x1_ref[...] = x1
    h2_ref[...] = _rms(x1, gpre_ref[...]).astype(BF16)
    acc_ref[...] = jnp.zeros(acc_ref.shape, F32)


def _ffn_tail(gate2, gate1, gate, val, cw_ref, cb_ref, wd_ref, acc_ref):
    conv = cb_ref[...] + (gate2 * cw_ref[0:1, :] + gate1 * cw_ref[1:2, :] + gate * cw_ref[2:3, :])
    acc_ref[...] += _dot((jax.nn.gelu(conv) * val).astype(BF16), wd_ref[...])


FFN_COLS = 256


def _ffn_seq_kernel(tiles_per_seq, x_ref, nsa_ref, rwkv_ref, wo_ref, gpost_ref, gpre_ref, gfpost_ref, wg_ref, wv_ref, cw_ref,
                    cb_ref, wd_ref, out_ref, tail_ref, x1_ref, h2_ref, acc_ref, carry_ref):
    i, j = pl.program_id(0), pl.program_id(1)
    tm = x_ref.shape[0]

    @pl.when(j == 0)
    def _():
        _ffn_head(x_ref, nsa_ref, rwkv_ref, wo_ref, gpost_ref, gpre_ref, x1_ref, h2_ref, acc_ref)

    @pl.when(i % tiles_per_seq == 0)
    def _():
        carry_ref[j] = jnp.zeros(carry_ref.shape[1:], F32)

    h2 = h2_ref[...]
    row = lax.broadcasted_iota(jnp.int32, (tm, 1), 0)
    tf = wg_ref.shape[1]
    cols = [slice(c0, min(c0 + FFN_COLS, tf)) for c0 in range(0, tf, FFN_COLS)]
    up = lambda c: (_dot(h2, wg_ref[:, c]), _dot(h2, wv_ref[:, c]))
    nxt = up(cols[0])
    for k, c in enumerate(cols):
        gate, val = nxt
        if k + 1 < len(cols):
            nxt = up(cols[k + 1])
        prev = carry_ref[j, :, c]
        gate1 = jnp.where(row == 0, prev[7:8], pltpu.roll(gate, 1, 0))
        gate2 = jnp.where(row == 0, prev[6:7], jnp.where(row == 1, prev[7:8], pltpu.roll(gate, 2, 0)))
        carry_ref[j, :, c] = gate[tm - 8:tm]
        tail_ref[0, :, c] = gate[tm - 8:tm]
        conv = cb_ref[:, c] + (gate2 * cw_ref[0:1, c] + gate1 * cw_ref[1:2, c] + gate * cw_ref[2:3, c])
        acc_ref[...] += _dot((jax.nn.gelu(conv) * val).astype(BF16), wd_ref[c, :])

    @pl.when(j == pl.num_programs(1) - 1)
    def _():
        out_ref[...] = x1_ref[...] + _rms(acc_ref[...], gfpost_ref[...])


def _ffn_step_kernel(x_ref, nsa_ref, rwkv_ref, wo_ref, gpost_ref, gpre_ref, gfpost_ref, wg_ref, wv_ref, cw_ref, cb_ref,
                     wd_ref, prev_ref, out_ref, gate_ref, x1_ref, h2_ref, acc_ref):
    j = pl.program_id(0)

    @pl.when(j == 0)
    def _():
        _ffn_head(x_ref, nsa_ref, rwkv_ref, wo_ref, gpost_ref, gpre_ref, x1_ref, h2_ref, acc_ref)

    h2 = h2_ref[...]
    gate, val = _dot(h2, wg_ref[...]), _dot(h2, wv_ref[...])
    gate_ref[...] = gate
    _ffn_tail(prev_ref[0], prev_ref[1], gate, val, cw_ref, cb_ref, wd_ref, acc_ref)

    @pl.when(j == pl.num_programs(0) - 1)
    def _():
        out_ref[...] = x1_ref[...] + _rms(acc_ref[...], gfpost_ref[...])


def _ffn_seq(x2, nsa2, rwkv2, fw, seq_len, tm, tf):
    m, d = x2.shape
    d_ff = fw["wd"].shape[0]
    nf = d_ff // tf
    row = lambda i, j: (i, 0)
    full = lambda i, j: (0, 0)
    tiles_per_seq = seq_len // tm
    scratch = [pltpu.VMEM((tm, d), F32), pltpu.VMEM((tm, d), BF16), pltpu.VMEM((tm, d), F32),
               pltpu.VMEM((nf, 8, tf), F32)]
    return pl.pallas_call(
        functools.partial(_ffn_seq_kernel, tiles_per_seq), grid=(m // tm, nf),
        in_specs=[pl.BlockSpec((tm, d), row), pl.BlockSpec((tm, nsa2.shape[1]), row),
                  pl.BlockSpec((tm, rwkv2.shape[1]), row), pl.BlockSpec((d, d), full),
                  pl.BlockSpec((1, d), full), pl.BlockSpec((1, d), full), pl.BlockSpec((1, d), full),
                  pl.BlockSpec((d, tf), lambda i, j: (0, j)), pl.BlockSpec((d, tf), lambda i, j: (0, nf + j)),
                  pl.BlockSpec((CONV_WIDTH, tf), lambda i, j: (0, j)), pl.BlockSpec((1, tf), lambda i, j: (0, j)),
                  pl.BlockSpec((tf, d), lambda i, j: (j, 0))],
        out_specs=[pl.BlockSpec((tm, d), row), pl.BlockSpec((1, 8, tf), lambda i, j: (i, 0, j))],
        out_shape=(jax.ShapeDtypeStruct((m, d), F32), jax.ShapeDtypeStruct((m // tm, 8, d_ff), F32)),
        scratch_shapes=scratch, compiler_params=_params("arbitrary", "arbitrary"), name="ffn_seq",
    )(x2, nsa2, rwkv2, fw["wo"], fw["gpost"], fw["gpre"], fw["gfpost"], fw["wup"], fw["wup"], fw["cw"], fw["cb"],
      fw["wd"])


def _ffn_step(x2, nsa2, rwkv2, fw, prev, tf):
    m, d = x2.shape
    d_ff = fw["wd"].shape[0]
    nf = d_ff // tf
    full = lambda j: (0, 0)
    scratch = [pltpu.VMEM((m, d), F32), pltpu.VMEM((m, d), BF16), pltpu.VMEM((m, d), F32)]
    return pl.pallas_call(
        _ffn_step_kernel, grid=(nf,),
        in_specs=[pl.BlockSpec((m, d), full), pl.BlockSpec(nsa2.shape, full), pl.BlockSpec(rwkv2.shape, full),
                  pl.BlockSpec((d, d), full),
                  pl.BlockSpec((1, d), full), pl.BlockSpec((1, d), full), pl.BlockSpec((1, d), full),
                  pl.BlockSpec((d, tf), lambda j: (0, j)), pl.BlockSpec((d, tf), lambda j: (0, nf + j)),
                  pl.BlockSpec((CONV_WIDTH, tf), lambda j: (0, j)), pl.BlockSpec((1, tf), lambda j: (0, j)),
                  pl.BlockSpec((tf, d), lambda j: (j, 0)),
                  pl.BlockSpec((CONV_WIDTH - 1, m, tf), lambda j: (0, 0, j))],
        out_specs=[pl.BlockSpec((m, d), full), pl.BlockSpec((m, tf), lambda j: (0, j))],
        out_shape=(jax.ShapeDtypeStruct((m, d), F32), jax.ShapeDtypeStruct((m, d_ff), F32)),
        scratch_shapes=scratch, compiler_params=_params("arbitrary"), name="ffn_step",
    )(x2, nsa2, rwkv2, fw["wo"], fw["gpost"], fw["gpre"], fw["gfpost"], fw["wup"], fw["wup"], fw["cw"], fw["cb"],
      fw["wd"], prev)


def _layer_weights(l, p):
    w_in = p["w_in"][l]
    o = NSA_WIDTH + 6 * KV_WIDTH
    wg = jnp.pad(w_in[:, o:NSA_PROJ], ((0, 0), (0, LANES - N_BRANCH * NSA_HEADS)))
    row = lambda a: a[l][None, :].astype(F32)
    zpad = jnp.zeros((AAA_LORA, RWKV_WIDTH), F32)
    rw = {"mu": row(p["rwkv_mu"]), "w0": row(p["rwkv_w0"]),
          "wup": jnp.concatenate([p["rwkv_w_up"][l], zpad], axis=0).astype(BF16),
          "a0": row(p["rwkv_a0"]), "aup": jnp.concatenate([zpad, p["rwkv_a_up"][l]], axis=0).astype(BF16),
          "gup": p["rwkv_g_up"][l].astype(BF16), "kk": row(p["rwkv_k_k"]), "ka": row(p["rwkv_k_a"]),
          "rk": p["rwkv_r_k"][l].reshape(1, RWKV_WIDTH), "lnw": row(p["rwkv_lnx_w"]), "lnb": row(p["rwkv_lnx_b"])}
    fw = {"wo": p["w_out"][l].astype(BF16), "gpost": row(p["norm_mix_post"]), "gpre": row(p["norm_ffn_pre"]),
          "gfpost": row(p["norm_ffn_post"]), "wup": p["ffn_w_up"][l].astype(BF16), "cw": p["ffn_conv_w"][l],
          "cb": row(p["ffn_conv_b"]), "wd": p["ffn_w_down"][l].astype(BF16)}
    return {"g_in": row(p["norm_mix_pre"]), "wq": w_in[:, :NSA_WIDTH].astype(BF16),
            "wkv": w_in[:, NSA_WIDTH:o].astype(BF16), "wg": wg.astype(BF16), "wr": w_in[:, NSA_PROJ:].astype(BF16),
            "cw": _compress_weights(p["cmp_pe"][l], p["cmp_w1"][l], p["cmp_w2"][l]), "rw": rw, "fw": fw}


def _prompt_layer(x, lw, tabs):
    b, t, d = x.shape
    m = b * t
    x2 = x.reshape(m, d)
    own_block = jnp.asarray(np.concatenate([np.zeros((t, HEAD_DIM)), NEG * _block_to_key(HEAD_DIM, t).T], axis=1),
                            dtype=BF16)
    kvw, gates, pr, qh, ksa, kw, va = _norm_proj(x2, lw["g_in"], lw["wq"], lw["wkv"], lw["wg"], lw["wr"], tabs,
                                                 tm=256, heads_of=(b, t, own_block))
    kvw3 = kvw.reshape(b, t, 6 * KV_WIDTH)
    kvc = _compress_prompt(kvw3, lw["cw"])
    cvh = kvc.astype(BF16).reshape(b, -1, 2, KV_HEADS, HEAD_DIM).transpose(0, 2, 3, 1, 4)
    gates_h = gates[:, :N_BRANCH * NSA_HEADS].reshape(b, t, NSA_HEADS, N_BRANCH).transpose(0, 2, 1, 3)
    o_nsa = _nsa_prompt(qh, ksa, kw, va, cvh, gates_h).reshape(m, NSA_WIDTH)
    s0 = jnp.zeros((b, RWKV_HEADS, RWKV_HEAD, RWKV_HEAD), F32)
    pr3 = pr.reshape(b, t, RWKV_PROJ)
    o_rwkv, wkv = _rwkv(pr3, jnp.zeros((b, RWKV_PROJ), F32), s0, lw["rw"], chunk=64, tt=min(t, 512), n_valid=t)
    d_ff = lw["fw"]["wd"].shape[0]
    tm = min(t, 512)
    y, tail = _ffn_seq(x2, o_nsa, o_rwkv.reshape(m, RWKV_WIDTH), lw["fw"], t, tm=tm, tf=d_ff // 2)
    tail = tail[t // tm - 1::t // tm]
    kv_rows = kvw3[:, :, :4 * KV_WIDTH].reshape(b, t, 4, KV_HEADS, HEAD_DIM)
    wn = min(WINDOW, t)
    win = kvw3[:, t - wn:, 4 * KV_WIDTH:].reshape(b, wn, 2, KV_HEADS, HEAD_DIM)
    return y.reshape(b, t, d), kv_rows, win, wkv, pr3[:, -1], tail[:, 8 - (CONV_WIDTH - 1):]


def _sample_layer(x, lw, tabs, l, cache_t, page_table, st_win, st_wkv, st_shift, st_conv):
    db, tn, d = x.shape
    past_len = page_table.shape[1] * cache_t.shape[3]
    x2 = x.reshape(db, d)
    q, kvw, gates, pr = _norm_proj(x2, lw["g_in"], lw["wq"], lw["wkv"], lw["wg"], lw["wr"], tabs, tm=db)
    qs = (q * (ATTN_SCALE * LOG2E)).astype(BF16).reshape(db, KV_HEADS, HEADS_PER_KV, HEAD_DIM)
    eye = jnp.eye(KV_HEADS, dtype=BF16)
    qz = jnp.einsum('bghd,gj->bghjd', qs, eye).reshape(db, KV_HEADS, HEADS_PER_KV, KV_WIDTH)
    qz = jnp.pad(qz, ((0, 0), (0, 0), (0, 8 - HEADS_PER_KV), (0, 0))).reshape(db, SROWS, KV_WIDTH)
    kvc = _compress_sample(cache_t, page_table, l, lw["cw"])
    n_cmp = past_len // CMP_STRIDE - 1
    n_sel = -(-(past_len + tn) // SEL_BLOCK)
    n_sel_pad = -(-n_sel // LANES) * LANES
    c2s = jnp.asarray(_cmp_to_sel(kvc.shape[1], n_cmp, n_sel_pad, n_sel))
    wb = st_win.shape[1]
    win = st_win.transpose(0, 2, 3, 4, 1).reshape(db, 2 * KV_WIDTH, wb)
    wnew = kvw[:, None, 4 * KV_WIDTH:]
    ocmp, owin, imp = _sample_cmp_win(past_len, n_cmp, qz, kvc, win, wnew, c2s)
    sel = _select(imp.reshape(db * 8, n_sel_pad), n_sel, past_len // SEL_BLOCK).reshape(db, 8, n_sel_pad)
    sel16 = jnp.broadcast_to(sel[:, :KV_HEADS, None, :], (db, KV_HEADS, 8, n_sel_pad)).reshape(db, SROWS, n_sel_pad)
    gz = gates[:, :N_BRANCH * NSA_HEADS].reshape(db, KV_HEADS, HEADS_PER_KV, N_BRANCH)
    gz = jnp.pad(gz, ((0, 0), (0, 0), (0, 8 - HEADS_PER_KV), (0, 0))).reshape(db, SROWS, N_BRANCH)
    knew = kvw[:, None, 2 * KV_WIDTH:4 * KV_WIDTH]
    o16 = _sample_sel(cache_t, page_table, l, qz, sel16, knew, ocmp, owin, gz)
    o4 = o16.reshape(db, KV_HEADS, 8, KV_HEADS, HEAD_DIM)[:, :, :HEADS_PER_KV]
    o_nsa = jnp.stack([o4[:, g, :, g] for g in range(KV_HEADS)], axis=1).reshape(db, NSA_WIDTH)
    pad_t = 8
    pr3 = jnp.pad(pr[:, None, :], ((0, 0), (0, pad_t - tn), (0, 0)))
    o_rwkv, wkv = _rwkv(pr3, st_shift, st_wkv, lw["rw"], chunk=pad_t, tt=pad_t, n_valid=tn)
    prev = st_conv.transpose(1, 0, 2)
    d_ff = lw["fw"]["wd"].shape[0]
    y, gate = _ffn_step(x2, o_nsa.astype(BF16), o_rwkv[:, 0], lw["fw"], prev, tf=d_ff // 2)
    kv_rows = kvw[:, :4 * KV_WIDTH].reshape(db, tn, 4, KV_HEADS, HEAD_DIM)
    win_new = jnp.concatenate([st_win[:, tn:], kvw[:, 4 * KV_WIDTH:].reshape(db, tn, 2, KV_HEADS, HEAD_DIM)], axis=1)
    conv_new = jnp.concatenate([st_conv[:, 1:], gate[:, None, :]], axis=1)
    return y.reshape(db, tn, d), kv_rows, win_new, wkv, pr, conv_new


def kernel(x_prompt, x_sample, cache_kv, page_table, state_win, state_wkv, state_shift, state_conv, norm_mix_pre, norm_mix_post, norm_ffn_pre, norm_ffn_post, w_in, w_out, cmp_pe, cmp_w1, cmp_w2, rwkv_mu, rwkv_w0, rwkv_w_up, rwkv_a0, rwkv_a_up, rwkv_g_up, rwkv_k_k, rwkv_k_a, rwkv_r_k, rwkv_lnx_w, rwkv_lnx_b, ffn_w_up, ffn_conv_w, ffn_conv_b, ffn_w_down):
    p = dict(norm_mix_pre=norm_mix_pre, norm_mix_post=norm_mix_post, norm_ffn_pre=norm_ffn_pre,
             norm_ffn_post=norm_ffn_post, w_in=w_in, w_out=w_out, cmp_pe=cmp_pe, cmp_w1=cmp_w1, cmp_w2=cmp_w2,
             rwkv_mu=rwkv_mu, rwkv_w0=rwkv_w0, rwkv_w_up=rwkv_w_up, rwkv_a0=rwkv_a0, rwkv_a_up=rwkv_a_up,
             rwkv_g_up=rwkv_g_up, rwkv_k_k=rwkv_k_k, rwkv_k_a=rwkv_k_a, rwkv_r_k=rwkv_r_k,
             rwkv_lnx_w=rwkv_lnx_w, rwkv_lnx_b=rwkv_lnx_b, ffn_w_up=ffn_w_up, ffn_conv_w=ffn_conv_w,
             ffn_conv_b=ffn_conv_b, ffn_w_down=ffn_w_down)
    depth = w_in.shape[0]
    b, t, _ = x_prompt.shape
    db, tn, _ = x_sample.shape
    n_pool, page = cache_kv.shape[1], cache_kv.shape[2]
    past_len = page_table.shape[1] * page
    cache_t = cache_kv.transpose(0, 1, 3, 4, 5, 2).reshape(depth, n_pool, 2 * CMP_LANES, page)
    tabs_p = _rope_tables(jnp.arange(t, dtype=jnp.int32))
    tabs_s = _rope_tables(jnp.full((db,), past_len, jnp.int32))
    xp, xs = x_prompt, x_sample
    outs_p, outs_s = [], []
    for l in range(depth):
        lw = _layer_weights(l, p)
        xp, *st_p = _prompt_layer(xp, lw, tabs_p)
        xs, *st_s = _sample_layer(xs, lw, tabs_s, l, cache_t, page_table, state_win[l], state_wkv[l],
                                  state_shift[l], state_conv[l])
        outs_p.append(st_p)
        outs_s.append(st_s)
    stack = lambda outs, i: jnp.stack([o[i] for o in outs])
    return (xp, xs, stack(outs_p, 0), stack(outs_s, 0), stack(outs_p, 1), stack(outs_s, 1),
            stack(outs_p, 2), stack(outs_s, 2), stack(outs_p, 3), stack(outs_s, 3),
            stack(outs_p, 4), stack(outs_s, 4))
```

```python
import functools

import jax
import jax.numpy as jnp
import numpy as np
from jax import lax
from jax.experimental import pallas as pl
from jax.experimental.pallas import tpu as pltpu

F32, BF16 = jnp.float32, jnp.bfloat16
HIGHEST = lax.Precision.HIGHEST

LANES = 128
VMEM_LIMIT = 56 * 1024 * 1024

HEAD_DIM = 64
NSA_HEADS = 8
KV_HEADS = 2
HEADS_PER_KV = NSA_HEADS // KV_HEADS
NSA_WIDTH = NSA_HEADS * HEAD_DIM
KV_WIDTH = KV_HEADS * HEAD_DIM
N_BRANCH = 3
CMP_STRIDE = 16
CMP_BLOCK = 2 * CMP_STRIDE
SEL_BLOCK = 64
SEL_TOPN = 16
WINDOW = 512
ROPE_THETA = 500000.0
ROPE_DIM = HEAD_DIM // 4
RWKV_HEAD = 64
RWKV_HEADS = 8
RWKV_WIDTH = RWKV_HEADS * RWKV_HEAD
DECAY_LORA = 64
AAA_LORA = 64
GATE_LORA = 128
RWKV_PROJ = 3 * RWKV_WIDTH + DECAY_LORA + AAA_LORA + GATE_LORA
NSA_PROJ = NSA_WIDTH + 6 * KV_WIDTH + N_BRANCH * NSA_HEADS
CONV_WIDTH = 3
NORM_EPS = 1e-6
LNX_EPS = 64e-5
NEG = -1e30
FORCED_SCORE = 1e9
ATTN_SCALE = HEAD_DIM ** -0.5
LOG2E = 1.4426950408889634


def _dot(a, b, precision=None):
    return jnp.dot(a, b, preferred_element_type=F32, precision=precision)


def _dot_nt(a, b, precision=None):
    return lax.dot_general(a, b, (((1,), (1,)), ((), ())), preferred_element_type=F32, precision=precision)


def _dot_tn(a, b, precision=None):
    return lax.dot_general(a, b, (((0,), (0,)), ((), ())), preferred_element_type=F32, precision=precision)


def _rms(x, g):
    return x * lax.rsqrt(jnp.mean(x * x, axis=-1, keepdims=True) + NORM_EPS) * g


def _masked_softmax(s, mask):
    s = jnp.where(mask, s, NEG)
    m = jnp.max(s, axis=-1, keepdims=True)
    e = jnp.where(mask, jnp.exp2(s - m), 0.0)
    return e / jnp.maximum(jnp.sum(e, axis=-1, keepdims=True), 1e-30)


def _params(*sem):
    return pltpu.CompilerParams(dimension_semantics=sem, vmem_limit_bytes=VMEM_LIMIT)


def _rope(v, cos, sa, sb):
    return v * cos + pltpu.roll(v, LANES - ROPE_DIM // 2, 1) * sa + pltpu.roll(v, ROPE_DIM // 2, 1) * sb


def _project(x_ref, g_ref, wq_ref, wkv_ref, wg_ref, wr_ref, tabs, kvw_ref, gate_ref, pr_ref):
    h = _rms(x_ref[...], g_ref[...]).astype(BF16)
    q = _dot(h, wq_ref[...])
    q = jnp.concatenate([_rope(q[:, j * LANES:(j + 1) * LANES], *tabs) for j in range(NSA_WIDTH // LANES)], axis=1)
    kv = _dot(h, wkv_ref[...])
    blocks = []
    for j in range(6 * KV_WIDTH // LANES):
        blk = kv[:, j * LANES:(j + 1) * LANES]
        blocks.append(_rope(blk, *tabs) if j % 2 == 0 else blk)
        kvw_ref[:, j * LANES:(j + 1) * LANES] = blocks[j]
    gate_ref[...] = jax.nn.sigmoid(_dot(h, wg_ref[...]))
    pr_ref[...] = _dot(h, wr_ref[...])
    return q, blocks


def _norm_proj_kernel(x_ref, g_ref, wq_ref, wkv_ref, wg_ref, wr_ref, cos_ref, sa_ref, sb_ref,
                      q_ref, kvw_ref, gate_ref, pr_ref):
    tabs = (cos_ref[...], sa_ref[...], sb_ref[...])
    q_ref[...], _ = _project(x_ref, g_ref, wq_ref, wkv_ref, wg_ref, wr_ref, tabs, kvw_ref, gate_ref, pr_ref)


def _norm_proj_heads_kernel(x_ref, g_ref, wq_ref, wkv_ref, wg_ref, wr_ref, cos_ref, sa_ref, sb_ref, own_ref,
                            kvw_ref, gate_ref, pr_ref, qh_ref, ksa_ref, kwh_ref, va_ref):
    tabs = (cos_ref[...], sa_ref[...], sb_ref[...])
    q, kv = _project(x_ref, g_ref, wq_ref, wkv_ref, wg_ref, wr_ref, tabs, kvw_ref, gate_ref, pr_ref)
    tm = q.shape[0]
    lane = lax.broadcasted_iota(jnp.int32, (tm, LANES), 1)
    low = lane < HEAD_DIM
    ones_lane = jnp.where(lane == HEAD_DIM, 1.0, 0.0)
    halves = lambda blk: (blk, pltpu.roll(blk, HEAD_DIM, 1))
    for j in range(NSA_WIDTH // LANES):
        for half, x in enumerate(halves(q[:, j * LANES:(j + 1) * LANES] * (ATTN_SCALE * LOG2E))):
            qh_ref[2 * j + half] = jnp.where(low, x, 0.0).astype(BF16)
    own = own_ref[...].astype(F32)
    for g, x in enumerate(halves(kv[2])):
        ksa_ref[g] = jnp.where(low, x, own).astype(BF16)
    for g, x in enumerate(halves(kv[4])):
        kwh_ref[g] = x[:, :HEAD_DIM].astype(BF16)
    for c, col in enumerate((3, 5)):
        for g, x in enumerate(halves(kv[col])):
            va_ref[c, g] = jnp.where(low, x, ones_lane).astype(BF16)


def _norm_proj(x2, g, wq, wkv, wg, wr, tabs, tm, heads_of=None):
    m, d = x2.shape
    cos, sa, sb = tabs
    tab_blocks = cos.shape[0] // tm
    row = lambda i: (i, 0)
    full = lambda i: (0, 0)
    tab = lambda i: (i % tab_blocks, 0)
    in_specs = [pl.BlockSpec((tm, d), row), pl.BlockSpec((1, d), full),
                pl.BlockSpec(wq.shape, full), pl.BlockSpec(wkv.shape, full),
                pl.BlockSpec(wg.shape, full), pl.BlockSpec(wr.shape, full),
                pl.BlockSpec((tm, LANES), tab), pl.BlockSpec((tm, LANES), tab), pl.BlockSpec((tm, LANES), tab)]
    rows_out = [(6 * KV_WIDTH, F32), (KV_HEADS * LANES, F32), (RWKV_PROJ, F32)]
    if heads_of is None:
        rows_out = [(NSA_WIDTH, F32)] + rows_out
        return pl.pallas_call(
            _norm_proj_kernel, grid=(m // tm,), in_specs=in_specs,
            out_specs=[pl.BlockSpec((tm, w), row) for w, _ in rows_out],
            out_shape=[jax.ShapeDtypeStruct((m, w), dt) for w, dt in rows_out],
            compiler_params=_params("arbitrary"), name="norm_proj",
        )(x2, g, wq, wkv, wg, wr, cos, sa, sb)
    b, t, own = heads_of
    head_shapes = [(NSA_HEADS, LANES), (KV_HEADS, LANES), (KV_HEADS, HEAD_DIM), (2, KV_HEADS, LANES)]
    hspec = lambda lead, w: pl.BlockSpec((None,) + lead + (tm, w),
                                         lambda i: (i // tab_blocks,) + (0,) * len(lead) + (i % tab_blocks, 0))
    return pl.pallas_call(
        _norm_proj_heads_kernel, grid=(m // tm,), in_specs=in_specs + [pl.BlockSpec((tm, LANES), tab)],
        out_specs=[pl.BlockSpec((tm, w), row) for w, _ in rows_out] + [hspec(sh[:-1], sh[-1]) for sh in head_shapes],
        out_shape=[jax.ShapeDtypeStruct((m, w), dt) for w, dt in rows_out] +
                  [jax.ShapeDtypeStruct((b,) + sh[:-1] + (t, sh[-1]), BF16) for sh in head_shapes],
        compiler_params=_params("arbitrary"), name="norm_proj_heads",
    )(x2, g, wq, wkv, wg, wr, cos, sa, sb, own)


def _rope_tables(pos):
    half = ROPE_DIM // 2
    inv_freq = ROPE_THETA ** (-jnp.arange(half, dtype=F32) / half)
    ang = pos.astype(F32)[:, None] * inv_freq[None, :]
    c, s = jnp.cos(ang), jnp.sin(ang)
    n = pos.shape[0]
    one = jnp.ones((n, HEAD_DIM - ROPE_DIM), F32)
    zero = jnp.zeros((n, HEAD_DIM - ROPE_DIM), F32)
    zh = jnp.zeros((n, half), F32)
    cos = jnp.concatenate([c, c, one], axis=1)
    sa = jnp.concatenate([-s, zh, zero], axis=1)
    sb = jnp.concatenate([zh, s, zero], axis=1)
    rep = LANES // HEAD_DIM
    return tuple(jnp.tile(t, (1, rep)) for t in (cos, sa, sb))


CMP_LANES = 2 * KV_WIDTH


def _compress_accumulate(position_rows, n_chunk, pea_ref, peb_ref, wa_ref, wb_ref, acca_ref, accb_ref, row0,
                         alongside=None):
    for c in range(2):
        lanes = slice(c * KV_WIDTH, (c + 1) * KV_WIDTH)
        da = db = None
        for l in range(CMP_STRIDE):
            if c == 0 and alongside is not None:
                alongside(l)
            x = position_rows(c, l)
            xa = (x + pea_ref[c, l:l + 1, :]).astype(BF16)
            xb = (x + peb_ref[c, l:l + 1, :]).astype(BF16)
            ta, tb = _dot(xa, wa_ref[c, l]), _dot(xb, wb_ref[c, l])
            da, db = (ta, tb) if l == 0 else (da + ta, db + tb)
        acca_ref[pl.ds(row0, n_chunk), lanes] = da
        accb_ref[pl.ds(row0, n_chunk), lanes] = db


def _compress_finish(acca_ref, accb_ref, w2_ref, out_ref):
    n = acca_ref.shape[0]
    rows = lax.broadcasted_iota(jnp.int32, (n, 1), 0)
    for c in range(2):
        lanes = slice(c * KV_WIDTH, (c + 1) * KV_WIDTH)
        nxt = pltpu.roll(accb_ref[:, lanes], n - 1, 0)
        h = jax.nn.gelu(acca_ref[:, lanes] + nxt).astype(BF16)
        out_ref[0, :, lanes] = jnp.where(rows < n - 1, _dot(h, w2_ref[c]), 0.0)


def _compress_prompt_kernel(k_ref, v_ref, pea_ref, peb_ref, wa_ref, wb_ref, w2_ref, out_ref, acca_ref, accb_ref):
    n = acca_ref.shape[0]
    refs = (k_ref.at[0], v_ref.at[0])
    strided = lambda c, l: refs[c][pl.ds(l, n, stride=CMP_STRIDE), :]
    _compress_accumulate(strided, n, pea_ref, peb_ref, wa_ref, wb_ref, acca_ref, accb_ref, 0)
    _compress_finish(acca_ref, accb_ref, w2_ref, out_ref)


_CW_NAMES = ("pea", "peb", "wa", "wb", "w2")


def _compress_prompt(kvw3, cw):
    b, t, _ = kvw3.shape
    n = t // CMP_STRIDE
    const = lambda a: pl.BlockSpec(a.shape, lambda i: (0,) * a.ndim)
    return pl.pallas_call(
        _compress_prompt_kernel, grid=(b,),
        in_specs=[pl.BlockSpec((1, t, KV_WIDTH), lambda i: (i, 0, 0)),
                  pl.BlockSpec((1, t, KV_WIDTH), lambda i: (i, 0, 1))] + [const(cw[k]) for k in _CW_NAMES],
        out_specs=pl.BlockSpec((1, n, CMP_LANES), lambda i: (i, 0, 0)),
        out_shape=jax.ShapeDtypeStruct((b, n, CMP_LANES), F32),
        scratch_shapes=[pltpu.VMEM((n, CMP_LANES), F32), pltpu.VMEM((n, CMP_LANES), F32)],
        compiler_params=_params("arbitrary"), name="compress_prompt",
    )(kvw3, kvw3, *[cw[k] for k in _CW_NAMES])


def _compress_weights(pe, w1, w2):
    eye = jnp.eye(KV_HEADS, dtype=F32)
    blockdiag = lambda w: jnp.einsum('...de,gj->...gdje', w, eye).reshape(w.shape[:-2] + (KV_WIDTH, KV_WIDTH))
    lanes = lambda p: jnp.tile(p, (1, 1, KV_HEADS))
    return {"pea": lanes(pe[:, :CMP_STRIDE]), "peb": lanes(pe[:, CMP_STRIDE:]),
            "wa": blockdiag(w1[:, :CMP_STRIDE]).astype(BF16), "wb": blockdiag(w1[:, CMP_STRIDE:]).astype(BF16),
            "w2": blockdiag(w2).astype(BF16)}


def _paged_fetch(page_copies, pages_per_step, steps_per_seq):
    step = pl.program_id(0) * steps_per_seq + pl.program_id(1)
    total = pl.num_programs(0) * steps_per_seq

    def start_all(s):
        def body(j, c):
            for cp in page_copies(s, j):
                cp.start()
            return c
        lax.fori_loop(0, pages_per_step, body, 0)

    @pl.when(step == 0)
    def _():
        start_all(step)

    @pl.when(step + 1 < total)
    def _():
        start_all(step + 1)

    def wait_body(j, c):
        for cp in page_copies(step, j):
            cp.wait()
        return c
    lax.fori_loop(0, pages_per_step, wait_body, 0)
    return step % 2


UNROLL_PAGES = 8


def _page_of(pt_ref, step, j, pages_per_step, steps_per_seq):
    return pt_ref[step // steps_per_seq, (step % steps_per_seq) * pages_per_step + j]


def _compress_sample_kernel(layer, pages_per_step, steps_per_seq, pt_ref, cache_ref, pea_ref, peb_ref, wa_ref,
                            wb_ref, w2_ref, out_ref, buf_ref, sem_ref, rows_ref, acca_ref, accb_ref):
    page_len = cache_ref.shape[3]

    def page_copies(step, j):
        page = _page_of(pt_ref, step, j, pages_per_step, steps_per_seq)
        return [pltpu.make_async_copy(cache_ref.at[layer, page, pl.ds(0, CMP_LANES), :], buf_ref.at[step % 2, j],
                                      sem_ref.at[step % 2])]

    slot = _paged_fetch(page_copies, pages_per_step, steps_per_seq)

    chunks = page_len // CMP_STRIDE

    def to_rows(c, j):
        tok = buf_ref[slot, j, c * KV_WIDTH:(c + 1) * KV_WIDTH, :].T
        pos = jnp.swapaxes(tok.reshape(chunks, CMP_STRIDE, KV_WIDTH), 0, 1)
        rows_ref[c, :, pl.ds(pl.multiple_of(j * chunks, chunks), chunks), :] = pos

    def key_rows(j, carry):
        to_rows(0, j)
        return carry
    lax.fori_loop(0, pages_per_step, key_rows, 0, unroll=UNROLL_PAGES)

    per_l = pages_per_step // CMP_STRIDE

    def value_rows(l):
        for j in range(l * per_l, (l + 1) * per_l):
            to_rows(1, j)

    part = pl.program_id(1)
    n = rows_ref.shape[2]
    row0 = pl.multiple_of(part * n, n)
    _compress_accumulate(lambda c, l: rows_ref[c, l], n, pea_ref, peb_ref, wa_ref, wb_ref, acca_ref, accb_ref, row0,
                         alongside=value_rows)

    @pl.when(part == steps_per_seq - 1)
    def _():
        _compress_finish(acca_ref, accb_ref, w2_ref, out_ref)


def _compress_sample(cache_t, page_table, layer, cw, steps_per_seq=2):
    db, n_pages = page_table.shape
    page = cache_t.shape[3]
    pps = n_pages // steps_per_seq
    n = n_pages * page // CMP_STRIDE
    const = lambda a: pl.BlockSpec(a.shape, lambda i, p, pt: (0,) * a.ndim)
    assert pps % CMP_STRIDE == 0
    kern = functools.partial(_compress_sample_kernel, layer, pps, steps_per_seq)
    return pl.pallas_call(
        kern,
        grid_spec=pltpu.PrefetchScalarGridSpec(
            num_scalar_prefetch=1, grid=(db, steps_per_seq),
            in_specs=[pl.BlockSpec(memory_space=pl.ANY)] + [const(cw[k]) for k in _CW_NAMES],
            out_specs=pl.BlockSpec((1, n, CMP_LANES), lambda i, p, pt: (i, 0, 0)),
            scratch_shapes=[pltpu.VMEM((2, pps, CMP_LANES, page), F32), pltpu.SemaphoreType.DMA((2,)),
                            pltpu.VMEM((2, CMP_STRIDE, pps * page // CMP_STRIDE, KV_WIDTH), F32),
                            pltpu.VMEM((n, CMP_LANES), F32), pltpu.VMEM((n, CMP_LANES), F32)]),
        out_shape=jax.ShapeDtypeStruct((db, n, CMP_LANES), F32),
        compiler_params=_params("arbitrary", "arbitrary"), name="compress_sample",
    )(page_table, cache_t, *[cw[k] for k in _CW_NAMES])


def _cmp_to_sel(n_cmp_pad, n_cmp, n_sel_pad, n_sel):
    c0 = np.arange(n_cmp_pad)[:, None] * CMP_STRIDE
    s0 = np.arange(n_sel_pad)[None, :] * SEL_BLOCK
    hit = (c0 < s0 + SEL_BLOCK) & (c0 + CMP_BLOCK > s0)
    hit &= (np.arange(n_cmp_pad)[:, None] < n_cmp) & (np.arange(n_sel_pad)[None, :] < n_sel)
    return hit.astype(np.float32)


def _block_to_key(n_blk, n_key):
    return (np.arange(n_key)[None, :] // SEL_BLOCK == np.arange(n_blk)[:, None]).astype(np.float32)


def _nsa_prompt_kernel(qb, tk, q_ref, kc_ref, vc_ref, ks_ref, vs_ref, kw_ref, vw_ref, gate_ref, c2s_ref, hi_ref,
                       o_ref):
    i = pl.program_id(2)
    qs = pl.multiple_of(i * qb, qb)
    rows = HEADS_PER_KV * qb
    n_cmp = kc_ref.shape[0]
    n_sel = c2s_ref.shape[0]
    q_pad = q_ref[...].reshape(rows, 2 * HEAD_DIM)
    q = q_pad[:, :HEAD_DIM]
    qpos = qs + lax.broadcasted_iota(jnp.int32, (rows, 1), 0) % qb

    def biased(s, bias):
        return (s.reshape(HEADS_PER_KV, qb, s.shape[-1]) + bias[None]).reshape(s.shape)

    def weights(s, m):
        return jnp.exp2((s - m).astype(BF16))

    def normalised(acc):
        return acc[:, :HEAD_DIM] / jnp.maximum(acc[:, HEAD_DIM:HEAD_DIM + 1], 1e-30)

    s_cmp = _dot_nt(q, kc_ref[...])
    nw = WINDOW + qb
    w0 = pl.multiple_of(jnp.maximum(qs - WINDOW, 0), qb)
    s_win = _dot_nt(q, kw_ref[pl.ds(w0, nw), :])

    cmp_end = lax.broadcasted_iota(jnp.int32, (1, n_cmp), 1) * CMP_STRIDE + (CMP_BLOCK - 1)
    p = _masked_softmax(s_cmp, cmp_end <= qpos)
    o_cmp = _dot(p.astype(BF16), vc_ref[...])
    psum = p[0:qb]
    for h in range(1, HEADS_PER_KV):
        psum = psum + p[h * qb:(h + 1) * qb]
    imp_t = _dot_nt(c2s_ref[...], psum, precision=HIGHEST)

    rel = (qs + lax.broadcasted_iota(jnp.int32, (qb, nw), 0)) - (w0 + lax.broadcasted_iota(jnp.int32, (qb, nw), 1))
    s = biased(s_win, jnp.where((rel >= 0) & (rel < WINDOW), 0.0, NEG))
    o_win = normalised(_dot(weights(s, jnp.max(s, axis=-1, keepdims=True)), vw_ref[pl.ds(w0, nw), :]))

    blk = lax.broadcasted_iota(jnp.int32, (n_sel, qb), 0)
    qp = qs + lax.broadcasted_iota(jnp.int32, (n_sel, qb), 1)
    cur = qp // SEL_BLOCK
    forced = (blk == 0) | (blk == cur) | (blk == cur - 1)
    valid = blk <= cur
    score = jnp.where(valid, jnp.where(forced, FORCED_SCORE, imp_t), NEG)
    cnt = jnp.zeros((n_sel, qb), jnp.int32)
    for j in range(n_sel):
        sj = score[j:j + 1, :]
        ahead = (sj > score) | ((sj == score) & (j < blk))
        cnt = cnt + ahead.astype(jnp.int32)
    def masked_queries(unsel):
        spread = _dot_tn(unsel.astype(BF16), hi_ref[...])
        return (q_pad.astype(F32) + jnp.concatenate([spread] * HEADS_PER_KV, axis=0)).astype(BF16)

    unsel = jnp.where((cnt < SEL_TOPN) & valid, 0.0, 1.0)
    q_all = masked_queries(unsel)
    q_sel = masked_queries(jnp.where(blk < qs // SEL_BLOCK, unsel, 1.0))

    qrow = lax.broadcasted_iota(jnp.int32, (qb, qb), 0)
    kcol = lax.broadcasted_iota(jnp.int32, (qb, qb), 1)
    s = biased(_dot_nt(q_all, ks_ref[pl.ds(qs, qb), :]), jnp.where(kcol <= qrow, 0.0, NEG))
    m = jnp.max(s, axis=-1, keepdims=True)
    acc = _dot(weights(s, m), vs_ref[pl.ds(qs, qb), :])

    def sel_scores(kt):
        return _dot_nt(q_sel, ks_ref[pl.ds(pl.multiple_of(kt * tk, tk), tk), :])

    def sel_update(kt, s, m, acc):
        m_new = jnp.maximum(m, jnp.max(s, axis=-1, keepdims=True))
        acc = jnp.exp2(m - m_new) * acc + _dot(weights(s, m_new), vs_ref[pl.ds(pl.multiple_of(kt * tk, tk), tk), :])
        return m_new, acc

    def sel_pair(kp, carry):
        s0, s1 = sel_scores(2 * kp), sel_scores(2 * kp + 1)
        return sel_update(2 * kp + 1, s1, *sel_update(2 * kp, s0, *carry))

    def sel_single(kt, carry):
        return sel_update(kt, sel_scores(kt), *carry)

    n_kt = (qs + tk - 1) // tk
    carry = lax.fori_loop(0, n_kt // 2, sel_pair, (m, acc))
    _, acc = lax.fori_loop(n_kt // 2 * 2, n_kt, sel_single, carry)
    o_sel = normalised(acc)

    gt = gate_ref[...]
    g = [jnp.concatenate([gt[:, N_BRANCH * h + c:N_BRANCH * h + c + 1] for h in range(HEADS_PER_KV)], axis=0)
         for c in range(N_BRANCH)]
    o = (g[0] * o_cmp + g[1] * o_sel + g[2] * o_win).astype(o_ref.dtype)
    for h in range(HEADS_PER_KV):
        o_ref[:, h * HEAD_DIM:(h + 1) * HEAD_DIM] = o[h * qb:(h + 1) * qb]


def _nsa_prompt(qh, ksa, kw, va, cvh, gates_h, qb=256, tk=512):
    b, _, t, _ = qh.shape
    n_cmp = cvh.shape[3]
    n_sel = t // SEL_BLOCK
    assert qb % SEL_BLOCK == 0 and t % tk == 0 and WINDOW % qb == 0 and WINDOW + qb <= t and n_sel <= HEAD_DIM
    c2s = jnp.asarray(_cmp_to_sel(n_cmp, n_cmp - 1, n_sel, n_sel).T)
    hi = jnp.asarray(np.concatenate([np.zeros((n_sel, HEAD_DIM)), np.eye(n_sel, HEAD_DIM)], axis=1), dtype=BF16)
    vspec = lambda c: pl.BlockSpec((None, None, None, t, 2 * HEAD_DIM), lambda bi, g, i: (bi, c, g, 0, 0))
    ccomp = lambda c: pl.BlockSpec((None, None, None, n_cmp, HEAD_DIM), lambda bi, g, i: (bi, c, g, 0, 0))
    qspec = pl.BlockSpec((None, HEADS_PER_KV, qb, 2 * HEAD_DIM), lambda bi, g, i: (bi, g, i, 0))
    ospec = pl.BlockSpec((None, qb, HEADS_PER_KV * HEAD_DIM), lambda bi, g, i: (bi, i, g))
    kern = functools.partial(_nsa_prompt_kernel, qb, tk)
    return pl.pallas_call(
        kern, grid=(b, KV_HEADS, t // qb),
        in_specs=[qspec, ccomp(0), ccomp(1),
                  pl.BlockSpec((None, None, t, 2 * HEAD_DIM), lambda bi, g, i: (bi, g, 0, 0)), vspec(0),
                  pl.BlockSpec((None, None, t, HEAD_DIM), lambda bi, g, i: (bi, g, 0, 0)), vspec(1),
                  pl.BlockSpec((None, qb, LANES), lambda bi, g, i: (bi, i, g)),
                  pl.BlockSpec(c2s.shape, lambda bi, g, i: (0, 0)), pl.BlockSpec(hi.shape, lambda bi, g, i: (0, 0))],
        out_specs=ospec,
        out_shape=jax.ShapeDtypeStruct((b, t, NSA_WIDTH), BF16),
        compiler_params=_params("arbitrary", "arbitrary", "arbitrary"), name="nsa_prompt",
    )(qh, cvh, cvh, ksa, va, kw, va, gates_h, c2s, hi)


SROWS = 8 * KV_HEADS


def _sample_cmp_win_kernel(past_len, n_cmp, qz_ref, kvc_ref, win_ref, wnew_ref, c2s_ref, ocmp_ref, owin_ref,
                           imp_ref):
    qz = qz_ref[0]
    kvc = kvc_ref[0]
    n_pad = kvc.shape[0]
    kc, vc = kvc[:, :KV_WIDTH].astype(BF16), kvc[:, KV_WIDTH:].astype(BF16)
    s = _dot_nt(qz, kc)
    n = lax.broadcasted_iota(jnp.int32, (1, n_pad), 1)
    p = _masked_softmax(s, (n < n_cmp) & (n * CMP_STRIDE + (CMP_BLOCK - 1) <= past_len))
    ocmp_ref[0] = _dot(p.astype(BF16), vc)
    r = lax.broadcasted_iota(jnp.int32, (SROWS, 1), 0)
    p = jnp.where(r % 8 < HEADS_PER_KV, p, 0.0)
    psum = jnp.concatenate([jnp.sum(p[8 * g:8 * g + 8], axis=0, keepdims=True) for g in range(KV_HEADS)] +
                           [jnp.zeros((8 - KV_HEADS, n_pad), F32)], axis=0)
    imp_ref[0] = _dot(psum, c2s_ref[...], precision=HIGHEST)

    win = win_ref[0]
    wb = win.shape[1]
    kw, vw = win[:KV_WIDTH].astype(BF16), win[KV_WIDTH:].astype(BF16)
    wnew = wnew_ref[0]
    s = _dot(qz, kw)
    s_new = jnp.sum(qz.astype(F32) * wnew[:, :KV_WIDTH].astype(BF16).astype(F32), axis=-1, keepdims=True)
    rel = wb - lax.broadcasted_iota(jnp.int32, (1, wb), 1)
    msk = (rel >= 0) & (rel < WINDOW)
    s = jnp.where(msk, s, NEG)
    m = jnp.maximum(jnp.max(s, axis=-1, keepdims=True), s_new)
    e = jnp.where(msk, jnp.exp2(s - m), 0.0)
    e_new = jnp.exp2(s_new - m)
    den = jnp.maximum(jnp.sum(e, axis=-1, keepdims=True) + e_new, 1e-30)
    pv = _dot_nt((e / den).astype(BF16), vw)
    pn = (e_new / den).astype(BF16).astype(F32)
    owin_ref[0] = pv + pn * wnew[:, KV_WIDTH:].astype(BF16).astype(F32)


def _sample_cmp_win(past_len, n_cmp, qz, kvc, win, wnew, c2s):
    db = qz.shape[0]
    blk3 = lambda a: pl.BlockSpec((1,) + a.shape[1:], lambda i: (i, 0, 0))
    kern = functools.partial(_sample_cmp_win_kernel, past_len, n_cmp)
    outs = (jax.ShapeDtypeStruct((db, SROWS, KV_WIDTH), F32), jax.ShapeDtypeStruct((db, SROWS, KV_WIDTH), F32),
            jax.ShapeDtypeStruct((db, 8, c2s.shape[1]), F32))
    return pl.pallas_call(
        kern, grid=(db,),
        in_specs=[blk3(qz), blk3(kvc), blk3(win), blk3(wnew), pl.BlockSpec(c2s.shape, lambda i: (0, 0))],
        out_specs=[pl.BlockSpec((1, SROWS, KV_WIDTH), lambda i: (i, 0, 0)),
                   pl.BlockSpec((1, SROWS, KV_WIDTH), lambda i: (i, 0, 0)),
                   pl.BlockSpec((1, 8, c2s.shape[1]), lambda i: (i, 0, 0))],
        out_shape=outs, compiler_params=_params("arbitrary"), name="sample_cmp_win",
    )(qz, kvc, win, wnew, c2s)


def _select_kernel(n_sel, cur, imp_ref, sel_ref):
    imp = imp_ref[...]
    blk = lax.broadcasted_iota(jnp.int32, imp.shape, 1)
    forced = (blk == 0) | (blk == cur) | (blk == cur - 1)
    valid = blk <= cur
    score = jnp.where(valid, jnp.where(forced, FORCED_SCORE, imp), NEG)

    def body(j, cnt):
        sj = jnp.sum(jnp.where(blk == j, score, 0.0), axis=-1, keepdims=True)
        ahead = (sj > score) | ((sj == score) & (j < blk))
        return cnt + ahead.astype(jnp.int32)

    cnt = lax.fori_loop(0, n_sel, body, jnp.zeros(imp.shape, jnp.int32))
    sel_ref[...] = jnp.where((cnt < SEL_TOPN) & valid, 1.0, 0.0)


def _select(imp2, n_sel, cur):
    return pl.pallas_call(
        functools.partial(_select_kernel, n_sel, cur),
        out_shape=jax.ShapeDtypeStruct(imp2.shape, F32), name="sample_select",
        compiler_params=pltpu.CompilerParams(vmem_limit_bytes=VMEM_LIMIT),
    )(imp2)


def _sample_sel_kernel(layer, pages_per_step, steps_per_seq, pt_ref, cache_ref, qz_ref, sel_ref, selnew_ref,
                       knew_ref, b2k_ref, ocmp_ref, owin_ref, gate_ref, o_ref, buf_ref, sem_ref, m_ref, l_ref,
                       acc_ref):
    page_len = cache_ref.shape[3]

    def page_copies(step, j):
        page = _page_of(pt_ref, step, j, pages_per_step, steps_per_seq)
        return [pltpu.make_async_copy(cache_ref.at[layer, page, pl.ds(CMP_LANES, CMP_LANES), :],
                                      buf_ref.at[step % 2, :, pl.ds(pl.multiple_of(j * page_len, page_len), page_len)],
                                      sem_ref.at[step % 2])]

    slot = _paged_fetch(page_copies, pages_per_step, steps_per_seq)
    part = pl.program_id(1)
    qz = qz_ref[0]

    @pl.when(part == 0)
    def _():
        m_ref[...] = jnp.full(m_ref.shape, NEG, F32)
        l_ref[...] = jnp.zeros(l_ref.shape, F32)
        acc_ref[...] = jnp.zeros(acc_ref.shape, F32)

    kv = buf_ref[slot].astype(BF16)
    s = _dot(qz, kv[:KV_WIDTH])
    msk = _dot(sel_ref[0].astype(BF16), b2k_ref[...]) > 0.5
    s = jnp.where(msk, s, NEG)
    m_old = m_ref[...]
    m_new = jnp.maximum(m_old, jnp.max(s, axis=-1, keepdims=True))
    alpha = jnp.exp2(m_old - m_new)
    e = jnp.where(msk, jnp.exp2(s - m_new), 0.0)
    l_ref[...] = alpha * l_ref[...] + jnp.sum(e, axis=-1, keepdims=True)
    acc_ref[...] = alpha * acc_ref[...] + _dot_nt(e.astype(BF16), kv[KV_WIDTH:])
    m_ref[...] = m_new

    @pl.when(part == steps_per_seq - 1)
    def _():
        knew = knew_ref[0].astype(BF16).astype(F32)
        s_new = jnp.sum(qz.astype(F32) * knew[:, :KV_WIDTH], axis=-1, keepdims=True)
        on = selnew_ref[0][:, 0:1] > 0.5
        s_new = jnp.where(on, s_new, NEG)
        m_old = m_ref[...]
        m_new = jnp.maximum(m_old, s_new)
        alpha = jnp.exp2(m_old - m_new)
        e_new = jnp.where(on, jnp.exp2(s_new - m_new), 0.0)
        l = alpha * l_ref[...] + e_new
        acc = alpha * acc_ref[...] + e_new.astype(BF16).astype(F32) * knew[:, KV_WIDTH:]
        o_sel = acc / jnp.maximum(l, 1e-30)
        g = gate_ref[0]
        o_ref[0] = g[:, 0:1] * ocmp_ref[0] + g[:, 1:2] * o_sel + g[:, 2:3] * owin_ref[0]


def _sample_sel(cache_t, page_table, layer, qz, sel16, knew, ocmp, owin, gz, steps_per_seq=2):
    db, n_pages = page_table.shape
    page = cache_t.shape[3]
    pps = n_pages // steps_per_seq
    keys = pps * page
    blocks = keys // SEL_BLOCK
    b2k = jnp.asarray(_block_to_key(blocks, keys), dtype=BF16)
    blk3 = lambda a: pl.BlockSpec((1,) + a.shape[1:], lambda i, p, pt: (i, 0, 0))
    kern = functools.partial(_sample_sel_kernel, layer, pps, steps_per_seq)
    return pl.pallas_call(
        kern,
        grid_spec=pltpu.PrefetchScalarGridSpec(
            num_scalar_prefetch=1, grid=(db, steps_per_seq),
            in_specs=[pl.BlockSpec(memory_space=pl.ANY), blk3(qz),
                      pl.BlockSpec((1, SROWS, blocks), lambda i, p, pt: (i, 0, p)),
                      pl.BlockSpec((1, SROWS, LANES), lambda i, p, pt: (i, 0, steps_per_seq * blocks // LANES)),
                      blk3(knew),
                      pl.BlockSpec(b2k.shape, lambda i, p, pt: (0, 0)), blk3(ocmp), blk3(owin), blk3(gz)],
            out_specs=pl.BlockSpec((1, SROWS, KV_WIDTH), lambda i, p, pt: (i, 0, 0)),
            scratch_shapes=[pltpu.VMEM((2, CMP_LANES, keys), F32), pltpu.SemaphoreType.DMA((2,)),
                            pltpu.VMEM((SROWS, 1), F32), pltpu.VMEM((SROWS, 1), F32),
                            pltpu.VMEM((SROWS, KV_WIDTH), F32)]),
        out_shape=jax.ShapeDtypeStruct((db, SROWS, KV_WIDTH), F32),
        compiler_params=_params("arbitrary", "arbitrary"), name="sample_sel",
    )(page_table, cache_t, qz, sel16, sel16, knew, b2k, ocmp, owin, gz)


def _bdot(dot, a, b):
    return dot(a.astype(BF16), b.astype(BF16))


def _head_sum(x, hm):
    xh = x.astype(BF16)
    xl = (x - xh.astype(F32)).astype(BF16)
    return _dot(xh, hm) + _dot(xl, hm)


def _softplus(z):
    return jnp.maximum(z, 0.0) + jnp.log(1.0 + jnp.exp(-jnp.abs(z)))


CHUNK_GROUP = 8


def _rwkv_kernel(chunk, n_valid, pr_ref, prev_ref, s0_ref, mu_ref, w0_ref, wup_ref, a0_ref, aup_ref, gup_ref,
                 kk_ref, ka_ref, rk_ref, lnw_ref, lnb_ref, hm_ref, tri_ref, out_ref, sout_ref,
                 carry_ref, state_ref, r_ref, k_ref, v_ref, lw_ref, a_ref, b_ref, rh_ref, y_ref, m_ref, n_ref, et_ref):
    ti = pl.program_id(1)
    tt = pr_ref.shape[1]
    w_ = RWKV_WIDTH
    hd = RWKV_HEAD

    @pl.when(ti == 0)
    def _():
        carry_ref[...] = prev_ref[0]
        state_ref[...] = s0_ref[0]

    p = pr_ref[0]
    row = lax.broadcasted_iota(jnp.int32, (tt, 1), 0)
    shifted = jnp.where(row == 0, carry_ref[...], pltpu.roll(p, 1, 0))
    carry_ref[...] = p[tt - 1:tt]
    xs = p + (shifted - p) * mu_ref[...]
    r, k, v = xs[:, :w_], xs[:, w_:2 * w_], xs[:, 2 * w_:3 * w_]
    wa = xs[:, 3 * w_:3 * w_ + DECAY_LORA + AAA_LORA]
    gl = xs[:, 3 * w_ + DECAY_LORA + AAA_LORA:]
    hm = hm_ref[...]
    w = -_softplus(-(w0_ref[...] + _dot(jnp.tanh(wa).astype(BF16), wup_ref[...]))) - 0.5
    lw = -jnp.exp(w)
    a = jax.nn.sigmoid(a0_ref[...] + _dot(wa.astype(BF16), aup_ref[...]))
    g = _dot(jax.nn.sigmoid(gl).astype(BF16), gup_ref[...])
    kk = k * kk_ref[...]
    kk = kk / jnp.maximum(jnp.sqrt(_head_sum(kk * kk, hm)), 1e-12)
    k = k * (1.0 + (a - 1.0) * ka_ref[...])
    av, bv = -kk, kk * a
    if n_valid < tt:
        keep = row < n_valid
        lw, k, v, av, bv = (jnp.where(keep, t, 0.0) for t in (lw, k, v, av, bv))
    r_ref[...], k_ref[...], v_ref[...], lw_ref[...], a_ref[...], b_ref[...] = r, k, v, lw, av, bv

    ci = lax.broadcasted_iota(jnp.int32, (chunk, chunk), 0)
    cj = lax.broadcasted_iota(jnp.int32, (chunk, chunk), 1)
    strict, incl = ci > cj, ci >= cj
    dot = functools.partial(_bdot, _dot)
    dot_nt = functools.partial(_bdot, _dot_nt)
    dot_tn = functools.partial(_bdot, _dot_tn)
    group = min(CHUNK_GROUP, tt // chunk)

    def prepare(cg, carry):
        jobs, pre = [], []
        for ci in range(group):
            c = cg * group + ci
            rows = pl.ds(pl.multiple_of(c * chunk, chunk), chunk)
            lwc = lw_ref[rows, :]
            cs = _dot(tri_ref[...], lwc, precision=HIGHEST)
            e_in, e_out = jnp.exp(cs), jnp.exp(-cs)
            total = cs[chunk - 1:chunk]
            e_end = jnp.exp(total - cs)
            kc, bc = k_ref[rows, :], b_ref[rows, :]
            pre.append(dict(c=c, rows=rows, rt=r_ref[rows, :] * e_in, at=a_ref[rows, :] * jnp.exp(cs - lwc),
                            kt=kc * e_out, bt=bc * e_out, kh=kc * e_end, bh=bc * e_end, vc=v_ref[rows, :]))
            et_ref[c] = jnp.broadcast_to(jnp.exp(total), et_ref.shape[1:])
            jobs += [(ci, h, slice(h * hd, (h + 1) * hd)) for h in range(RWKV_HEADS)]
        nj = range(len(jobs))
        part = lambda name: [pre[ci][name][:, sl] for ci, _, sl in jobs]
        at, rt, bt, kt, bh, kh, vh = (part(n_) for n_ in ("at", "rt", "bt", "kt", "bh", "kh", "vc"))
        gm = [dot_nt(jnp.concatenate([at[j], rt[j]], axis=0), jnp.concatenate([bt[j], kt[j]], axis=0))
              for j in nj]
        lab = [jnp.where(strict, g_[:chunk, :chunk], 0.0) for g_ in gm]
        lak = [jnp.where(strict, g_[:chunk, chunk:], 0.0) for g_ in gm]
        mrb = [jnp.where(incl, g_[chunk:, :chunk], 0.0) for g_ in gm]
        mrk = [jnp.where(incl, g_[chunk:, chunk:], 0.0) for g_ in gm]
        wm = [jnp.concatenate([at[j], dot(lak[j], vh[j])], axis=1) for j in nj]
        x, n = lab, 1
        while 2 * n < chunk:
            xw = [dot(x[j], jnp.concatenate([wm[j], x[j]], axis=1)) for j in nj]
            x = [t[:, 2 * hd:] for t in xw]
            wm = [wm[j] + xw[j][:, :2 * hd] for j in nj]
            n *= 2
        wm = [wm[j] + dot(x[j], wm[j]) for j in nj]
        mw = [dot(mrb[j], wm[j]) for j in nj]
        y0 = [mw[j][:, hd:] + dot(mrk[j], vh[j]) for j in nj]
        wb = [dot_tn(wm[j], bh[j]) for j in nj]
        vk = [dot_tn(vh[j], kh[j]) for j in nj]
        for j, (ci, h, sl) in enumerate(jobs):
            c, rows = pre[ci]["c"], pre[ci]["rows"]
            rh_ref[rows, sl] = rt[j] + mw[j][:, :hd]
            y_ref[rows, sl] = y0[j]
            m_ref[c, h] = wb[j][:hd]
            n_ref[c, h] = wb[j][hd:] + vk[j]
        return carry

    def advance(c, carry):
        rows = pl.ds(pl.multiple_of(c * chunk, chunk), chunk)
        e_tot = et_ref[c]
        sls = [slice(h * hd, (h + 1) * hd) for h in range(RWKV_HEADS)]
        s0 = [state_ref[h] for h in range(RWKV_HEADS)]
        ys = [dot_nt(rh_ref[rows, sl], s0[h]) for h, sl in enumerate(sls)]
        sm = [dot(s0[h], m_ref[c, h]) for h in range(RWKV_HEADS)]
        for h, sl in enumerate(sls):
            y_ref[rows, sl] += ys[h]
            state_ref[h] = s0[h] * e_tot[0:1, sl] + sm[h] + n_ref[c, h]
        return carry

    lax.fori_loop(0, tt // chunk // group, prepare, 0)
    lax.fori_loop(0, tt // chunk, advance, 0, unroll=True)

    y = y_ref[...]
    inv_n = 1.0 / hd
    mean = _head_sum(y, hm) * inv_n
    d = y - mean
    var = _head_sum(d * d, hm) * inv_n
    y = d * lax.rsqrt(var + LNX_EPS) * lnw_ref[...] + lnb_ref[...]
    y = y + _head_sum(r_ref[...] * k_ref[...] * rk_ref[...], hm) * v_ref[...]
    out_ref[0] = (y * g).astype(out_ref.dtype)
    sout_ref[0] = state_ref[...]


def _rwkv(pr3, prev, s0, rw, chunk, tt, n_valid):
    b, t, _ = pr3.shape
    hm = jnp.asarray(np.kron(np.eye(RWKV_HEADS), np.ones((RWKV_HEAD, RWKV_HEAD))), dtype=BF16)
    tri = jnp.asarray(np.tril(np.ones((chunk, chunk))), dtype=F32)
    vec = lambda a: pl.BlockSpec(a.shape, lambda i, j: (0, 0))
    prev3 = prev[:, None, :]
    names = ("mu", "w0", "wup", "a0", "aup", "gup", "kk", "ka", "rk", "lnw", "lnb")
    kern = functools.partial(_rwkv_kernel, chunk, n_valid)
    per_chunk = (tt // chunk, RWKV_HEADS, RWKV_HEAD, RWKV_HEAD)
    return pl.pallas_call(
        kern, grid=(b, t // tt),
        in_specs=[pl.BlockSpec((1, tt, RWKV_PROJ), lambda i, j: (i, j, 0)),
                  pl.BlockSpec((1, 1, RWKV_PROJ), lambda i, j: (i, 0, 0)),
                  pl.BlockSpec((1, RWKV_HEADS, RWKV_HEAD, RWKV_HEAD), lambda i, j: (i, 0, 0, 0))] +
                 [vec(rw[n]) for n in names] + [vec(hm), vec(tri)],
        out_specs=[pl.BlockSpec((1, tt, RWKV_WIDTH), lambda i, j: (i, j, 0)),
                   pl.BlockSpec((1, RWKV_HEADS, RWKV_HEAD, RWKV_HEAD), lambda i, j: (i, 0, 0, 0))],
        out_shape=(jax.ShapeDtypeStruct((b, t, RWKV_WIDTH), BF16),
                   jax.ShapeDtypeStruct((b, RWKV_HEADS, RWKV_HEAD, RWKV_HEAD), F32)),
        scratch_shapes=[pltpu.VMEM((1, RWKV_PROJ), F32), pltpu.VMEM((RWKV_HEADS, RWKV_HEAD, RWKV_HEAD), F32)] +
                       [pltpu.VMEM((tt, RWKV_WIDTH), F32) for _ in range(8)] +
                       [pltpu.VMEM(per_chunk, F32), pltpu.VMEM(per_chunk, F32),
                        pltpu.VMEM((tt // chunk, 8, RWKV_WIDTH), F32)],
        compiler_params=_params("arbitrary", "arbitrary"), name="rwkv",
    )(pr3, prev3, s0, *[rw[n] for n in names], hm, tri)


def _ffn_head(x_ref, nsa_ref, rwkv_ref, wo_ref, gpost_ref, gpre_ref, x1_ref, h2_ref, acc_ref):
    n = nsa_ref.shape[1]
    mixed = _dot(nsa_ref[...], wo_ref[:n, :]) + _dot(rwkv_ref[...], wo_ref[n:, :])
    x1 = x_ref[...] + _rms(mixed, gpost_ref[...])
    x1_ref[...] = x1
    h2_ref[...] = _rms(x1, gpre_ref[...]).astype(BF16)
    acc_ref[...] = jnp.zeros(acc_ref.shape, F32)


def _ffn_tail(gate2, gate1, gate, val, cw_ref, cb_ref, wd_ref, acc_ref):
    conv = cb_ref[...] + (gate2 * cw_ref[0:1, :] + gate1 * cw_ref[1:2, :] + gate * cw_ref[2:3, :])
    acc_ref[...] += _dot((jax.nn.gelu(conv) * val).astype(BF16), wd_ref[...])


FFN_COLS = 256


def _ffn_seq_kernel(tiles_per_seq, x_ref, nsa_ref, rwkv_ref, wo_ref, gpost_ref, gpre_ref, gfpost_ref, wg_ref, wv_ref, cw_ref,
                    cb_ref, wd_ref, out_ref, tail_ref, x1_ref, h2_ref, acc_ref, carry_ref):
    i, j = pl.program_id(0), pl.program_id(1)
    tm = x_ref.shape[0]

    @pl.when(j == 0)
    def _():
        _ffn_head(x_ref, nsa_ref, rwkv_ref, wo_ref, gpost_ref, gpre_ref, x1_ref, h2_ref, acc_ref)

    @pl.when(i % tiles_per_seq == 0)
    def _():
        carry_ref[j] = jnp.zeros(carry_ref.shape[1:], F32)

    h2 = h2_ref[...]
    row = lax.broadcasted_iota(jnp.int32, (tm, 1), 0)
    tf = wg_ref.shape[1]
    cols = [slice(c0, min(c0 + FFN_COLS, tf)) for c0 in range(0, tf, FFN_COLS)]
    up = lambda c: (_dot(h2, wg_ref[:, c]), _dot(h2, wv_ref[:, c]))
    nxt = up(cols[0])
    for k, c in enumerate(cols):
        gate, val = nxt
        if k + 1 < len(cols):
            nxt = up(cols[k + 1])
        prev = carry_ref[j, :, c]
        gate1 = jnp.where(row == 0, prev[7:8], pltpu.roll(gate, 1, 0))
        gate2 = jnp.where(row == 0, prev[6:7], jnp.where(row == 1, prev[7:8], pltpu.roll(gate, 2, 0)))
        carry_ref[j, :, c] = gate[tm - 8:tm]
        tail_ref[0, :, c] = gate[tm - 8:tm]
        conv = cb_ref[:, c] + (gate2 * cw_ref[0:1, c] + gate1 * cw_ref[1:2, c] + gate * cw_ref[2:3, c])
        acc_ref[...] += _dot((jax.nn.gelu(conv) * val).astype(BF16), wd_ref[c, :])

    @pl.when(j == pl.num_programs(1) - 1)
    def _():
        out_ref[...] = x1_ref[...] + _rms(acc_ref[...], gfpost_ref[...])


def _ffn_step_kernel(x_ref, nsa_ref, rwkv_ref, wo_ref, gpost_ref, gpre_ref, gfpost_ref, wg_ref, wv_ref, cw_ref, cb_ref,
                     wd_ref, prev_ref, out_ref, gate_ref, x1_ref, h2_ref, acc_ref):
    j = pl.program_id(0)

    @pl.when(j == 0)
    def _():
        _ffn_head(x_ref, nsa_ref, rwkv_ref, wo_ref, gpost_ref, gpre_ref, x1_ref, h2_ref, acc_ref)

    h2 = h2_ref[...]
    gate, val = _dot(h2, wg_ref[...]), _dot(h2, wv_ref[...])
    gate_ref[...] = gate
    _ffn_tail(prev_ref[0], prev_ref[1], gate, val, cw_ref, cb_ref, wd_ref, acc_ref)

    @pl.when(j == pl.num_programs(0) - 1)
    def _():
        out_ref[...] = x1_ref[...] + _rms(acc_ref[...], gfpost_ref[...])


def _ffn_seq(x2, nsa2, rwkv2, fw, seq_len, tm, tf):
    m, d = x2.shape
    d_ff = fw["wd"].shape[0]
    nf = d_ff // tf
    row = lambda i, j: (i, 0)
    full = lambda i, j: (0, 0)
    tiles_per_seq = seq_len // tm
    scratch = [pltpu.VMEM((tm, d), F32), pltpu.VMEM((tm, d), BF16), pltpu.VMEM((tm, d), F32),
               pltpu.VMEM((nf, 8, tf), F32)]
    return pl.pallas_call(
        functools.partial(_ffn_seq_kernel, tiles_per_seq), grid=(m // tm, nf),
        in_specs=[pl.BlockSpec((tm, d), row), pl.BlockSpec((tm, nsa2.shape[1]), row),
                  pl.BlockSpec((tm, rwkv2.shape[1]), row), pl.BlockSpec((d, d), full),
                  pl.BlockSpec((1, d), full), pl.BlockSpec((1, d), full), pl.BlockSpec((1, d), full),
                  pl.BlockSpec((d, tf), lambda i, j: (0, j)), pl.BlockSpec((d, tf), lambda i, j: (0, nf + j)),
                  pl.BlockSpec((CONV_WIDTH, tf), lambda i, j: (0, j)), pl.BlockSpec((1, tf), lambda i, j: (0, j)),
                  pl.BlockSpec((tf, d), lambda i, j: (j, 0))],
        out_specs=[pl.BlockSpec((tm, d), row), pl.BlockSpec((1, 8, tf), lambda i, j: (i, 0, j))],
        out_shape=(jax.ShapeDtypeStruct((m, d), F32), jax.ShapeDtypeStruct((m // tm, 8, d_ff), F32)),
        scratch_shapes=scratch, compiler_params=_params("arbitrary", "arbitrary"), name="ffn_seq",
    )(x2, nsa2, rwkv2, fw["wo"], fw["gpost"], fw["gpre"], fw["gfpost"], fw["wup"], fw["wup"], fw["cw"], fw["cb"],
      fw["wd"])


def _ffn_step(x2, nsa2, rwkv2, fw, prev, tf):
    m, d = x2.shape
    d_ff = fw["wd"].shape[0]
    nf = d_ff // tf
    full = lambda j: (0, 0)
    scratch = [pltpu.VMEM((m, d), F32), pltpu.VMEM((m, d), BF16), pltpu.VMEM((m, d), F32)]
    return pl.pallas_call(
        _ffn_step_kernel, grid=(nf,),
        in_specs=[pl.BlockSpec((m, d), full), pl.BlockSpec(nsa2.shape, full), pl.BlockSpec(rwkv2.shape, full),
                  pl.BlockSpec((d, d), full),
                  pl.BlockSpec((1, d), full), pl.BlockSpec((1, d), full), pl.BlockSpec((1, d), full),
                  pl.BlockSpec((d, tf), lambda j: (0, j)), pl.BlockSpec((d, tf), lambda j: (0, nf + j)),
                  pl.BlockSpec((CONV_WIDTH, tf), lambda j: (0, j)), pl.BlockSpec((1, tf), lambda j: (0, j)),
                  pl.BlockSpec((tf, d), lambda j: (j, 0)),
                  pl.BlockSpec((CONV_WIDTH - 1, m, tf), lambda j: (0, 0, j))],
        out_specs=[pl.BlockSpec((m, d), full), pl.BlockSpec((m, tf), lambda j: (0, j))],
        out_shape=(jax.ShapeDtypeStruct((m, d), F32), jax.ShapeDtypeStruct((m, d_ff), F32)),
        scratch_shapes=scratch, compiler_params=_params("arbitrary"), name="ffn_step",
    )(x2, nsa2, rwkv2, fw["wo"], fw["gpost"], fw["gpre"], fw["gfpost"], fw["wup"], fw["wup"], fw["cw"], fw["cb"],
      fw["wd"], prev)


def _layer_weights(l, p):
    w_in = p["w_in"][l]
    o = NSA_WIDTH + 6 * KV_WIDTH
    per_group = N_BRANCH * HEADS_PER_KV
    wg = jnp.pad(w_in[:, o:NSA_PROJ].reshape(-1, KV_HEADS, per_group), ((0, 0), (0, 0), (0, LANES - per_group)))
    wg = wg.reshape(-1, KV_HEADS * LANES)
    row = lambda a: a[l][None, :].astype(F32)
    zpad = jnp.zeros((AAA_LORA, RWKV_WIDTH), F32)
    rw = {"mu": row(p["rwkv_mu"]), "w0": row(p["rwkv_w0"]),
          "wup": jnp.concatenate([p["rwkv_w_up"][l], zpad], axis=0).astype(BF16),
          "a0": row(p["rwkv_a0"]), "aup": jnp.concatenate([zpad, p["rwkv_a_up"][l]], axis=0).astype(BF16),
          "gup": p["rwkv_g_up"][l].astype(BF16), "kk": row(p["rwkv_k_k"]), "ka": row(p["rwkv_k_a"]),
          "rk": p["rwkv_r_k"][l].reshape(1, RWKV_WIDTH), "lnw": row(p["rwkv_lnx_w"]), "lnb": row(p["rwkv_lnx_b"])}
    fw = {"wo": p["w_out"][l].astype(BF16), "gpost": row(p["norm_mix_post"]), "gpre": row(p["norm_ffn_pre"]),
          "gfpost": row(p["norm_ffn_post"]), "wup": p["ffn_w_up"][l].astype(BF16), "cw": p["ffn_conv_w"][l],
          "cb": row(p["ffn_conv_b"]), "wd": p["ffn_w_down"][l].astype(BF16)}
    return {"g_in": row(p["norm_mix_pre"]), "wq": w_in[:, :NSA_WIDTH].astype(BF16),
            "wkv": w_in[:, NSA_WIDTH:o].astype(BF16), "wg": wg.astype(BF16), "wr": w_in[:, NSA_PROJ:].astype(BF16),
            "cw": _compress_weights(p["cmp_pe"][l], p["cmp_w1"][l], p["cmp_w2"][l]), "rw": rw, "fw": fw}


def _prompt_layer(x, lw, tabs):
    b, t, d = x.shape
    m = b * t
    x2 = x.reshape(m, d)
    own_block = jnp.asarray(np.concatenate([np.zeros((t, HEAD_DIM)), NEG * _block_to_key(HEAD_DIM, t).T], axis=1),
                            dtype=BF16)
    kvw, gates, pr, qh, ksa, kw, va = _norm_proj(x2, lw["g_in"], lw["wq"], lw["wkv"], lw["wg"], lw["wr"], tabs,
                                                 tm=256, heads_of=(b, t, own_block))
    kvw3 = kvw.reshape(b, t, 6 * KV_WIDTH)
    kvc = _compress_prompt(kvw3, lw["cw"])
    cvh = kvc.astype(BF16).reshape(b, -1, 2, KV_HEADS, HEAD_DIM).transpose(0, 2, 3, 1, 4)
    o_nsa = _nsa_prompt(qh, ksa, kw, va, cvh, gates.reshape(b, t, KV_HEADS * LANES)).reshape(m, NSA_WIDTH)
    s0 = jnp.zeros((b, RWKV_HEADS, RWKV_HEAD, RWKV_HEAD), F32)
    pr3 = pr.reshape(b, t, RWKV_PROJ)
    o_rwkv, wkv = _rwkv(pr3, jnp.zeros((b, RWKV_PROJ), F32), s0, lw["rw"], chunk=64, tt=min(t, 512), n_valid=t)
    d_ff = lw["fw"]["wd"].shape[0]
    tm = min(t, 512)
    y, tail = _ffn_seq(x2, o_nsa, o_rwkv.reshape(m, RWKV_WIDTH), lw["fw"], t, tm=tm, tf=d_ff // 2)
    tail = tail[t // tm - 1::t // tm]
    kv_rows = kvw3[:, :, :4 * KV_WIDTH].reshape(b, t, 4, KV_HEADS, HEAD_DIM)
    wn = min(WINDOW, t)
    win = kvw3[:, t - wn:, 4 * KV_WIDTH:].reshape(b, wn, 2, KV_HEADS, HEAD_DIM)
    return y.reshape(b, t, d), kv_rows, win, wkv, pr3[:, -1], tail[:, 8 - (CONV_WIDTH - 1):]


def _sample_layer(x, lw, tabs, l, cache_t, page_table, st_win, st_wkv, st_shift, st_conv):
    db, tn, d = x.shape
    past_len = page_table.shape[1] * cache_t.shape[3]
    x2 = x.reshape(db, d)
    q, kvw, gates, pr = _norm_proj(x2, lw["g_in"], lw["wq"], lw["wkv"], lw["wg"], lw["wr"], tabs, tm=db)
    qs = (q * (ATTN_SCALE * LOG2E)).astype(BF16).reshape(db, KV_HEADS, HEADS_PER_KV, HEAD_DIM)
    eye = jnp.eye(KV_HEADS, dtype=BF16)
    qz = jnp.einsum('bghd,gj->bghjd', qs, eye).reshape(db, KV_HEADS, HEADS_PER_KV, KV_WIDTH)
    qz = jnp.pad(qz, ((0, 0), (0, 0), (0, 8 - HEADS_PER_KV), (0, 0))).reshape(db, SROWS, KV_WIDTH)
    kvc = _compress_sample(cache_t, page_table, l, lw["cw"])
    n_cmp = past_len // CMP_STRIDE - 1
    n_sel = -(-(past_len + tn) // SEL_BLOCK)
    n_sel_pad = -(-n_sel // LANES) * LANES
    c2s = jnp.asarray(_cmp_to_sel(kvc.shape[1], n_cmp, n_sel_pad, n_sel))
    wb = st_win.shape[1]
    win = st_win.transpose(0, 2, 3, 4, 1).reshape(db, 2 * KV_WIDTH, wb)
    wnew = kvw[:, None, 4 * KV_WIDTH:]
    ocmp, owin, imp = _sample_cmp_win(past_len, n_cmp, qz, kvc, win, wnew, c2s)
    sel = _select(imp.reshape(db * 8, n_sel_pad), n_sel, past_len // SEL_BLOCK).reshape(db, 8, n_sel_pad)
    sel16 = jnp.broadcast_to(sel[:, :KV_HEADS, None, :], (db, KV_HEADS, 8, n_sel_pad)).reshape(db, SROWS, n_sel_pad)
    gz = gates.reshape(db, KV_HEADS, LANES)[:, :, :N_BRANCH * HEADS_PER_KV].reshape(db, KV_HEADS, HEADS_PER_KV, N_BRANCH)
    gz = jnp.pad(gz, ((0, 0), (0, 0), (0, 8 - HEADS_PER_KV), (0, 0))).reshape(db, SROWS, N_BRANCH)
    knew = kvw[:, None, 2 * KV_WIDTH:4 * KV_WIDTH]
    o16 = _sample_sel(cache_t, page_table, l, qz, sel16, knew, ocmp, owin, gz)
    o4 = o16.reshape(db, KV_HEADS, 8, KV_HEADS, HEAD_DIM)[:, :, :HEADS_PER_KV]
    o_nsa = jnp.stack([o4[:, g, :, g] for g in range(KV_HEADS)], axis=1).reshape(db, NSA_WIDTH)
    pad_t = 8
    pr3 = jnp.pad(pr[:, None, :], ((0, 0), (0, pad_t - tn), (0, 0)))
    o_rwkv, wkv = _rwkv(pr3, st_shift, st_wkv, lw["rw"], chunk=pad_t, tt=pad_t, n_valid=tn)
    prev = st_conv.transpose(1, 0, 2)
    d_ff = lw["fw"]["wd"].shape[0]
    y, gate = _ffn_step(x2, o_nsa.astype(BF16), o_rwkv[:, 0], lw["fw"], prev, tf=d_ff // 2)
    kv_rows = kvw[:, :4 * KV_WIDTH].reshape(db, tn, 4, KV_HEADS, HEAD_DIM)
    win_new = jnp.concatenate([st_win[:, tn:], kvw[:, 4 * KV_WIDTH:].reshape(db, tn, 2, KV_HEADS, HEAD_DIM)], axis=1)
    conv_new = jnp.concatenate([st_conv[:, 1:], gate[:, None, :]], axis=1)
    return y.reshape(db, tn, d), kv_rows, win_new, wkv, pr, conv_new


def kernel(x_prompt, x_sample, cache_kv, page_table, state_win, state_wkv, state_shift, state_conv, norm_mix_pre, norm_mix_post, norm_ffn_pre, norm_ffn_post, w_in, w_out, cmp_pe, cmp_w1, cmp_w2, rwkv_mu, rwkv_w0, rwkv_w_up, rwkv_a0, rwkv_a_up, rwkv_g_up, rwkv_k_k, rwkv_k_a, rwkv_r_k, rwkv_lnx_w, rwkv_lnx_b, ffn_w_up, ffn_conv_w, ffn_conv_b, ffn_w_down):
    p = dict(norm_mix_pre=norm_mix_pre, norm_mix_post=norm_mix_post, norm_ffn_pre=norm_ffn_pre,
             norm_ffn_post=norm_ffn_post, w_in=w_in, w_out=w_out, cmp_pe=cmp_pe, cmp_w1=cmp_w1, cmp_w2=cmp_w2,
             rwkv_mu=rwkv_mu, rwkv_w0=rwkv_w0, rwkv_w_up=rwkv_w_up, rwkv_a0=rwkv_a0, rwkv_a_up=rwkv_a_up,
             rwkv_g_up=rwkv_g_up, rwkv_k_k=rwkv_k_k, rwkv_k_a=rwkv_k_a, rwkv_r_k=rwkv_r_k,
             rwkv_lnx_w=rwkv_lnx_w, rwkv_lnx_b=rwkv_lnx_b, ffn_w_up=ffn_w_up, ffn_conv_w=ffn_conv_w,
             ffn_conv_b=ffn_conv_b, ffn_w_down=ffn_w_down)
    depth = w_in.shape[0]
    b, t, _ = x_prompt.shape
    db, tn, _ = x_sample.shape
    n_pool, page = cache_kv.shape[1], cache_kv.shape[2]
    past_len = page_table.shape[1] * page
    cache_t = cache_kv.transpose(0, 1, 3, 4, 5, 2).reshape(depth, n_pool, 2 * CMP_LANES, page)
    tabs_p = _rope_tables(jnp.arange(t, dtype=jnp.int32))
    tabs_s = _rope_tables(jnp.full((db,), past_len, jnp.int32))
    xp, xs = x_prompt, x_sample
    outs_p, outs_s = [], []
    for l in range(depth):
        lw = _layer_weights(l, p)
        xp, *st_p = _prompt_layer(xp, lw, tabs_p)
        xs, *st_s = _sample_layer(xs, lw, tabs_s, l, cache_t, page_table, state_win[l], state_wkv[l],
                                  state_shift[l], state_conv[l])
        outs_p.append(st_p)
        outs_s.append(st_s)
    stack = lambda outs, i: jnp.stack([o[i] for o in outs])
    return (xp, xs, stack(outs_p, 0), stack(outs_s, 0), stack(outs_p, 1), stack(outs_s, 1),
            stack(outs_p, 2), stack(outs_s, 2), stack(outs_p, 3), stack(outs_s, 3),
            stack(outs_p, 4), stack(outs_s, 4))
```

```python
import functools

import jax
import jax.numpy as jnp
import numpy as np
from jax import lax
from jax.experimental import pallas as pl
from jax.experimental.pallas import tpu as pltpu

F32, BF16 = jnp.float32, jnp.bfloat16
HIGHEST = lax.Precision.HIGHEST

LANES = 128
VMEM_LIMIT = 56 * 1024 * 1024

HEAD_DIM = 64
NSA_HEADS = 8
KV_HEADS = 2
HEADS_PER_KV = NSA_HEADS // KV_HEADS
NSA_WIDTH = NSA_HEADS * HEAD_DIM
KV_WIDTH = KV_HEADS * HEAD_DIM
N_BRANCH = 3
CMP_STRIDE = 16
CMP_BLOCK = 2 * CMP_STRIDE
SEL_BLOCK = 64
SEL_TOPN = 16
WINDOW = 512
ROPE_THETA = 500000.0
ROPE_DIM = HEAD_DIM // 4
RWKV_HEAD = 64
RWKV_HEADS = 8
RWKV_WIDTH = RWKV_HEADS * RWKV_HEAD
DECAY_LORA = 64
AAA_LORA = 64
GATE_LORA = 128
RWKV_PROJ = 3 * RWKV_WIDTH + DECAY_LORA + AAA_LORA + GATE_LORA
NSA_PROJ = NSA_WIDTH + 6 * KV_WIDTH + N_BRANCH * NSA_HEADS
CONV_WIDTH = 3
NORM_EPS = 1e-6
LNX_EPS = 64e-5
NEG = -1e30
FORCED_SCORE = 1e9
ATTN_SCALE = HEAD_DIM ** -0.5
LOG2E = 1.4426950408889634


def _dot(a, b, precision=None):
    return jnp.dot(a, b, preferred_element_type=F32, precision=precision)


def _dot_nt(a, b, precision=None):
    return lax.dot_general(a, b, (((1,), (1,)), ((), ())), preferred_element_type=F32, precision=precision)


def _dot_tn(a, b, precision=None):
    return lax.dot_general(a, b, (((0,), (0,)), ((), ())), preferred_element_type=F32, precision=precision)


def _rms(x, g):
    return x * lax.rsqrt(jnp.mean(x * x, axis=-1, keepdims=True) + NORM_EPS) * g


def _masked_softmax(s, mask):
    s = jnp.where(mask, s, NEG)
    m = jnp.max(s, axis=-1, keepdims=True)
    e = jnp.where(mask, jnp.exp2(s - m), 0.0)
    return e / jnp.maximum(jnp.sum(e, axis=-1, keepdims=True), 1e-30)


def _params(*sem):
    return pltpu.CompilerParams(dimension_semantics=sem, vmem_limit_bytes=VMEM_LIMIT)


def _rope(v, cos, sa, sb):
    return v * cos + pltpu.roll(v, LANES - ROPE_DIM // 2, 1) * sa + pltpu.roll(v, ROPE_DIM // 2, 1) * sb


def _project(x_ref, g_ref, wq_ref, wkv_ref, wg_ref, wr_ref, tabs, kvw_ref, gate_ref, pr_ref):
    h = _rms(x_ref[...], g_ref[...]).astype(BF16)
    q = _dot(h, wq_ref[...])
    q = jnp.concatenate([_rope(q[:, j * LANES:(j + 1) * LANES], *tabs) for j in range(NSA_WIDTH // LANES)], axis=1)
    kv = _dot(h, wkv_ref[...])
    blocks = []
    for j in range(6 * KV_WIDTH // LANES):
        blk = kv[:, j * LANES:(j + 1) * LANES]
        blocks.append(_rope(blk, *tabs) if j % 2 == 0 else blk)
        kvw_ref[:, j * LANES:(j + 1) * LANES] = blocks[j]
    gate_ref[...] = jax.nn.sigmoid(_dot(h, wg_ref[...]))
    pr_ref[...] = _dot(h, wr_ref[...])
    return q, blocks


def _norm_proj_kernel(x_ref, g_ref, wq_ref, wkv_ref, wg_ref, wr_ref, cos_ref, sa_ref, sb_ref,
                      q_ref, kvw_ref, gate_ref, pr_ref):
    tabs = (cos_ref[...], sa_ref[...], sb_ref[...])
    q_ref[...], _ = _project(x_ref, g_ref, wq_ref, wkv_ref, wg_ref, wr_ref, tabs, kvw_ref, gate_ref, pr_ref)


def _norm_proj_heads_kernel(x_ref, g_ref, wq_ref, wkv_ref, wg_ref, wr_ref, cos_ref, sa_ref, sb_ref, own_ref,
                            kvw_ref, gate_ref, pr_ref, qh_ref, ksa_ref, kwh_ref, va_ref):
    tabs = (cos_ref[...], sa_ref[...], sb_ref[...])
    q, kv = _project(x_ref, g_ref, wq_ref, wkv_ref, wg_ref, wr_ref, tabs, kvw_ref, gate_ref, pr_ref)
    tm = q.shape[0]
    lane = lax.broadcasted_iota(jnp.int32, (tm, LANES), 1)
    low = lane < HEAD_DIM
    ones_lane = jnp.where(lane == HEAD_DIM, 1.0, 0.0)
    halves = lambda blk: (blk, pltpu.roll(blk, HEAD_DIM, 1))
    for j in range(NSA_WIDTH // LANES):
        for half, x in enumerate(halves(q[:, j * LANES:(j + 1) * LANES] * (ATTN_SCALE * LOG2E))):
            qh_ref[2 * j + half] = jnp.where(low, x, 0.0).astype(BF16)
    own = own_ref[...].astype(F32)
    for g, x in enumerate(halves(kv[2])):
        ksa_ref[g] = jnp.where(low, x, own).astype(BF16)
    for g, x in enumerate(halves(kv[4])):
        kwh_ref[g] = x[:, :HEAD_DIM].astype(BF16)
    for c, col in enumerate((3, 5)):
        for g, x in enumerate(halves(kv[col])):
            va_ref[c, g] = jnp.where(low, x, ones_lane).astype(BF16)


def _norm_proj(x2, g, wq, wkv, wg, wr, tabs, tm, heads_of=None):
    m, d = x2.shape
    cos, sa, sb = tabs
    tab_blocks = cos.shape[0] // tm
    row = lambda i: (i, 0)
    full = lambda i: (0, 0)
    tab = lambda i: (i % tab_blocks, 0)
    in_specs = [pl.BlockSpec((tm, d), row), pl.BlockSpec((1, d), full),
                pl.BlockSpec(wq.shape, full), pl.BlockSpec(wkv.shape, full),
                pl.BlockSpec(wg.shape, full), pl.BlockSpec(wr.shape, full),
                pl.BlockSpec((tm, LANES), tab), pl.BlockSpec((tm, LANES), tab), pl.BlockSpec((tm, LANES), tab)]
    rows_out = [(6 * KV_WIDTH, F32), (KV_HEADS * LANES, F32), (RWKV_PROJ, F32)]
    if heads_of is None:
        rows_out = [(NSA_WIDTH, F32)] + rows_out
        return pl.pallas_call(
            _norm_proj_kernel, grid=(m // tm,), in_specs=in_specs,
            out_specs=[pl.BlockSpec((tm, w), row) for w, _ in rows_out],
            out_shape=[jax.ShapeDtypeStruct((m, w), dt) for w, dt in rows_out],
            compiler_params=_params("arbitrary"), name="norm_proj",
        )(x2, g, wq, wkv, wg, wr, cos, sa, sb)
    b, t, own = heads_of
    head_shapes = [(NSA_HEADS, LANES), (KV_HEADS, LANES), (KV_HEADS, HEAD_DIM), (2, KV_HEADS, LANES)]
    hspec = lambda lead, w: pl.BlockSpec((None,) + lead + (tm, w),
                                         lambda i: (i // tab_blocks,) + (0,) * len(lead) + (i % tab_blocks, 0))
    return pl.pallas_call(
        _norm_proj_heads_kernel, grid=(m // tm,), in_specs=in_specs + [pl.BlockSpec((tm, LANES), tab)],
        out_specs=[pl.BlockSpec((tm, w), row) for w, _ in rows_out] + [hspec(sh[:-1], sh[-1]) for sh in head_shapes],
        out_shape=[jax.ShapeDtypeStruct((m, w), dt) for w, dt in rows_out] +
                  [jax.ShapeDtypeStruct((b,) + sh[:-1] + (t, sh[-1]), BF16) for sh in head_shapes],
        compiler_params=_params("arbitrary"), name="norm_proj_heads",
    )(x2, g, wq, wkv, wg, wr, cos, sa, sb, own)


def _rope_tables(pos):
    half = ROPE_DIM // 2
    inv_freq = ROPE_THETA ** (-jnp.arange(half, dtype=F32) / half)
    ang = pos.astype(F32)[:, None] * inv_freq[None, :]
    c, s = jnp.cos(ang), jnp.sin(ang)
    n = pos.shape[0]
    one = jnp.ones((n, HEAD_DIM - ROPE_DIM), F32)
    zero = jnp.zeros((n, HEAD_DIM - ROPE_DIM), F32)
    zh = jnp.zeros((n, half), F32)
    cos = jnp.concatenate([c, c, one], axis=1)
    sa = jnp.concatenate([-s, zh, zero], axis=1)
    sb = jnp.concatenate([zh, s, zero], axis=1)
    rep = LANES // HEAD_DIM
    return tuple(jnp.tile(t, (1, rep)) for t in (cos, sa, sb))


CMP_LANES = 2 * KV_WIDTH


def _compress_accumulate(position_rows, n_chunk, pea_ref, peb_ref, wa_ref, wb_ref, acca_ref, accb_ref, row0,
                         alongside=None):
    for c in range(2):
        lanes = slice(c * KV_WIDTH, (c + 1) * KV_WIDTH)
        da = db = None
        for l in range(CMP_STRIDE):
            if c == 0 and alongside is not None:
                alongside(l)
            x = position_rows(c, l)
            xa = (x + pea_ref[c, l:l + 1, :]).astype(BF16)
            xb = (x + peb_ref[c, l:l + 1, :]).astype(BF16)
            ta, tb = _dot(xa, wa_ref[c, l]), _dot(xb, wb_ref[c, l])
            da, db = (ta, tb) if l == 0 else (da + ta, db + tb)
        acca_ref[pl.ds(row0, n_chunk), lanes] = da
        accb_ref[pl.ds(row0, n_chunk), lanes] = db


def _compress_finish(acca_ref, accb_ref, w2_ref, out_ref):
    n = acca_ref.shape[0]
    rows = lax.broadcasted_iota(jnp.int32, (n, 1), 0)
    for c in range(2):
        lanes = slice(c * KV_WIDTH, (c + 1) * KV_WIDTH)
        nxt = pltpu.roll(accb_ref[:, lanes], n - 1, 0)
        h = jax.nn.gelu(acca_ref[:, lanes] + nxt).astype(BF16)
        out_ref[0, :, lanes] = jnp.where(rows < n - 1, _dot(h, w2_ref[c]), 0.0)


def _compress_prompt_kernel(k_ref, v_ref, pea_ref, peb_ref, wa_ref, wb_ref, w2_ref, out_ref, acca_ref, accb_ref):
    n = acca_ref.shape[0]
    refs = (k_ref.at[0], v_ref.at[0])
    strided = lambda c, l: refs[c][pl.ds(l, n, stride=CMP_STRIDE), :]
    _compress_accumulate(strided, n, pea_ref, peb_ref, wa_ref, wb_ref, acca_ref, accb_ref, 0)
    _compress_finish(acca_ref, accb_ref, w2_ref, out_ref)


_CW_NAMES = ("pea", "peb", "wa", "wb", "w2")


def _compress_prompt(kvw3, cw):
    b, t, _ = kvw3.shape
    n = t // CMP_STRIDE
    const = lambda a: pl.BlockSpec(a.shape, lambda i: (0,) * a.ndim)
    return pl.pallas_call(
        _compress_prompt_kernel, grid=(b,),
        in_specs=[pl.BlockSpec((1, t, KV_WIDTH), lambda i: (i, 0, 0)),
                  pl.BlockSpec((1, t, KV_WIDTH), lambda i: (i, 0, 1))] + [const(cw[k]) for k in _CW_NAMES],
        out_specs=pl.BlockSpec((1, n, CMP_LANES), lambda i: (i, 0, 0)),
        out_shape=jax.ShapeDtypeStruct((b, n, CMP_LANES), F32),
        scratch_shapes=[pltpu.VMEM((n, CMP_LANES), F32), pltpu.VMEM((n, CMP_LANES), F32)],
        compiler_params=_params("arbitrary"), name="compress_prompt",
    )(kvw3, kvw3, *[cw[k] for k in _CW_NAMES])


def _compress_weights(pe, w1, w2):
    eye = jnp.eye(KV_HEADS, dtype=F32)
    blockdiag = lambda w: jnp.einsum('...de,gj->...gdje', w, eye).reshape(w.shape[:-2] + (KV_WIDTH, KV_WIDTH))
    lanes = lambda p: jnp.tile(p, (1, 1, KV_HEADS))
    return {"pea": lanes(pe[:, :CMP_STRIDE]), "peb": lanes(pe[:, CMP_STRIDE:]),
            "wa": blockdiag(w1[:, :CMP_STRIDE]).astype(BF16), "wb": blockdiag(w1[:, CMP_STRIDE:]).astype(BF16),
            "w2": blockdiag(w2).astype(BF16)}


def _paged_fetch(page_copies, pages_per_step, steps_per_seq):
    step = pl.program_id(0) * steps_per_seq + pl.program_id(1)
    total = pl.num_programs(0) * steps_per_seq

    def start_all(s):
        def body(j, c):
            for cp in page_copies(s, j):
                cp.start()
            return c
        lax.fori_loop(0, pages_per_step, body, 0)

    @pl.when(step == 0)
    def _():
        start_all(step)

    @pl.when(step + 1 < total)
    def _():
        start_all(step + 1)

    def wait_body(j, c):
        for cp in page_copies(step, j):
            cp.wait()
        return c
    lax.fori_loop(0, pages_per_step, wait_body, 0)
    return step % 2


UNROLL_PAGES = 8


def _page_of(pt_ref, step, j, pages_per_step, steps_per_seq):
    return pt_ref[step // steps_per_seq, (step % steps_per_seq) * pages_per_step + j]


def _compress_sample_kernel(layer, pages_per_step, steps_per_seq, pt_ref, cache_ref, pea_ref, peb_ref, wa_ref,
                            wb_ref, w2_ref, out_ref, buf_ref, sem_ref, rows_ref, acca_ref, accb_ref):
    page_len = cache_ref.shape[3]

    def page_copies(step, j):
        page = _page_of(pt_ref, step, j, pages_per_step, steps_per_seq)
        return [pltpu.make_async_copy(cache_ref.at[layer, page, pl.ds(0, CMP_LANES), :], buf_ref.at[step % 2, j],
                                      sem_ref.at[step % 2])]

    slot = _paged_fetch(page_copies, pages_per_step, steps_per_seq)

    chunks = page_len // CMP_STRIDE

    def to_rows(c, j):
        tok = buf_ref[slot, j, c * KV_WIDTH:(c + 1) * KV_WIDTH, :].T
        pos = jnp.swapaxes(tok.reshape(chunks, CMP_STRIDE, KV_WIDTH), 0, 1)
        rows_ref[c, :, pl.ds(pl.multiple_of(j * chunks, chunks), chunks), :] = pos

    def key_rows(j, carry):
        to_rows(0, j)
        return carry
    lax.fori_loop(0, pages_per_step, key_rows, 0, unroll=UNROLL_PAGES)

    per_l = pages_per_step // CMP_STRIDE

    def value_rows(l):
        for j in range(l * per_l, (l + 1) * per_l):
            to_rows(1, j)

    part = pl.program_id(1)
    n = rows_ref.shape[2]
    row0 = pl.multiple_of(part * n, n)
    _compress_accumulate(lambda c, l: rows_ref[c, l], n, pea_ref, peb_ref, wa_ref, wb_ref, acca_ref, accb_ref, row0,
                         alongside=value_rows)

    @pl.when(part == steps_per_seq - 1)
    def _():
        _compress_finish(acca_ref, accb_ref, w2_ref, out_ref)


def _compress_sample(cache_t, page_table, layer, cw, steps_per_seq=2):
    db, n_pages = page_table.shape
    page = cache_t.shape[3]
    pps = n_pages // steps_per_seq
    n = n_pages * page // CMP_STRIDE
    const = lambda a: pl.BlockSpec(a.shape, lambda i, p, pt: (0,) * a.ndim)
    assert pps % CMP_STRIDE == 0
    kern = functools.partial(_compress_sample_kernel, layer, pps, steps_per_seq)
    return pl.pallas_call(
        kern,
        grid_spec=pltpu.PrefetchScalarGridSpec(
            num_scalar_prefetch=1, grid=(db, steps_per_seq),
            in_specs=[pl.BlockSpec(memory_space=pl.ANY)] + [const(cw[k]) for k in _CW_NAMES],
            out_specs=pl.BlockSpec((1, n, CMP_LANES), lambda i, p, pt: (i, 0, 0)),
            scratch_shapes=[pltpu.VMEM((2, pps, CMP_LANES, page), F32), pltpu.SemaphoreType.DMA((2,)),
                            pltpu.VMEM((2, CMP_STRIDE, pps * page // CMP_STRIDE, KV_WIDTH), F32),
                            pltpu.VMEM((n, CMP_LANES), F32), pltpu.VMEM((n, CMP_LANES), F32)]),
        out_shape=jax.ShapeDtypeStruct((db, n, CMP_LANES), F32),
        compiler_params=_params("arbitrary", "arbitrary"), name="compress_sample",
    )(page_table, cache_t, *[cw[k] for k in _CW_NAMES])


def _cmp_to_sel(n_cmp_pad, n_cmp, n_sel_pad, n_sel):
    c0 = np.arange(n_cmp_pad)[:, None] * CMP_STRIDE
    s0 = np.arange(n_sel_pad)[None, :] * SEL_BLOCK
    hit = (c0 < s0 + SEL_BLOCK) & (c0 + CMP_BLOCK > s0)
    hit &= (np.arange(n_cmp_pad)[:, None] < n_cmp) & (np.arange(n_sel_pad)[None, :] < n_sel)
    return hit.astype(np.float32)


def _block_to_key(n_blk, n_key):
    return (np.arange(n_key)[None, :] // SEL_BLOCK == np.arange(n_blk)[:, None]).astype(np.float32)


def _nsa_prompt_kernel(qb, tk, q_ref, kc_ref, vc_ref, ks_ref, vs_ref, kw_ref, vw_ref, gate_ref, c2s_ref, hi_ref,
                       o_ref):
    i = pl.program_id(2)
    qs = pl.multiple_of(i * qb, qb)
    rows = HEADS_PER_KV * qb
    n_cmp = kc_ref.shape[0]
    n_sel = c2s_ref.shape[0]
    q_pad = q_ref[...].reshape(rows, 2 * HEAD_DIM)
    q = q_pad[:, :HEAD_DIM]
    qpos = qs + lax.broadcasted_iota(jnp.int32, (rows, 1), 0) % qb

    def biased(s, bias):
        return (s.reshape(HEADS_PER_KV, qb, s.shape[-1]) + bias[None]).reshape(s.shape)

    def weights(s, m):
        return jnp.exp2((s - m).astype(BF16))

    def normalised(acc):
        return acc[:, :HEAD_DIM] / jnp.maximum(acc[:, HEAD_DIM:HEAD_DIM + 1], 1e-30)

    s_cmp = _dot_nt(q, kc_ref[...])
    nw = WINDOW + qb
    w0 = pl.multiple_of(jnp.maximum(qs - WINDOW, 0), qb)
    s_win = _dot_nt(q, kw_ref[pl.ds(w0, nw), :])

    cmp_end = lax.broadcasted_iota(jnp.int32, (1, n_cmp), 1) * CMP_STRIDE + (CMP_BLOCK - 1)
    p = _masked_softmax(s_cmp, cmp_end <= qpos)
    o_cmp = _dot(p.astype(BF16), vc_ref[...])
    psum = p[0:qb]
    for h in range(1, HEADS_PER_KV):
        psum = psum + p[h * qb:(h + 1) * qb]
    imp_t = _dot_nt(c2s_ref[...], psum, precision=HIGHEST)

    rel = (qs + lax.broadcasted_iota(jnp.int32, (qb, nw), 0)) - (w0 + lax.broadcasted_iota(jnp.int32, (qb, nw), 1))
    s = biased(s_win, jnp.where((rel >= 0) & (rel < WINDOW), 0.0, NEG))
    o_win = normalised(_dot(weights(s, jnp.max(s, axis=-1, keepdims=True)), vw_ref[pl.ds(w0, nw), :]))

    blk = lax.broadcasted_iota(jnp.int32, (n_sel, qb), 0)
    qp = qs + lax.broadcasted_iota(jnp.int32, (n_sel, qb), 1)
    cur = qp // SEL_BLOCK
    forced = (blk == 0) | (blk == cur) | (blk == cur - 1)
    valid = blk <= cur
    score = jnp.where(valid, jnp.where(forced, FORCED_SCORE, imp_t), NEG)
    cnt = jnp.zeros((n_sel, qb), jnp.int32)
    for j in range(n_sel):
        sj = score[j:j + 1, :]
        ahead = (sj > score) | ((sj == score) & (j < blk))
        cnt = cnt + ahead.astype(jnp.int32)
    def masked_queries(unsel):
        spread = _dot_tn(unsel.astype(BF16), hi_ref[...])
        return (q_pad.astype(F32) + jnp.concatenate([spread] * HEADS_PER_KV, axis=0)).astype(BF16)

    unsel = jnp.where((cnt < SEL_TOPN) & valid, 0.0, 1.0)
    q_all = masked_queries(unsel)
    q_sel = masked_queries(jnp.where(blk < qs // SEL_BLOCK, unsel, 1.0))

    qrow = lax.broadcasted_iota(jnp.int32, (qb, qb), 0)
    kcol = lax.broadcasted_iota(jnp.int32, (qb, qb), 1)
    s = biased(_dot_nt(q_all, ks_ref[pl.ds(qs, qb), :]), jnp.where(kcol <= qrow, 0.0, NEG))
    m = jnp.max(s, axis=-1, keepdims=True)
    acc = _dot(weights(s, m), vs_ref[pl.ds(qs, qb), :])

    def sel_scores(kt):
        return _dot_nt(q_sel, ks_ref[pl.ds(pl.multiple_of(kt * tk, tk), tk), :])

    def sel_update(kt, s, m, acc):
        m_new = jnp.maximum(m, jnp.max(s, axis=-1, keepdims=True))
        acc = jnp.exp2(m - m_new) * acc + _dot(weights(s, m_new), vs_ref[pl.ds(pl.multiple_of(kt * tk, tk), tk), :])
        return m_new, acc

    def sel_pair(kp, carry):
        s0, s1 = sel_scores(2 * kp), sel_scores(2 * kp + 1)
        return sel_update(2 * kp + 1, s1, *sel_update(2 * kp, s0, *carry))

    def sel_single(kt, carry):
        return sel_update(kt, sel_scores(kt), *carry)

    n_kt = (qs + tk - 1) // tk
    carry = lax.fori_loop(0, n_kt // 2, sel_pair, (m, acc))
    _, acc = lax.fori_loop(n_kt // 2 * 2, n_kt, sel_single, carry)
    o_sel = normalised(acc)

    gt = gate_ref[...]
    g = [jnp.concatenate([gt[:, N_BRANCH * h + c:N_BRANCH * h + c + 1] for h in range(HEADS_PER_KV)], axis=0)
         for c in range(N_BRANCH)]
    o = (g[0] * o_cmp + g[1] * o_sel + g[2] * o_win).astype(o_ref.dtype)
    for h in range(HEADS_PER_KV):
        o_ref[:, h * HEAD_DIM:(h + 1) * HEAD_DIM] = o[h * qb:(h + 1) * qb]


def _nsa_prompt(qh, ksa, kw, va, cvh, gates_h, qb=256, tk=512):
    b, _, t, _ = qh.shape
    n_cmp = cvh.shape[3]
    n_sel = t // SEL_BLOCK
    assert qb % SEL_BLOCK == 0 and t % tk == 0 and WINDOW % qb == 0 and WINDOW + qb <= t and n_sel <= HEAD_DIM
    c2s = jnp.asarray(_cmp_to_sel(n_cmp, n_cmp - 1, n_sel, n_sel).T)
    hi = jnp.asarray(np.concatenate([np.zeros((n_sel, HEAD_DIM)), np.eye(n_sel, HEAD_DIM)], axis=1), dtype=BF16)
    vspec = lambda c: pl.BlockSpec((None, None, None, t, 2 * HEAD_DIM), lambda bi, g, i: (bi, c, g, 0, 0))
    ccomp = lambda c: pl.BlockSpec((None, None, None, n_cmp, HEAD_DIM), lambda bi, g, i: (bi, c, g, 0, 0))
    qspec = pl.BlockSpec((None, HEADS_PER_KV, qb, 2 * HEAD_DIM), lambda bi, g, i: (bi, g, i, 0))
    ospec = pl.BlockSpec((None, qb, HEADS_PER_KV * HEAD_DIM), lambda bi, g, i: (bi, i, g))
    kern = functools.partial(_nsa_prompt_kernel, qb, tk)
    return pl.pallas_call(
        kern, grid=(b, KV_HEADS, t // qb),
        in_specs=[qspec, ccomp(0), ccomp(1),
                  pl.BlockSpec((None, None, t, 2 * HEAD_DIM), lambda bi, g, i: (bi, g, 0, 0)), vspec(0),
                  pl.BlockSpec((None, None, t, HEAD_DIM), lambda bi, g, i: (bi, g, 0, 0)), vspec(1),
                  pl.BlockSpec((None, qb, LANES), lambda bi, g, i: (bi, i, g)),
                  pl.BlockSpec(c2s.shape, lambda bi, g, i: (0, 0)), pl.BlockSpec(hi.shape, lambda bi, g, i: (0, 0))],
        out_specs=ospec,
        out_shape=jax.ShapeDtypeStruct((b, t, NSA_WIDTH), BF16),
        compiler_params=_params("arbitrary", "arbitrary", "arbitrary"), name="nsa_prompt",
    )(qh, cvh, cvh, ksa, va, kw, va, gates_h, c2s, hi)


SROWS = 8 * KV_HEADS


def _sample_cmp_win_kernel(past_len, n_cmp, qz_ref, kvc_ref, win_ref, wnew_ref, c2s_ref, ocmp_ref, owin_ref,
                           imp_ref):
    qz = qz_ref[0]
    kvc = kvc_ref[0]
    n_pad = kvc.shape[0]
    kc, vc = kvc[:, :KV_WIDTH].astype(BF16), kvc[:, KV_WIDTH:].astype(BF16)
    s = _dot_nt(qz, kc)
    n = lax.broadcasted_iota(jnp.int32, (1, n_pad), 1)
    p = _masked_softmax(s, (n < n_cmp) & (n * CMP_STRIDE + (CMP_BLOCK - 1) <= past_len))
    ocmp_ref[0] = _dot(p.astype(BF16), vc)
    r = lax.broadcasted_iota(jnp.int32, (SROWS, 1), 0)
    p = jnp.where(r % 8 < HEADS_PER_KV, p, 0.0)
    psum = jnp.concatenate([jnp.sum(p[8 * g:8 * g + 8], axis=0, keepdims=True) for g in range(KV_HEADS)] +
                           [jnp.zeros((8 - KV_HEADS, n_pad), F32)], axis=0)
    imp_ref[0] = _dot(psum, c2s_ref[...], precision=HIGHEST)

    win = win_ref[0]
    wb = win.shape[1]
    kw, vw = win[:KV_WIDTH].astype(BF16), win[KV_WIDTH:].astype(BF16)
    wnew = wnew_ref[0]
    s = _dot(qz, kw)
    s_new = jnp.sum(qz.astype(F32) * wnew[:, :KV_WIDTH].astype(BF16).astype(F32), axis=-1, keepdims=True)
    rel = wb - lax.broadcasted_iota(jnp.int32, (1, wb), 1)
    msk = (rel >= 0) & (rel < WINDOW)
    s = jnp.where(msk, s, NEG)
    m = jnp.maximum(jnp.max(s, axis=-1, keepdims=True), s_new)
    e = jnp.where(msk, jnp.exp2(s - m), 0.0)
    e_new = jnp.exp2(s_new - m)
    den = jnp.maximum(jnp.sum(e, axis=-1, keepdims=True) + e_new, 1e-30)
    pv = _dot_nt((e / den).astype(BF16), vw)
    pn = (e_new / den).astype(BF16).astype(F32)
    owin_ref[0] = pv + pn * wnew[:, KV_WIDTH:].astype(BF16).astype(F32)


def _sample_cmp_win(past_len, n_cmp, qz, kvc, win, wnew, c2s):
    db = qz.shape[0]
    blk3 = lambda a: pl.BlockSpec((1,) + a.shape[1:], lambda i: (i, 0, 0))
    kern = functools.partial(_sample_cmp_win_kernel, past_len, n_cmp)
    outs = (jax.ShapeDtypeStruct((db, SROWS, KV_WIDTH), F32), jax.ShapeDtypeStruct((db, SROWS, KV_WIDTH), F32),
            jax.ShapeDtypeStruct((db, 8, c2s.shape[1]), F32))
    return pl.pallas_call(
        kern, grid=(db,),
        in_specs=[blk3(qz), blk3(kvc), blk3(win), blk3(wnew), pl.BlockSpec(c2s.shape, lambda i: (0, 0))],
        out_specs=[pl.BlockSpec((1, SROWS, KV_WIDTH), lambda i: (i, 0, 0)),
                   pl.BlockSpec((1, SROWS, KV_WIDTH), lambda i: (i, 0, 0)),
                   pl.BlockSpec((1, 8, c2s.shape[1]), lambda i: (i, 0, 0))],
        out_shape=outs, compiler_params=_params("arbitrary"), name="sample_cmp_win",
    )(qz, kvc, win, wnew, c2s)


def _select_kernel(n_sel, cur, imp_ref, sel_ref):
    imp = imp_ref[...]
    blk = lax.broadcasted_iota(jnp.int32, imp.shape, 1)
    forced = (blk == 0) | (blk == cur) | (blk == cur - 1)
    valid = blk <= cur
    score = jnp.where(valid, jnp.where(forced, FORCED_SCORE, imp), NEG)

    def body(j, cnt):
        sj = jnp.sum(jnp.where(blk == j, score, 0.0), axis=-1, keepdims=True)
        ahead = (sj > score) | ((sj == score) & (j < blk))
        return cnt + ahead.astype(jnp.int32)

    cnt = lax.fori_loop(0, n_sel, body, jnp.zeros(imp.shape, jnp.int32))
    sel_ref[...] = jnp.where((cnt < SEL_TOPN) & valid, 1.0, 0.0)


def _select(imp2, n_sel, cur):
    return pl.pallas_call(
        functools.partial(_select_kernel, n_sel, cur),
        out_shape=jax.ShapeDtypeStruct(imp2.shape, F32), name="sample_select",
        compiler_params=pltpu.CompilerParams(vmem_limit_bytes=VMEM_LIMIT),
    )(imp2)


def _sample_sel_kernel(layer, pages_per_step, steps_per_seq, pt_ref, cache_ref, qz_ref, sel_ref, selnew_ref,
                       knew_ref, b2k_ref, ocmp_ref, owin_ref, gate_ref, o_ref, buf_ref, sem_ref, m_ref, l_ref,
                       acc_ref):
    page_len = cache_ref.shape[3]

    def page_copies(step, j):
        page = _page_of(pt_ref, step, j, pages_per_step, steps_per_seq)
        return [pltpu.make_async_copy(cache_ref.at[layer, page, pl.ds(CMP_LANES, CMP_LANES), :],
                                      buf_ref.at[step % 2, :, pl.ds(pl.multiple_of(j * page_len, page_len), page_len)],
                                      sem_ref.at[step % 2])]

    slot = _paged_fetch(page_copies, pages_per_step, steps_per_seq)
    part = pl.program_id(1)
    qz = qz_ref[0]

    @pl.when(part == 0)
    def _():
        m_ref[...] = jnp.full(m_ref.shape, NEG, F32)
        l_ref[...] = jnp.zeros(l_ref.shape, F32)
        acc_ref[...] = jnp.zeros(acc_ref.shape, F32)

    kv = buf_ref[slot].astype(BF16)
    s = _dot(qz, kv[:KV_WIDTH])
    msk = _dot(sel_ref[0].astype(BF16), b2k_ref[...]) > 0.5
    s = jnp.where(msk, s, NEG)
    m_old = m_ref[...]
    m_new = jnp.maximum(m_old, jnp.max(s, axis=-1, keepdims=True))
    alpha = jnp.exp2(m_old - m_new)
    e = jnp.where(msk, jnp.exp2(s - m_new), 0.0)
    l_ref[...] = alpha * l_ref[...] + jnp.sum(e, axis=-1, keepdims=True)
    acc_ref[...] = alpha * acc_ref[...] + _dot_nt(e.astype(BF16), kv[KV_WIDTH:])
    m_ref[...] = m_new

    @pl.when(part == steps_per_seq - 1)
    def _():
        knew = knew_ref[0].astype(BF16).astype(F32)
        s_new = jnp.sum(qz.astype(F32) * knew[:, :KV_WIDTH], axis=-1, keepdims=True)
        on = selnew_ref[0][:, 0:1] > 0.5
        s_new = jnp.where(on, s_new, NEG)
        m_old = m_ref[...]
        m_new = jnp.maximum(m_old, s_new)
        alpha = jnp.exp2(m_old - m_new)
        e_new = jnp.where(on, jnp.exp2(s_new - m_new), 0.0)
        l = alpha * l_ref[...] + e_new
        acc = alpha * acc_ref[...] + e_new.astype(BF16).astype(F32) * knew[:, KV_WIDTH:]
        o_sel = acc / jnp.maximum(l, 1e-30)
        g = gate_ref[0]
        o_ref[0] = g[:, 0:1] * ocmp_ref[0] + g[:, 1:2] * o_sel + g[:, 2:3] * owin_ref[0]


def _sample_sel(cache_t, page_table, layer, qz, sel16, knew, ocmp, owin, gz, steps_per_seq=2):
    db, n_pages = page_table.shape
    page = cache_t.shape[3]
    pps = n_pages // steps_per_seq
    keys = pps * page
    blocks = keys // SEL_BLOCK
    b2k = jnp.asarray(_block_to_key(blocks, keys), dtype=BF16)
    blk3 = lambda a: pl.BlockSpec((1,) + a.shape[1:], lambda i, p, pt: (i, 0, 0))
    kern = functools.partial(_sample_sel_kernel, layer, pps, steps_per_seq)
    return pl.pallas_call(
        kern,
        grid_spec=pltpu.PrefetchScalarGridSpec(
            num_scalar_prefetch=1, grid=(db, steps_per_seq),
            in_specs=[pl.BlockSpec(memory_space=pl.ANY), blk3(qz),
                      pl.BlockSpec((1, SROWS, blocks), lambda i, p, pt: (i, 0, p)),
                      pl.BlockSpec((1, SROWS, LANES), lambda i, p, pt: (i, 0, steps_per_seq * blocks // LANES)),
                      blk3(knew),
                      pl.BlockSpec(b2k.shape, lambda i, p, pt: (0, 0)), blk3(ocmp), blk3(owin), blk3(gz)],
            out_specs=pl.BlockSpec((1, SROWS, KV_WIDTH), lambda i, p, pt: (i, 0, 0)),
            scratch_shapes=[pltpu.VMEM((2, CMP_LANES, keys), F32), pltpu.SemaphoreType.DMA((2,)),
                            pltpu.VMEM((SROWS, 1), F32), pltpu.VMEM((SROWS, 1), F32),
                            pltpu.VMEM((SROWS, KV_WIDTH), F32)]),
        out_shape=jax.ShapeDtypeStruct((db, SROWS, KV_WIDTH), F32),
        compiler_params=_params("arbitrary", "arbitrary"), name="sample_sel",
    )(page_table, cache_t, qz, sel16, sel16, knew, b2k, ocmp, owin, gz)


def _bdot(dot, a, b):
    return dot(a.astype(BF16), b.astype(BF16))


def _head_sum(x, hm):
    xh = x.astype(BF16)
    xl = (x - xh.astype(F32)).astype(BF16)
    return _dot(xh, hm) + _dot(xl, hm)


def _softplus(z):
    return jnp.maximum(z, 0.0) + jnp.log(1.0 + jnp.exp(-jnp.abs(z)))


CHUNK_GROUP = 8


def _rwkv_kernel(chunk, n_valid, pr_ref, prev_ref, s0_ref, mu_ref, w0_ref, wup_ref, a0_ref, aup_ref, gup_ref,
                 kk_ref, ka_ref, rk_ref, lnw_ref, lnb_ref, hm_ref, tri_ref, out_ref, sout_ref,
                 carry_ref, state_ref, r_ref, k_ref, v_ref, lw_ref, a_ref, b_ref, rh_ref, y_ref, m_ref, n_ref, et_ref):
    ti = pl.program_id(1)
    tt = pr_ref.shape[1]
    w_ = RWKV_WIDTH
    hd = RWKV_HEAD

    @pl.when(ti == 0)
    def _():
        carry_ref[...] = prev_ref[0]
        state_ref[...] = s0_ref[0]

    p = pr_ref[0]
    row = lax.broadcasted_iota(jnp.int32, (tt, 1), 0)
    shifted = jnp.where(row == 0, carry_ref[...], pltpu.roll(p, 1, 0))
    carry_ref[...] = p[tt - 1:tt]
    xs = p + (shifted - p) * mu_ref[...]
    r, k, v = xs[:, :w_], xs[:, w_:2 * w_], xs[:, 2 * w_:3 * w_]
    wa = xs[:, 3 * w_:3 * w_ + DECAY_LORA + AAA_LORA]
    gl = xs[:, 3 * w_ + DECAY_LORA + AAA_LORA:]
    hm = hm_ref[...]
    w = -_softplus(-(w0_ref[...] + _dot(jnp.tanh(wa).astype(BF16), wup_ref[...]))) - 0.5
    lw = -jnp.exp(w)
    a = jax.nn.sigmoid(a0_ref[...] + _dot(wa.astype(BF16), aup_ref[...]))
    g = _dot(jax.nn.sigmoid(gl).astype(BF16), gup_ref[...])
    kk = k * kk_ref[...]
    kk = kk / jnp.maximum(jnp.sqrt(_head_sum(kk * kk, hm)), 1e-12)
    k = k * (1.0 + (a - 1.0) * ka_ref[...])
    av, bv = -kk, kk * a
    if n_valid < tt:
        keep = row < n_valid
        lw, k, v, av, bv = (jnp.where(keep, t, 0.0) for t in (lw, k, v, av, bv))
    r_ref[...], k_ref[...], v_ref[...], lw_ref[...], a_ref[...], b_ref[...] = r, k, v, lw, av, bv

    ci = lax.broadcasted_iota(jnp.int32, (chunk, chunk), 0)
    cj = lax.broadcasted_iota(jnp.int32, (chunk, chunk), 1)
    strict, incl = ci > cj, ci >= cj
    dot = functools.partial(_bdot, _dot)
    dot_nt = functools.partial(_bdot, _dot_nt)
    dot_tn = functools.partial(_bdot, _dot_tn)
    group = min(CHUNK_GROUP, tt // chunk)

    def prepare(cg, carry):
        jobs, pre = [], []
        for ci in range(group):
            c = cg * group + ci
            rows = pl.ds(pl.multiple_of(c * chunk, chunk), chunk)
            lwc = lw_ref[rows, :]
            cs = _dot(tri_ref[...], lwc, precision=HIGHEST)
            e_in, e_out = jnp.exp(cs), jnp.exp(-cs)
            total = cs[chunk - 1:chunk]
            e_end = jnp.exp(total - cs)
            kc, bc = k_ref[rows, :], b_ref[rows, :]
            pre.append(dict(c=c, rows=rows, rt=r_ref[rows, :] * e_in, at=a_ref[rows, :] * jnp.exp(cs - lwc),
                            kt=kc * e_out, bt=bc * e_out, kh=kc * e_end, bh=bc * e_end, vc=v_ref[rows, :]))
            et_ref[c] = jnp.broadcast_to(jnp.exp(total), et_ref.shape[1:])
            jobs += [(ci, h, slice(h * hd, (h + 1) * hd)) for h in range(RWKV_HEADS)]
        nj = range(len(jobs))
        part = lambda name: [pre[ci][name][:, sl] for ci, _, sl in jobs]
        at, rt, bt, kt, bh, kh, vh = (part(n_) for n_ in ("at", "rt", "bt", "kt", "bh", "kh", "vc"))
        gm = [dot_nt(jnp.concatenate([at[j], rt[j]], axis=0), jnp.concatenate([bt[j], kt[j]], axis=0))
              for j in nj]
        lab = [jnp.where(strict, g_[:chunk, :chunk], 0.0) for g_ in gm]
        lak = [jnp.where(strict, g_[:chunk, chunk:], 0.0) for g_ in gm]
        mrb = [jnp.where(incl, g_[chunk:, :chunk], 0.0) for g_ in gm]
        mrk = [jnp.where(incl, g_[chunk:, chunk:], 0.0) for g_ in gm]
        wm = [jnp.concatenate([at[j], dot(lak[j], vh[j])], axis=1) for j in nj]
        x, n = lab, 1
        while 2 * n < chunk:
            xw = [dot(x[j], jnp.concatenate([wm[j], x[j]], axis=1)) for j in nj]
            x = [t[:, 2 * hd:] for t in xw]
            wm = [wm[j] + xw[j][:, :2 * hd] for j in nj]
            n *= 2
        wm = [wm[j] + dot(x[j], wm[j]) for j in nj]
        mw = [dot(mrb[j], wm[j]) for j in nj]
        y0 = [mw[j][:, hd:] + dot(mrk[j], vh[j]) for j in nj]
        wb = [dot_tn(wm[j], bh[j]) for j in nj]
        vk = [dot_tn(vh[j], kh[j]) for j in nj]
        for j, (ci, h, sl) in enumerate(jobs):
            c, rows = pre[ci]["c"], pre[ci]["rows"]
            rh_ref[rows, sl] = rt[j] + mw[j][:, :hd]
            y_ref[rows, sl] = y0[j]
            m_ref[c, h] = wb[j][:hd]
            n_ref[c, h] = wb[j][hd:] + vk[j]
        return carry

    def advance(c, carry):
        rows = pl.ds(pl.multiple_of(c * chunk, chunk), chunk)
        e_tot = et_ref[c]
        sls = [slice(h * hd, (h + 1) * hd) for h in range(RWKV_HEADS)]
        s0 = [state_ref[h] for h in range(RWKV_HEADS)]
        ys = [dot_nt(rh_ref[rows, sl], s0[h]) for h, sl in enumerate(sls)]
        sm = [dot(s0[h], m_ref[c, h]) for h in range(RWKV_HEADS)]
        for h, sl in enumerate(sls):
            y_ref[rows, sl] += ys[h]
            state_ref[h] = s0[h] * e_tot[0:1, sl] + sm[h] + n_ref[c, h]
        return carry

    lax.fori_loop(0, tt // chunk // group, prepare, 0)
    lax.fori_loop(0, tt // chunk, advance, 0, unroll=True)

    y = y_ref[...]
    inv_n = 1.0 / hd
    mean = _head_sum(y, hm) * inv_n
    d = y - mean
    var = _head_sum(d * d, hm) * inv_n
    y = d * lax.rsqrt(var + LNX_EPS) * lnw_ref[...] + lnb_ref[...]
    y = y + _head_sum(r_ref[...] * k_ref[...] * rk_ref[...], hm) * v_ref[...]
    out_ref[0] = (y * g).astype(out_ref.dtype)
    sout_ref[0] = state_ref[...]


def _rwkv(pr3, prev, s0, rw, chunk, tt, n_valid):
    b, t, _ = pr3.shape
    hm = jnp.asarray(np.kron(np.eye(RWKV_HEADS), np.ones((RWKV_HEAD, RWKV_HEAD))), dtype=BF16)
    tri = jnp.asarray(np.tril(np.ones((chunk, chunk))), dtype=F32)
    vec = lambda a: pl.BlockSpec(a.shape, lambda i, j: (0, 0))
    prev3 = prev[:, None, :]
    names = ("mu", "w0", "wup", "a0", "aup", "gup", "kk", "ka", "rk", "lnw", "lnb")
    kern = functools.partial(_rwkv_kernel, chunk, n_valid)
    per_chunk = (tt // chunk, RWKV_HEADS, RWKV_HEAD, RWKV_HEAD)
    return pl.pallas_call(
        kern, grid=(b, t // tt),
        in_specs=[pl.BlockSpec((1, tt, RWKV_PROJ), lambda i, j: (i, j, 0)),
                  pl.BlockSpec((1, 1, RWKV_PROJ), lambda i, j: (i, 0, 0)),
                  pl.BlockSpec((1, RWKV_HEADS, RWKV_HEAD, RWKV_HEAD), lambda i, j: (i, 0, 0, 0))] +
                 [vec(rw[n]) for n in names] + [vec(hm), vec(tri)],
        out_specs=[pl.BlockSpec((1, tt, RWKV_WIDTH), lambda i, j: (i, j, 0)),
                   pl.BlockSpec((1, RWKV_HEADS, RWKV_HEAD, RWKV_HEAD), lambda i, j: (i, 0, 0, 0))],
        out_shape=(jax.ShapeDtypeStruct((b, t, RWKV_WIDTH), BF16),
                   jax.ShapeDtypeStruct((b, RWKV_HEADS, RWKV_HEAD, RWKV_HEAD), F32)),
        scratch_shapes=[pltpu.VMEM((1, RWKV_PROJ), F32), pltpu.VMEM((RWKV_HEADS, RWKV_HEAD, RWKV_HEAD), F32)] +
                       [pltpu.VMEM((tt, RWKV_WIDTH), F32) for _ in range(8)] +
                       [pltpu.VMEM(per_chunk, F32), pltpu.VMEM(per_chunk, F32),
                        pltpu.VMEM((tt // chunk, 8, RWKV_WIDTH), F32)],
        compiler_params=_params("arbitrary", "arbitrary"), name="rwkv",
    )(pr3, prev3, s0, *[rw[n] for n in names], hm, tri)


def _ffn_head(x_ref, nsa_ref, rwkv_ref, wo_ref, gpost_ref, gpre_ref, x1_ref, h2_ref, acc_ref):
    n = nsa_ref.shape[1]
    mixed = _dot(nsa_ref[...], wo_ref[:n, :]) + _dot(rwkv_ref[...], wo_ref[n:, :])
    x1 = x_ref[...] + _rms(mixed, gpost_ref[...])
    x1_ref[...] = x1
    h2_ref[...] = _rms(x1, gpre_ref[...]).astype(BF16)
    acc_ref[...] = jnp.zeros(acc_ref.shape, F32)


def _ffn_tail(gate2, gate1, gate, val, cw_ref, cb_ref, wd_ref, acc_ref):
    conv = cb_ref[...] + (gate2 * cw_ref[0:1, :] + gate1 * cw_ref[1:2, :] + gate * cw_ref[2:3, :])
    acc_ref[...] += _dot((jax.nn.gelu(conv) * val).astype(BF16), wd_ref[...])


FFN_COLS = 256


def _ffn_seq_kernel(tiles_per_seq, x_ref, nsa_ref, rwkv_ref, wo_ref, gpost_ref, gpre_ref, gfpost_ref, wg_ref, wv_ref, cw_ref,
                    cb_ref, wd_ref, out_ref, tail_ref, x1_ref, h2_ref, acc_ref, carry_ref):
    i, j = pl.program_id(0), pl.program_id(1)
    tm = x_ref.shape[0]

    @pl.when(j == 0)
    def _():
        _ffn_head(x_ref, nsa_ref, rwkv_ref, wo_ref, gpost_ref, gpre_ref, x1_ref, h2_ref, acc_ref)

    @pl.when(i % tiles_per_seq == 0)
    def _():
        carry_ref[j] = jnp.zeros(carry_ref.shape[1:], F32)

    h2 = h2_ref[...]
    row = lax.broadcasted_iota(jnp.int32, (tm, 1), 0)
    tf = wg_ref.shape[1]
    cols = [slice(c0, min(c0 + FFN_COLS, tf)) for c0 in range(0, tf, FFN_COLS)]
    up = lambda c: (_dot(h2, wg_ref[:, c]), _dot(h2, wv_ref[:, c]))
    nxt = up(cols[0])
    for k, c in enumerate(cols):
        gate, val = nxt
        if k + 1 < len(cols):
            nxt = up(cols[k + 1])
        prev = carry_ref[j, :, c]
        gate1 = jnp.where(row == 0, prev[7:8], pltpu.roll(gate, 1, 0))
        gate2 = jnp.where(row == 0, prev[6:7], jnp.where(row == 1, prev[7:8], pltpu.roll(gate, 2, 0)))
        carry_ref[j, :, c] = gate[tm - 8:tm]
        tail_ref[0, :, c] = gate[tm - 8:tm]
        conv = cb_ref[:, c] + (gate2 * cw_ref[0:1, c] + gate1 * cw_ref[1:2, c] + gate * cw_ref[2:3, c])
        acc_ref[...] += _dot((jax.nn.gelu(conv) * val).astype(BF16), wd_ref[c, :])

    @pl.when(j == pl.num_programs(1) - 1)
    def _():
        out_ref[...] = x1_ref[...] + _rms(acc_ref[...], gfpost_ref[...])


def _ffn_step_kernel(x_ref, nsa_ref, rwkv_ref, wo_ref, gpost_ref, gpre_ref, gfpost_ref, wg_ref, wv_ref, cw_ref, cb_ref,
                     wd_ref, prev_ref, out_ref, gate_ref, x1_ref, h2_ref, acc_ref):
    j = pl.program_id(0)

    @pl.when(j == 0)
    def _():
        _ffn_head(x_ref, nsa_ref, rwkv_ref, wo_ref, gpost_ref, gpre_ref, x1_ref, h2_ref, acc_ref)

    h2 = h2_ref[...]
    gate, val = _dot(h2, wg_ref[...]), _dot(h2, wv_ref[...])
    gate_ref[...] = gate
    _ffn_tail(prev_ref[0], prev_ref[1], gate, val, cw_ref, cb_ref, wd_ref, acc_ref)

    @pl.when(j == pl.num_programs(0) - 1)
    def _():
        out_ref[...] = x1_ref[...] + _rms(acc_ref[...], gfpost_ref[...])


def _ffn_seq(x2, nsa2, rwkv2, fw, seq_len, tm, tf):
    m, d = x2.shape
    d_ff = fw["wd"].shape[0]
    nf = d_ff // tf
    row = lambda i, j: (i, 0)
    full = lambda i, j: (0, 0)
    tiles_per_seq = seq_len // tm
    scratch = [pltpu.VMEM((tm, d), F32), pltpu.VMEM((tm, d), BF16), pltpu.VMEM((tm, d), F32),
               pltpu.VMEM((nf, 8, tf), F32)]
    once = dict(pipeline_mode=pl.Buffered(1)) if nf == 1 else {}
    return pl.pallas_call(
        functools.partial(_ffn_seq_kernel, tiles_per_seq), grid=(m // tm, nf),
        in_specs=[pl.BlockSpec((tm, d), row), pl.BlockSpec((tm, nsa2.shape[1]), row),
                  pl.BlockSpec((tm, rwkv2.shape[1]), row), pl.BlockSpec((d, d), full, **once),
                  pl.BlockSpec((1, d), full), pl.BlockSpec((1, d), full), pl.BlockSpec((1, d), full),
                  pl.BlockSpec((d, tf), lambda i, j: (0, j), **once),
                  pl.BlockSpec((d, tf), lambda i, j: (0, nf + j), **once),
                  pl.BlockSpec((CONV_WIDTH, tf), lambda i, j: (0, j)), pl.BlockSpec((1, tf), lambda i, j: (0, j)),
                  pl.BlockSpec((tf, d), lambda i, j: (j, 0), **once)],
        out_specs=[pl.BlockSpec((tm, d), row), pl.BlockSpec((1, 8, tf), lambda i, j: (i, 0, j))],
        out_shape=(jax.ShapeDtypeStruct((m, d), F32), jax.ShapeDtypeStruct((m // tm, 8, d_ff), F32)),
        scratch_shapes=scratch, compiler_params=_params("arbitrary", "arbitrary"), name="ffn_seq",
    )(x2, nsa2, rwkv2, fw["wo"], fw["gpost"], fw["gpre"], fw["gfpost"], fw["wup"], fw["wup"], fw["cw"], fw["cb"],
      fw["wd"])


def _ffn_step(x2, nsa2, rwkv2, fw, prev, tf):
    m, d = x2.shape
    d_ff = fw["wd"].shape[0]
    nf = d_ff // tf
    full = lambda j: (0, 0)
    scratch = [pltpu.VMEM((m, d), F32), pltpu.VMEM((m, d), BF16), pltpu.VMEM((m, d), F32)]
    return pl.pallas_call(
        _ffn_step_kernel, grid=(nf,),
        in_specs=[pl.BlockSpec((m, d), full), pl.BlockSpec(nsa2.shape, full), pl.BlockSpec(rwkv2.shape, full),
                  pl.BlockSpec((d, d), full),
                  pl.BlockSpec((1, d), full), pl.BlockSpec((1, d), full), pl.BlockSpec((1, d), full),
                  pl.BlockSpec((d, tf), lambda j: (0, j)), pl.BlockSpec((d, tf), lambda j: (0, nf + j)),
                  pl.BlockSpec((CONV_WIDTH, tf), lambda j: (0, j)), pl.BlockSpec((1, tf), lambda j: (0, j)),
                  pl.BlockSpec((tf, d), lambda j: (j, 0)),
                  pl.BlockSpec((CONV_WIDTH - 1, m, tf), lambda j: (0, 0, j))],
        out_specs=[pl.BlockSpec((m, d), full), pl.BlockSpec((m, tf), lambda j: (0, j))],
        out_shape=(jax.ShapeDtypeStruct((m, d), F32), jax.ShapeDtypeStruct((m, d_ff), F32)),
        scratch_shapes=scratch, compiler_params=_params("arbitrary"), name="ffn_step",
    )(x2, nsa2, rwkv2, fw["wo"], fw["gpost"], fw["gpre"], fw["gfpost"], fw["wup"], fw["wup"], fw["cw"], fw["cb"],
      fw["wd"], prev)


def _layer_weights(l, p):
    w_in = p["w_in"][l]
    o = NSA_WIDTH + 6 * KV_WIDTH
    per_group = N_BRANCH * HEADS_PER_KV
    wg = jnp.pad(w_in[:, o:NSA_PROJ].reshape(-1, KV_HEADS, per_group), ((0, 0), (0, 0), (0, LANES - per_group)))
    wg = wg.reshape(-1, KV_HEADS * LANES)
    row = lambda a: a[l][None, :].astype(F32)
    zpad = jnp.zeros((AAA_LORA, RWKV_WIDTH), F32)
    rw = {"mu": row(p["rwkv_mu"]), "w0": row(p["rwkv_w0"]),
          "wup": jnp.concatenate([p["rwkv_w_up"][l], zpad], axis=0).astype(BF16),
          "a0": row(p["rwkv_a0"]), "aup": jnp.concatenate([zpad, p["rwkv_a_up"][l]], axis=0).astype(BF16),
          "gup": p["rwkv_g_up"][l].astype(BF16), "kk": row(p["rwkv_k_k"]), "ka": row(p["rwkv_k_a"]),
          "rk": p["rwkv_r_k"][l].reshape(1, RWKV_WIDTH), "lnw": row(p["rwkv_lnx_w"]), "lnb": row(p["rwkv_lnx_b"])}
    fw = {"wo": p["w_out"][l].astype(BF16), "gpost": row(p["norm_mix_post"]), "gpre": row(p["norm_ffn_pre"]),
          "gfpost": row(p["norm_ffn_post"]), "wup": p["ffn_w_up"][l].astype(BF16), "cw": p["ffn_conv_w"][l],
          "cb": row(p["ffn_conv_b"]), "wd": p["ffn_w_down"][l].astype(BF16)}
    return {"g_in": row(p["norm_mix_pre"]), "wq": w_in[:, :NSA_WIDTH].astype(BF16),
            "wkv": w_in[:, NSA_WIDTH:o].astype(BF16), "wg": wg.astype(BF16), "wr": w_in[:, NSA_PROJ:].astype(BF16),
            "cw": _compress_weights(p["cmp_pe"][l], p["cmp_w1"][l], p["cmp_w2"][l]), "rw": rw, "fw": fw}


def _prompt_layer(x, lw, tabs):
    b, t, d = x.shape
    m = b * t
    x2 = x.reshape(m, d)
    own_block = jnp.asarray(np.concatenate([np.zeros((t, HEAD_DIM)), NEG * _block_to_key(HEAD_DIM, t).T], axis=1),
                            dtype=BF16)
    kvw, gates, pr, qh, ksa, kw, va = _norm_proj(x2, lw["g_in"], lw["wq"], lw["wkv"], lw["wg"], lw["wr"], tabs,
                                                 tm=256, heads_of=(b, t, own_block))
    kvw3 = kvw.reshape(b, t, 6 * KV_WIDTH)
    kvc = _compress_prompt(kvw3, lw["cw"])
    cvh = kvc.astype(BF16).reshape(b, -1, 2, KV_HEADS, HEAD_DIM).transpose(0, 2, 3, 1, 4)
    o_nsa = _nsa_prompt(qh, ksa, kw, va, cvh, gates.reshape(b, t, KV_HEADS * LANES)).reshape(m, NSA_WIDTH)
    s0 = jnp.zeros((b, RWKV_HEADS, RWKV_HEAD, RWKV_HEAD), F32)
    pr3 = pr.reshape(b, t, RWKV_PROJ)
    o_rwkv, wkv = _rwkv(pr3, jnp.zeros((b, RWKV_PROJ), F32), s0, lw["rw"], chunk=64, tt=min(t, 512), n_valid=t)
    d_ff = lw["fw"]["wd"].shape[0]
    tm = min(t, 512)
    y, tail = _ffn_seq(x2, o_nsa, o_rwkv.reshape(m, RWKV_WIDTH), lw["fw"], t, tm=tm, tf=d_ff)
    tail = tail[t // tm - 1::t // tm]
    kv_rows = kvw3[:, :, :4 * KV_WIDTH].reshape(b, t, 4, KV_HEADS, HEAD_DIM)
    wn = min(WINDOW, t)
    win = kvw3[:, t - wn:, 4 * KV_WIDTH:].reshape(b, wn, 2, KV_HEADS, HEAD_DIM)
    return y.reshape(b, t, d), kv_rows, win, wkv, pr3[:, -1], tail[:, 8 - (CONV_WIDTH - 1):]


def _sample_layer(x, lw, tabs, l, cache_t, page_table, st_win, st_wkv, st_shift, st_conv):
    db, tn, d = x.shape
    past_len = page_table.shape[1] * cache_t.shape[3]
    x2 = x.reshape(db, d)
    q, kvw, gates, pr = _norm_proj(x2, lw["g_in"], lw["wq"], lw["wkv"], lw["wg"], lw["wr"], tabs, tm=db)
    qs = (q * (ATTN_SCALE * LOG2E)).astype(BF16).reshape(db, KV_HEADS, HEADS_PER_KV, HEAD_DIM)
    eye = jnp.eye(KV_HEADS, dtype=BF16)
    qz = jnp.einsum('bghd,gj->bghjd', qs, eye).reshape(db, KV_HEADS, HEADS_PER_KV, KV_WIDTH)
    qz = jnp.pad(qz, ((0, 0), (0, 0), (0, 8 - HEADS_PER_KV), (0, 0))).reshape(db, SROWS, KV_WIDTH)
    kvc = _compress_sample(cache_t, page_table, l, lw["cw"])
    n_cmp = past_len // CMP_STRIDE - 1
    n_sel = -(-(past_len + tn) // SEL_BLOCK)
    n_sel_pad = -(-n_sel // LANES) * LANES
    c2s = jnp.asarray(_cmp_to_sel(kvc.shape[1], n_cmp, n_sel_pad, n_sel))
    wb = st_win.shape[1]
    win = st_win.transpose(0, 2, 3, 4, 1).reshape(db, 2 * KV_WIDTH, wb)
    wnew = kvw[:, None, 4 * KV_WIDTH:]
    ocmp, owin, imp = _sample_cmp_win(past_len, n_cmp, qz, kvc, win, wnew, c2s)
    sel = _select(imp.reshape(db * 8, n_sel_pad), n_sel, past_len // SEL_BLOCK).reshape(db, 8, n_sel_pad)
    sel16 = jnp.broadcast_to(sel[:, :KV_HEADS, None, :], (db, KV_HEADS, 8, n_sel_pad)).reshape(db, SROWS, n_sel_pad)
    gz = gates.reshape(db, KV_HEADS, LANES)[:, :, :N_BRANCH * HEADS_PER_KV].reshape(db, KV_HEADS, HEADS_PER_KV, N_BRANCH)
    gz = jnp.pad(gz, ((0, 0), (0, 0), (0, 8 - HEADS_PER_KV), (0, 0))).reshape(db, SROWS, N_BRANCH)
    knew = kvw[:, None, 2 * KV_WIDTH:4 * KV_WIDTH]
    o16 = _sample_sel(cache_t, page_table, l, qz, sel16, knew, ocmp, owin, gz)
    o4 = o16.reshape(db, KV_HEADS, 8, KV_HEADS, HEAD_DIM)[:, :, :HEADS_PER_KV]
    o_nsa = jnp.stack([o4[:, g, :, g] for g in range(KV_HEADS)], axis=1).reshape(db, NSA_WIDTH)
    pad_t = 8
    pr3 = jnp.pad(pr[:, None, :], ((0, 0), (0, pad_t - tn), (0, 0)))
    o_rwkv, wkv = _rwkv(pr3, st_shift, st_wkv, lw["rw"], chunk=pad_t, tt=pad_t, n_valid=tn)
    prev = st_conv.transpose(1, 0, 2)
    d_ff = lw["fw"]["wd"].shape[0]
    y, gate = _ffn_step(x2, o_nsa.astype(BF16), o_rwkv[:, 0], lw["fw"], prev, tf=d_ff // 2)
    kv_rows = kvw[:, :4 * KV_WIDTH].reshape(db, tn, 4, KV_HEADS, HEAD_DIM)
    win_new = jnp.concatenate([st_win[:, tn:], kvw[:, 4 * KV_WIDTH:].reshape(db, tn, 2, KV_HEADS, HEAD_DIM)], axis=1)
    conv_new = jnp.concatenate([st_conv[:, 1:], gate[:, None, :]], axis=1)
    return y.reshape(db, tn, d), kv_rows, win_new, wkv, pr, conv_new


def kernel(x_prompt, x_sample, cache_kv, page_table, state_win, state_wkv, state_shift, state_conv, norm_mix_pre, norm_mix_post, norm_ffn_pre, norm_ffn_post, w_in, w_out, cmp_pe, cmp_w1, cmp_w2, rwkv_mu, rwkv_w0, rwkv_w_up, rwkv_a0, rwkv_a_up, rwkv_g_up, rwkv_k_k, rwkv_k_a, rwkv_r_k, rwkv_lnx_w, rwkv_lnx_b, ffn_w_up, ffn_conv_w, ffn_conv_b, ffn_w_down):
    p = dict(norm_mix_pre=norm_mix_pre, norm_mix_post=norm_mix_post, norm_ffn_pre=norm_ffn_pre,
             norm_ffn_post=norm_ffn_post, w_in=w_in, w_out=w_out, cmp_pe=cmp_pe, cmp_w1=cmp_w1, cmp_w2=cmp_w2,
             rwkv_mu=rwkv_mu, rwkv_w0=rwkv_w0, rwkv_w_up=rwkv_w_up, rwkv_a0=rwkv_a0, rwkv_a_up=rwkv_a_up,
             rwkv_g_up=rwkv_g_up, rwkv_k_k=rwkv_k_k, rwkv_k_a=rwkv_k_a, rwkv_r_k=rwkv_r_k,
             rwkv_lnx_w=rwkv_lnx_w, rwkv_lnx_b=rwkv_lnx_b, ffn_w_up=ffn_w_up, ffn_conv_w=ffn_conv_w,
             ffn_conv_b=ffn_conv_b, ffn_w_down=ffn_w_down)
    depth = w_in.shape[0]
    b, t, _ = x_prompt.shape
    db, tn, _ = x_sample.shape
    n_pool, page = cache_kv.shape[1], cache_kv.shape[2]
    past_len = page_table.shape[1] * page
    cache_t = cache_kv.transpose(0, 1, 3, 4, 5, 2).reshape(depth, n_pool, 2 * CMP_LANES, page)
    tabs_p = _rope_tables(jnp.arange(t, dtype=jnp.int32))
    tabs_s = _rope_tables(jnp.full((db,), past_len, jnp.int32))
    xp, xs = x_prompt, x_sample
    outs_p, outs_s = [], []
    for l in range(depth):
        lw = _layer_weights(l, p)
        xp, *st_p = _prompt_layer(xp, lw, tabs_p)
        xs, *st_s = _sample_layer(xs, lw, tabs_s, l, cache_t, page_table, state_win[l], state_wkv[l],
                                  state_shift[l], state_conv[l])
        outs_p.append(st_p)
        outs_s.append(st_s)
    stack = lambda outs, i: jnp.stack([o[i] for o in outs])
    return (xp, xs, stack(outs_p, 0), stack(outs_s, 0), stack(outs_p, 1), stack(outs_s, 1),
            stack(outs_p, 2), stack(outs_s, 2), stack(outs_p, 3), stack(outs_s, 3),
            stack(outs_p, 4), stack(outs_s, 4))
```

```python
import functools

import jax
import jax.numpy as jnp
import numpy as np
from jax import lax
from jax.experimental import pallas as pl
from jax.experimental.pallas import tpu as pltpu

F32, BF16 = jnp.float32, jnp.bfloat16
HIGHEST = lax.Precision.HIGHEST

LANES = 128
VMEM_LIMIT = 56 * 1024 * 1024

HEAD_DIM = 64
NSA_HEADS = 8
KV_HEADS = 2
HEADS_PER_KV = NSA_HEADS // KV_HEADS
NSA_WIDTH = NSA_HEADS * HEAD_DIM
KV_WIDTH = KV_HEADS * HEAD_DIM
N_BRANCH = 3
CMP_STRIDE = 16
CMP_BLOCK = 2 * CMP_STRIDE
SEL_BLOCK = 64
SEL_TOPN = 16
WINDOW = 512
ROPE_THETA = 500000.0
ROPE_DIM = HEAD_DIM // 4
RWKV_HEAD = 64
RWKV_HEADS = 8
RWKV_WIDTH = RWKV_HEADS * RWKV_HEAD
DECAY_LORA = 64
AAA_LORA = 64
GATE_LORA = 128
RWKV_PROJ = 3 * RWKV_WIDTH + DECAY_LORA + AAA_LORA + GATE_LORA
NSA_PROJ = NSA_WIDTH + 6 * KV_WIDTH + N_BRANCH * NSA_HEADS
CONV_WIDTH = 3
NORM_EPS = 1e-6
LNX_EPS = 64e-5
NEG = -1e30
FORCED_SCORE = 1e9
ATTN_SCALE = HEAD_DIM ** -0.5
LOG2E = 1.4426950408889634


def _dot(a, b, precision=None):
    return jnp.dot(a, b, preferred_element_type=F32, precision=precision)


def _dot_nt(a, b, precision=None):
    return lax.dot_general(a, b, (((1,), (1,)), ((), ())), preferred_element_type=F32, precision=precision)


def _dot_tn(a, b, precision=None):
    return lax.dot_general(a, b, (((0,), (0,)), ((), ())), preferred_element_type=F32, precision=precision)


def _rms(x, g):
    return x * lax.rsqrt(jnp.mean(x * x, axis=-1, keepdims=True) + NORM_EPS) * g


def _masked_softmax(s, mask):
    s = jnp.where(mask, s, NEG)
    m = jnp.max(s, axis=-1, keepdims=True)
    e = jnp.where(mask, jnp.exp2(s - m), 0.0)
    return e / jnp.maximum(jnp.sum(e, axis=-1, keepdims=True), 1e-30)


def _params(*sem):
    return pltpu.CompilerParams(dimension_semantics=sem, vmem_limit_bytes=VMEM_LIMIT)


def _rope(v, cos, sa, sb):
    return v * cos + pltpu.roll(v, LANES - ROPE_DIM // 2, 1) * sa + pltpu.roll(v, ROPE_DIM // 2, 1) * sb


def _project(x_ref, g_ref, wq_ref, wkv_ref, wg_ref, wr_ref, tabs, kvw_ref, gate_ref, pr_ref):
    h = _rms(x_ref[...], g_ref[...]).astype(BF16)
    q = _dot(h, wq_ref[...])
    q = jnp.concatenate([_rope(q[:, j * LANES:(j + 1) * LANES], *tabs) for j in range(NSA_WIDTH // LANES)], axis=1)
    kv = _dot(h, wkv_ref[...])
    blocks = []
    for j in range(6 * KV_WIDTH // LANES):
        blk = kv[:, j * LANES:(j + 1) * LANES]
        blocks.append(_rope(blk, *tabs) if j % 2 == 0 else blk)
        kvw_ref[:, j * LANES:(j + 1) * LANES] = blocks[j]
    gate_ref[...] = jax.nn.sigmoid(_dot(h, wg_ref[...]))
    pr_ref[...] = _dot(h, wr_ref[...])
    return q, blocks


def _norm_proj_kernel(x_ref, g_ref, wq_ref, wkv_ref, wg_ref, wr_ref, cos_ref, sa_ref, sb_ref,
                      q_ref, kvw_ref, gate_ref, pr_ref):
    tabs = (cos_ref[...], sa_ref[...], sb_ref[...])
    q_ref[...], _ = _project(x_ref, g_ref, wq_ref, wkv_ref, wg_ref, wr_ref, tabs, kvw_ref, gate_ref, pr_ref)


def _norm_proj_heads_kernel(x_ref, g_ref, wq_ref, wkv_ref, wg_ref, wr_ref, cos_ref, sa_ref, sb_ref, own_ref,
                            kvw_ref, gate_ref, pr_ref, qh_ref, ksa_ref, kwh_ref, va_ref):
    tabs = (cos_ref[...], sa_ref[...], sb_ref[...])
    q, kv = _project(x_ref, g_ref, wq_ref, wkv_ref, wg_ref, wr_ref, tabs, kvw_ref, gate_ref, pr_ref)
    tm = q.shape[0]
    lane = lax.broadcasted_iota(jnp.int32, (tm, LANES), 1)
    low = lane < HEAD_DIM
    ones_lane = jnp.where(lane == HEAD_DIM, 1.0, 0.0)
    halves = lambda blk: (blk, pltpu.roll(blk, HEAD_DIM, 1))
    for j in range(NSA_WIDTH // LANES):
        for half, x in enumerate(halves(q[:, j * LANES:(j + 1) * LANES] * (ATTN_SCALE * LOG2E))):
            qh_ref[2 * j + half] = jnp.where(low, x, 0.0).astype(BF16)
    own = own_ref[...].astype(F32)
    for g, x in enumerate(halves(kv[2])):
        ksa_ref[g] = jnp.where(low, x, own).astype(BF16)
    for g, x in enumerate(halves(kv[4])):
        kwh_ref[g] = x[:, :HEAD_DIM].astype(BF16)
    for c, col in enumerate((3, 5)):
        for g, x in enumerate(halves(kv[col])):
            va_ref[c, g] = jnp.where(low, x, ones_lane).astype(BF16)


def _norm_proj(x2, g, wq, wkv, wg, wr, tabs, tm, heads_of=None):
    m, d = x2.shape
    cos, sa, sb = tabs
    tab_blocks = cos.shape[0] // tm
    row = lambda i: (i, 0)
    full = lambda i: (0, 0)
    tab = lambda i: (i % tab_blocks, 0)
    once = dict(pipeline_mode=pl.Buffered(1))
    in_specs = [pl.BlockSpec((tm, d), row), pl.BlockSpec((1, d), full),
                pl.BlockSpec(wq.shape, full, **once), pl.BlockSpec(wkv.shape, full, **once),
                pl.BlockSpec(wg.shape, full, **once), pl.BlockSpec(wr.shape, full, **once),
                pl.BlockSpec((tm, LANES), tab), pl.BlockSpec((tm, LANES), tab), pl.BlockSpec((tm, LANES), tab)]
    rows_out = [(6 * KV_WIDTH, F32), (KV_HEADS * LANES, F32), (RWKV_PROJ, F32)]
    if heads_of is None:
        rows_out = [(NSA_WIDTH, F32)] + rows_out
        return pl.pallas_call(
            _norm_proj_kernel, grid=(m // tm,), in_specs=in_specs,
            out_specs=[pl.BlockSpec((tm, w), row) for w, _ in rows_out],
            out_shape=[jax.ShapeDtypeStruct((m, w), dt) for w, dt in rows_out],
            compiler_params=_params("arbitrary"), name="norm_proj",
        )(x2, g, wq, wkv, wg, wr, cos, sa, sb)
    b, t, own = heads_of
    head_shapes = [(NSA_HEADS, LANES), (KV_HEADS, LANES), (KV_HEADS, HEAD_DIM), (2, KV_HEADS, LANES)]
    hspec = lambda lead, w: pl.BlockSpec((None,) + lead + (tm, w),
                                         lambda i: (i // tab_blocks,) + (0,) * len(lead) + (i % tab_blocks, 0))
    return pl.pallas_call(
        _norm_proj_heads_kernel, grid=(m // tm,), in_specs=in_specs + [pl.BlockSpec((tm, LANES), tab)],
        out_specs=[pl.BlockSpec((tm, w), row) for w, _ in rows_out] + [hspec(sh[:-1], sh[-1]) for sh in head_shapes],
        out_shape=[jax.ShapeDtypeStruct((m, w), dt) for w, dt in rows_out] +
                  [jax.ShapeDtypeStruct((b,) + sh[:-1] + (t, sh[-1]), BF16) for sh in head_shapes],
        compiler_params=_params("arbitrary"), name="norm_proj_heads",
    )(x2, g, wq, wkv, wg, wr, cos, sa, sb, own)


def _rope_tables(pos):
    half = ROPE_DIM // 2
    inv_freq = ROPE_THETA ** (-jnp.arange(half, dtype=F32) / half)
    ang = pos.astype(F32)[:, None] * inv_freq[None, :]
    c, s = jnp.cos(ang), jnp.sin(ang)
    n = pos.shape[0]
    one = jnp.ones((n, HEAD_DIM - ROPE_DIM), F32)
    zero = jnp.zeros((n, HEAD_DIM - ROPE_DIM), F32)
    zh = jnp.zeros((n, half), F32)
    cos = jnp.concatenate([c, c, one], axis=1)
    sa = jnp.concatenate([-s, zh, zero], axis=1)
    sb = jnp.concatenate([zh, s, zero], axis=1)
    rep = LANES // HEAD_DIM
    return tuple(jnp.tile(t, (1, rep)) for t in (cos, sa, sb))


CMP_LANES = 2 * KV_WIDTH


def _compress_accumulate(position_rows, n_chunk, pea_ref, peb_ref, wa_ref, wb_ref, acca_ref, accb_ref, row0,
                         alongside=None):
    for c in range(2):
        lanes = slice(c * KV_WIDTH, (c + 1) * KV_WIDTH)
        da = db = None
        for l in range(CMP_STRIDE):
            if c == 0 and alongside is not None:
                alongside(l)
            x = position_rows(c, l)
            xa = (x + pea_ref[c, l:l + 1, :]).astype(BF16)
            xb = (x + peb_ref[c, l:l + 1, :]).astype(BF16)
            ta, tb = _dot(xa, wa_ref[c, l]), _dot(xb, wb_ref[c, l])
            da, db = (ta, tb) if l == 0 else (da + ta, db + tb)
        acca_ref[pl.ds(row0, n_chunk), lanes] = da
        accb_ref[pl.ds(row0, n_chunk), lanes] = db


def _compress_finish(acca_ref, accb_ref, w2_ref, out_ref):
    n = acca_ref.shape[0]
    rows = lax.broadcasted_iota(jnp.int32, (n, 1), 0)
    for c in range(2):
        lanes = slice(c * KV_WIDTH, (c + 1) * KV_WIDTH)
        nxt = pltpu.roll(accb_ref[:, lanes], n - 1, 0)
        h = jax.nn.gelu(acca_ref[:, lanes] + nxt).astype(BF16)
        out_ref[0, :, lanes] = jnp.where(rows < n - 1, _dot(h, w2_ref[c]), 0.0)


def _compress_prompt_kernel(k_ref, v_ref, pea_ref, peb_ref, wa_ref, wb_ref, w2_ref, out_ref, acca_ref, accb_ref):
    n = acca_ref.shape[0]
    refs = (k_ref.at[0], v_ref.at[0])
    strided = lambda c, l: refs[c][pl.ds(l, n, stride=CMP_STRIDE), :]
    _compress_accumulate(strided, n, pea_ref, peb_ref, wa_ref, wb_ref, acca_ref, accb_ref, 0)
    _compress_finish(acca_ref, accb_ref, w2_ref, out_ref)


_CW_NAMES = ("pea", "peb", "wa", "wb", "w2")


def _compress_prompt(kvw3, cw):
    b, t, _ = kvw3.shape
    n = t // CMP_STRIDE
    const = lambda a: pl.BlockSpec(a.shape, lambda i: (0,) * a.ndim)
    return pl.pallas_call(
        _compress_prompt_kernel, grid=(b,),
        in_specs=[pl.BlockSpec((1, t, KV_WIDTH), lambda i: (i, 0, 0)),
                  pl.BlockSpec((1, t, KV_WIDTH), lambda i: (i, 0, 1))] + [const(cw[k]) for k in _CW_NAMES],
        out_specs=pl.BlockSpec((1, n, CMP_LANES), lambda i: (i, 0, 0)),
        out_shape=jax.ShapeDtypeStruct((b, n, CMP_LANES), F32),
        scratch_shapes=[pltpu.VMEM((n, CMP_LANES), F32), pltpu.VMEM((n, CMP_LANES), F32)],
        compiler_params=_params("arbitrary"), name="compress_prompt",
    )(kvw3, kvw3, *[cw[k] for k in _CW_NAMES])


def _compress_weights(pe, w1, w2):
    eye = jnp.eye(KV_HEADS, dtype=F32)
    blockdiag = lambda w: jnp.einsum('...de,gj->...gdje', w, eye).reshape(w.shape[:-2] + (KV_WIDTH, KV_WIDTH))
    lanes = lambda p: jnp.tile(p, (1, 1, KV_HEADS))
    return {"pea": lanes(pe[:, :CMP_STRIDE]), "peb": lanes(pe[:, CMP_STRIDE:]),
            "wa": blockdiag(w1[:, :CMP_STRIDE]).astype(BF16), "wb": blockdiag(w1[:, CMP_STRIDE:]).astype(BF16),
            "w2": blockdiag(w2).astype(BF16)}


def _paged_fetch(page_copies, pages_per_step, steps_per_seq):
    step = pl.program_id(0) * steps_per_seq + pl.program_id(1)
    total = pl.num_programs(0) * steps_per_seq

    def start_all(s):
        def body(j, c):
            for cp in page_copies(s, j):
                cp.start()
            return c
        lax.fori_loop(0, pages_per_step, body, 0)

    @pl.when(step == 0)
    def _():
        start_all(step)

    @pl.when(step + 1 < total)
    def _():
        start_all(step + 1)

    def wait_body(j, c):
        for cp in page_copies(step, j):
            cp.wait()
        return c
    lax.fori_loop(0, pages_per_step, wait_body, 0)
    return step % 2


UNROLL_PAGES = 8


def _page_of(pt_ref, step, j, pages_per_step, steps_per_seq):
    return pt_ref[step // steps_per_seq, (step % steps_per_seq) * pages_per_step + j]


def _compress_sample_kernel(layer, pages_per_step, steps_per_seq, pt_ref, cache_ref, pea_ref, peb_ref, wa_ref,
                            wb_ref, w2_ref, out_ref, buf_ref, sem_ref, rows_ref, acca_ref, accb_ref):
    page_len = cache_ref.shape[3]

    def page_copies(step, j):
        page = _page_of(pt_ref, step, j, pages_per_step, steps_per_seq)
        return [pltpu.make_async_copy(cache_ref.at[layer, page, pl.ds(0, CMP_LANES), :], buf_ref.at[step % 2, j],
                                      sem_ref.at[step % 2])]

    slot = _paged_fetch(page_copies, pages_per_step, steps_per_seq)

    chunks = page_len // CMP_STRIDE

    def to_rows(c, j):
        tok = buf_ref[slot, j, c * KV_WIDTH:(c + 1) * KV_WIDTH, :].T
        pos = jnp.swapaxes(tok.reshape(chunks, CMP_STRIDE, KV_WIDTH), 0, 1)
        rows_ref[c, :, pl.ds(pl.multiple_of(j * chunks, chunks), chunks), :] = pos

    def key_rows(j, carry):
        to_rows(0, j)
        return carry
    lax.fori_loop(0, pages_per_step, key_rows, 0, unroll=UNROLL_PAGES)

    per_l = pages_per_step // CMP_STRIDE

    def value_rows(l):
        for j in range(l * per_l, (l + 1) * per_l):
            to_rows(1, j)

    part = pl.program_id(1)
    n = rows_ref.shape[2]
    row0 = pl.multiple_of(part * n, n)
    _compress_accumulate(lambda c, l: rows_ref[c, l], n, pea_ref, peb_ref, wa_ref, wb_ref, acca_ref, accb_ref, row0,
                         alongside=value_rows)

    @pl.when(part == steps_per_seq - 1)
    def _():
        _compress_finish(acca_ref, accb_ref, w2_ref, out_ref)


def _compress_sample(cache_t, page_table, layer, cw, steps_per_seq=2):
    db, n_pages = page_table.shape
    page = cache_t.shape[3]
    pps = n_pages // steps_per_seq
    n = n_pages * page // CMP_STRIDE
    const = lambda a: pl.BlockSpec(a.shape, lambda i, p, pt: (0,) * a.ndim)
    assert pps % CMP_STRIDE == 0
    kern = functools.partial(_compress_sample_kernel, layer, pps, steps_per_seq)
    return pl.pallas_call(
        kern,
        grid_spec=pltpu.PrefetchScalarGridSpec(
            num_scalar_prefetch=1, grid=(db, steps_per_seq),
            in_specs=[pl.BlockSpec(memory_space=pl.ANY)] + [const(cw[k]) for k in _CW_NAMES],
            out_specs=pl.BlockSpec((1, n, CMP_LANES), lambda i, p, pt: (i, 0, 0)),
            scratch_shapes=[pltpu.VMEM((2, pps, CMP_LANES, page), F32), pltpu.SemaphoreType.DMA((2,)),
                            pltpu.VMEM((2, CMP_STRIDE, pps * page // CMP_STRIDE, KV_WIDTH), F32),
                            pltpu.VMEM((n, CMP_LANES), F32), pltpu.VMEM((n, CMP_LANES), F32)]),
        out_shape=jax.ShapeDtypeStruct((db, n, CMP_LANES), F32),
        compiler_params=_params("arbitrary", "arbitrary"), name="compress_sample",
    )(page_table, cache_t, *[cw[k] for k in _CW_NAMES])


def _cmp_to_sel(n_cmp_pad, n_cmp, n_sel_pad, n_sel):
    c0 = np.arange(n_cmp_pad)[:, None] * CMP_STRIDE
    s0 = np.arange(n_sel_pad)[None, :] * SEL_BLOCK
    hit = (c0 < s0 + SEL_BLOCK) & (c0 + CMP_BLOCK > s0)
    hit &= (np.arange(n_cmp_pad)[:, None] < n_cmp) & (np.arange(n_sel_pad)[None, :] < n_sel)
    return hit.astype(np.float32)


def _block_to_key(n_blk, n_key):
    return (np.arange(n_key)[None, :] // SEL_BLOCK == np.arange(n_blk)[:, None]).astype(np.float32)


def _nsa_prompt_kernel(qb, tk, q_ref, kc_ref, vc_ref, ks_ref, vs_ref, kw_ref, vw_ref, gate_ref, c2s_ref, hi_ref,
                       o_ref):
    i = pl.program_id(2)
    qs = pl.multiple_of(i * qb, qb)
    rows = HEADS_PER_KV * qb
    n_cmp = kc_ref.shape[0]
    n_sel = c2s_ref.shape[0]
    q_pad = q_ref[...].reshape(rows, 2 * HEAD_DIM)
    q = q_pad[:, :HEAD_DIM]
    qpos = qs + lax.broadcasted_iota(jnp.int32, (rows, 1), 0) % qb

    def biased(s, bias):
        return (s.reshape(HEADS_PER_KV, qb, s.shape[-1]) + bias[None]).reshape(s.shape)

    def weights(s, m):
        return jnp.exp2((s - m).astype(BF16))

    def normalised(acc):
        return acc[:, :HEAD_DIM] / jnp.maximum(acc[:, HEAD_DIM:HEAD_DIM + 1], 1e-30)

    s_cmp = _dot_nt(q, kc_ref[...])
    nw = WINDOW + qb
    w0 = pl.multiple_of(jnp.maximum(qs - WINDOW, 0), qb)
    s_win = _dot_nt(q, kw_ref[pl.ds(w0, nw), :])

    cmp_end = lax.broadcasted_iota(jnp.int32, (1, n_cmp), 1) * CMP_STRIDE + (CMP_BLOCK - 1)
    p = _masked_softmax(s_cmp, cmp_end <= qpos)
    o_cmp = _dot(p.astype(BF16), vc_ref[...])
    psum = p[0:qb]
    for h in range(1, HEADS_PER_KV):
        psum = psum + p[h * qb:(h + 1) * qb]
    imp_t = _dot_nt(c2s_ref[...], psum, precision=HIGHEST)

    rel = (qs + lax.broadcasted_iota(jnp.int32, (qb, nw), 0)) - (w0 + lax.broadcasted_iota(jnp.int32, (qb, nw), 1))
    s = biased(s_win, jnp.where((rel >= 0) & (rel < WINDOW), 0.0, NEG))
    o_win = normalised(_dot(weights(s, jnp.max(s, axis=-1, keepdims=True)), vw_ref[pl.ds(w0, nw), :]))

    blk = lax.broadcasted_iota(jnp.int32, (n_sel, qb), 0)
    qp = qs + lax.broadcasted_iota(jnp.int32, (n_sel, qb), 1)
    cur = qp // SEL_BLOCK
    forced = (blk == 0) | (blk == cur) | (blk == cur - 1)
    valid = blk <= cur
    score = jnp.where(valid, jnp.where(forced, FORCED_SCORE, imp_t), NEG)
    cnt = jnp.zeros((n_sel, qb), jnp.int32)
    for j in range(n_sel):
        sj = score[j:j + 1, :]
        ahead = (sj > score) | ((sj == score) & (j < blk))
        cnt = cnt + ahead.astype(jnp.int32)
    def masked_queries(unsel):
        spread = _dot_tn(unsel.astype(BF16), hi_ref[...])
        return (q_pad.astype(F32) + jnp.concatenate([spread] * HEADS_PER_KV, axis=0)).astype(BF16)

    unsel = jnp.where((cnt < SEL_TOPN) & valid, 0.0, 1.0)
    q_all = masked_queries(unsel)
    q_sel = masked_queries(jnp.where(blk < qs // SEL_BLOCK, unsel, 1.0))

    qrow = lax.broadcasted_iota(jnp.int32, (qb, qb), 0)
    kcol = lax.broadcasted_iota(jnp.int32, (qb, qb), 1)
    s = biased(_dot_nt(q_all, ks_ref[pl.ds(qs, qb), :]), jnp.where(kcol <= qrow, 0.0, NEG))
    m = jnp.max(s, axis=-1, keepdims=True)
    acc = _dot(weights(s, m), vs_ref[pl.ds(qs, qb), :])

    def sel_scores(kt):
        return _dot_nt(q_sel, ks_ref[pl.ds(pl.multiple_of(kt * tk, tk), tk), :])

    def sel_update(kt, s, m, acc):
        m_new = jnp.maximum(m, jnp.max(s, axis=-1, keepdims=True))
        acc = jnp.exp2(m - m_new) * acc + _dot(weights(s, m_new), vs_ref[pl.ds(pl.multiple_of(kt * tk, tk), tk), :])
        return m_new, acc

    def sel_pair(kp, carry):
        s0, s1 = sel_scores(2 * kp), sel_scores(2 * kp + 1)
        return sel_update(2 * kp + 1, s1, *sel_update(2 * kp, s0, *carry))

    def sel_single(kt, carry):
        return sel_update(kt, sel_scores(kt), *carry)

    n_kt = (qs + tk - 1) // tk
    carry = lax.fori_loop(0, n_kt // 2, sel_pair, (m, acc))
    _, acc = lax.fori_loop(n_kt // 2 * 2, n_kt, sel_single, carry)
    o_sel = normalised(acc)

    gt = gate_ref[...]
    g = [jnp.concatenate([gt[:, N_BRANCH * h + c:N_BRANCH * h + c + 1] for h in range(HEADS_PER_KV)], axis=0)
         for c in range(N_BRANCH)]
    o = (g[0] * o_cmp + g[1] * o_sel + g[2] * o_win).astype(o_ref.dtype)
    for h in range(HEADS_PER_KV):
        o_ref[:, h * HEAD_DIM:(h + 1) * HEAD_DIM] = o[h * qb:(h + 1) * qb]


def _nsa_prompt(qh, ksa, kw, va, cvh, gates_h, qb=256, tk=512):
    b, _, t, _ = qh.shape
    n_cmp = cvh.shape[3]
    n_sel = t // SEL_BLOCK
    assert qb % SEL_BLOCK == 0 and t % tk == 0 and WINDOW % qb == 0 and WINDOW + qb <= t and n_sel <= HEAD_DIM
    c2s = jnp.asarray(_cmp_to_sel(n_cmp, n_cmp - 1, n_sel, n_sel).T)
    hi = jnp.asarray(np.concatenate([np.zeros((n_sel, HEAD_DIM)), np.eye(n_sel, HEAD_DIM)], axis=1), dtype=BF16)
    vspec = lambda c: pl.BlockSpec((None, None, None, t, 2 * HEAD_DIM), lambda bi, g, i: (bi, c, g, 0, 0))
    ccomp = lambda c: pl.BlockSpec((None, None, None, n_cmp, HEAD_DIM), lambda bi, g, i: (bi, c, g, 0, 0))
    qspec = pl.BlockSpec((None, HEADS_PER_KV, qb, 2 * HEAD_DIM), lambda bi, g, i: (bi, g, i, 0))
    ospec = pl.BlockSpec((None, qb, HEADS_PER_KV * HEAD_DIM), lambda bi, g, i: (bi, i, g))
    kern = functools.partial(_nsa_prompt_kernel, qb, tk)
    return pl.pallas_call(
        kern, grid=(b, KV_HEADS, t // qb),
        in_specs=[qspec, ccomp(0), ccomp(1),
                  pl.BlockSpec((None, None, t, 2 * HEAD_DIM), lambda bi, g, i: (bi, g, 0, 0)), vspec(0),
                  pl.BlockSpec((None, None, t, HEAD_DIM), lambda bi, g, i: (bi, g, 0, 0)), vspec(1),
                  pl.BlockSpec((None, qb, LANES), lambda bi, g, i: (bi, i, g)),
                  pl.BlockSpec(c2s.shape, lambda bi, g, i: (0, 0)), pl.BlockSpec(hi.shape, lambda bi, g, i: (0, 0))],
        out_specs=ospec,
        out_shape=jax.ShapeDtypeStruct((b, t, NSA_WIDTH), BF16),
        compiler_params=_params("arbitrary", "arbitrary", "arbitrary"), name="nsa_prompt",
    )(qh, cvh, cvh, ksa, va, kw, va, gates_h, c2s, hi)


SROWS = 8 * KV_HEADS


def _sample_cmp_win_kernel(past_len, n_cmp, qz_ref, kvc_ref, win_ref, wnew_ref, c2s_ref, ocmp_ref, owin_ref,
                           imp_ref):
    qz = qz_ref[0]
    kvc = kvc_ref[0]
    n_pad = kvc.shape[0]
    kc, vc = kvc[:, :KV_WIDTH].astype(BF16), kvc[:, KV_WIDTH:].astype(BF16)
    s = _dot_nt(qz, kc)
    n = lax.broadcasted_iota(jnp.int32, (1, n_pad), 1)
    p = _masked_softmax(s, (n < n_cmp) & (n * CMP_STRIDE + (CMP_BLOCK - 1) <= past_len))
    ocmp_ref[0] = _dot(p.astype(BF16), vc)
    r = lax.broadcasted_iota(jnp.int32, (SROWS, 1), 0)
    p = jnp.where(r % 8 < HEADS_PER_KV, p, 0.0)
    psum = jnp.concatenate([jnp.sum(p[8 * g:8 * g + 8], axis=0, keepdims=True) for g in range(KV_HEADS)] +
                           [jnp.zeros((8 - KV_HEADS, n_pad), F32)], axis=0)
    imp_ref[0] = _dot(psum, c2s_ref[...], precision=HIGHEST)

    win = win_ref[0]
    wb = win.shape[1]
    kw, vw = win[:KV_WIDTH].astype(BF16), win[KV_WIDTH:].astype(BF16)
    wnew = wnew_ref[0]
    s = _dot(qz, kw)
    s_new = jnp.sum(qz.astype(F32) * wnew[:, :KV_WIDTH].astype(BF16).astype(F32), axis=-1, keepdims=True)
    rel = wb - lax.broadcasted_iota(jnp.int32, (1, wb), 1)
    msk = (rel >= 0) & (rel < WINDOW)
    s = jnp.where(msk, s, NEG)
    m = jnp.maximum(jnp.max(s, axis=-1, keepdims=True), s_new)
    e = jnp.where(msk, jnp.exp2(s - m), 0.0)
    e_new = jnp.exp2(s_new - m)
    den = jnp.maximum(jnp.sum(e, axis=-1, keepdims=True) + e_new, 1e-30)
    pv = _dot_nt((e / den).astype(BF16), vw)
    pn = (e_new / den).astype(BF16).astype(F32)
    owin_ref[0] = pv + pn * wnew[:, KV_WIDTH:].astype(BF16).astype(F32)


def _sample_cmp_win(past_len, n_cmp, qz, kvc, win, wnew, c2s):
    db = qz.shape[0]
    blk3 = lambda a: pl.BlockSpec((1,) + a.shape[1:], lambda i: (i, 0, 0))
    kern = functools.partial(_sample_cmp_win_kernel, past_len, n_cmp)
    outs = (jax.ShapeDtypeStruct((db, SROWS, KV_WIDTH), F32), jax.ShapeDtypeStruct((db, SROWS, KV_WIDTH), F32),
            jax.ShapeDtypeStruct((db, 8, c2s.shape[1]), F32))
    return pl.pallas_call(
        kern, grid=(db,),
        in_specs=[blk3(qz), blk3(kvc), blk3(win), blk3(wnew), pl.BlockSpec(c2s.shape, lambda i: (0, 0))],
        out_specs=[pl.BlockSpec((1, SROWS, KV_WIDTH), lambda i: (i, 0, 0)),
                   pl.BlockSpec((1, SROWS, KV_WIDTH), lambda i: (i, 0, 0)),
                   pl.BlockSpec((1, 8, c2s.shape[1]), lambda i: (i, 0, 0))],
        out_shape=outs, compiler_params=_params("arbitrary"), name="sample_cmp_win",
    )(qz, kvc, win, wnew, c2s)


def _select_kernel(n_sel, cur, imp_ref, sel_ref):
    imp = imp_ref[...]
    blk = lax.broadcasted_iota(jnp.int32, imp.shape, 1)
    forced = (blk == 0) | (blk == cur) | (blk == cur - 1)
    valid = blk <= cur
    score = jnp.where(valid, jnp.where(forced, FORCED_SCORE, imp), NEG)

    def body(j, cnt):
        sj = jnp.sum(jnp.where(blk == j, score, 0.0), axis=-1, keepdims=True)
        ahead = (sj > score) | ((sj == score) & (j < blk))
        return cnt + ahead.astype(jnp.int32)

    cnt = lax.fori_loop(0, n_sel, body, jnp.zeros(imp.shape, jnp.int32))
    sel_ref[...] = jnp.where((cnt < SEL_TOPN) & valid, 1.0, 0.0)


def _select(imp2, n_sel, cur):
    return pl.pallas_call(
        functools.partial(_select_kernel, n_sel, cur),
        out_shape=jax.ShapeDtypeStruct(imp2.shape, F32), name="sample_select",
        compiler_params=pltpu.CompilerParams(vmem_limit_bytes=VMEM_LIMIT),
    )(imp2)


def _sample_sel_kernel(layer, pages_per_step, steps_per_seq, pt_ref, cache_ref, qz_ref, sel_ref, selnew_ref,
                       knew_ref, b2k_ref, ocmp_ref, owin_ref, gate_ref, o_ref, buf_ref, sem_ref, m_ref, l_ref,
                       acc_ref):
    page_len = cache_ref.shape[3]

    def page_copies(step, j):
        page = _page_of(pt_ref, step, j, pages_per_step, steps_per_seq)
        return [pltpu.make_async_copy(cache_ref.at[layer, page, pl.ds(CMP_LANES, CMP_LANES), :],
                                      buf_ref.at[step % 2, :, pl.ds(pl.multiple_of(j * page_len, page_len), page_len)],
                                      sem_ref.at[step % 2])]

    slot = _paged_fetch(page_copies, pages_per_step, steps_per_seq)
    part = pl.program_id(1)
    qz = qz_ref[0]

    @pl.when(part == 0)
    def _():
        m_ref[...] = jnp.full(m_ref.shape, NEG, F32)
        l_ref[...] = jnp.zeros(l_ref.shape, F32)
        acc_ref[...] = jnp.zeros(acc_ref.shape, F32)

    kv = buf_ref[slot].astype(BF16)
    s = _dot(qz, kv[:KV_WIDTH])
    msk = _dot(sel_ref[0].astype(BF16), b2k_ref[...]) > 0.5
    s = jnp.where(msk, s, NEG)
    m_old = m_ref[...]
    m_new = jnp.maximum(m_old, jnp.max(s, axis=-1, keepdims=True))
    alpha = jnp.exp2(m_old - m_new)
    e = jnp.where(msk, jnp.exp2(s - m_new), 0.0)
    l_ref[...] = alpha * l_ref[...] + jnp.sum(e, axis=-1, keepdims=True)
    acc_ref[...] = alpha * acc_ref[...] + _dot_nt(e.astype(BF16), kv[KV_WIDTH:])
    m_ref[...] = m_new

    @pl.when(part == steps_per_seq - 1)
    def _():
        knew = knew_ref[0].astype(BF16).astype(F32)
        s_new = jnp.sum(qz.astype(F32) * knew[:, :KV_WIDTH], axis=-1, keepdims=True)
        on = selnew_ref[0][:, 0:1] > 0.5
        s_new = jnp.where(on, s_new, NEG)
        m_old = m_ref[...]
        m_new = jnp.maximum(m_old, s_new)
        alpha = jnp.exp2(m_old - m_new)
        e_new = jnp.where(on, jnp.exp2(s_new - m_new), 0.0)
        l = alpha * l_ref[...] + e_new
        acc = alpha * acc_ref[...] + e_new.astype(BF16).astype(F32) * knew[:, KV_WIDTH:]
        o_sel = acc / jnp.maximum(l, 1e-30)
        g = gate_ref[0]
        o_ref[0] = g[:, 0:1] * ocmp_ref[0] + g[:, 1:2] * o_sel + g[:, 2:3] * owin_ref[0]


def _sample_sel(cache_t, page_table, layer, qz, sel16, knew, ocmp, owin, gz, steps_per_seq=2):
    db, n_pages = page_table.shape
    page = cache_t.shape[3]
    pps = n_pages // steps_per_seq
    keys = pps * page
    blocks = keys // SEL_BLOCK
    b2k = jnp.asarray(_block_to_key(blocks, keys), dtype=BF16)
    blk3 = lambda a: pl.BlockSpec((1,) + a.shape[1:], lambda i, p, pt: (i, 0, 0))
    kern = functools.partial(_sample_sel_kernel, layer, pps, steps_per_seq)
    return pl.pallas_call(
        kern,
        grid_spec=pltpu.PrefetchScalarGridSpec(
            num_scalar_prefetch=1, grid=(db, steps_per_seq),
            in_specs=[pl.BlockSpec(memory_space=pl.ANY), blk3(qz),
                      pl.BlockSpec((1, SROWS, blocks), lambda i, p, pt: (i, 0, p)),
                      pl.BlockSpec((1, SROWS, LANES), lambda i, p, pt: (i, 0, steps_per_seq * blocks // LANES)),
                      blk3(knew),
                      pl.BlockSpec(b2k.shape, lambda i, p, pt: (0, 0)), blk3(ocmp), blk3(owin), blk3(gz)],
            out_specs=pl.BlockSpec((1, SROWS, KV_WIDTH), lambda i, p, pt: (i, 0, 0)),
            scratch_shapes=[pltpu.VMEM((2, CMP_LANES, keys), F32), pltpu.SemaphoreType.DMA((2,)),
                            pltpu.VMEM((SROWS, 1), F32), pltpu.VMEM((SROWS, 1), F32),
                            pltpu.VMEM((SROWS, KV_WIDTH), F32)]),
        out_shape=jax.ShapeDtypeStruct((db, SROWS, KV_WIDTH), F32),
        compiler_params=_params("arbitrary", "arbitrary"), name="sample_sel",
    )(page_table, cache_t, qz, sel16, sel16, knew, b2k, ocmp, owin, gz)


def _bdot(dot, a, b):
    return dot(a.astype(BF16), b.astype(BF16))


def _head_sum(x, hm):
    xh = x.astype(BF16)
    xl = (x - xh.astype(F32)).astype(BF16)
    return _dot(xh, hm) + _dot(xl, hm)


def _softplus(z):
    return jnp.maximum(z, 0.0) + jnp.log(1.0 + jnp.exp(-jnp.abs(z)))


CHUNK_GROUP = 8


def _rwkv_kernel(chunk, n_valid, pr_ref, prev_ref, s0_ref, mu_ref, w0_ref, wup_ref, a0_ref, aup_ref, gup_ref,
                 kk_ref, ka_ref, rk_ref, lnw_ref, lnb_ref, hm_ref, tri_ref, out_ref, sout_ref,
                 carry_ref, state_ref, r_ref, k_ref, v_ref, lw_ref, a_ref, b_ref, rh_ref, y_ref, m_ref, n_ref, et_ref):
    ti = pl.program_id(1)
    tt = pr_ref.shape[1]
    w_ = RWKV_WIDTH
    hd = RWKV_HEAD

    @pl.when(ti == 0)
    def _():
        carry_ref[...] = prev_ref[0]
        state_ref[...] = s0_ref[0]

    p = pr_ref[0]
    row = lax.broadcasted_iota(jnp.int32, (tt, 1), 0)
    shifted = jnp.where(row == 0, carry_ref[...], pltpu.roll(p, 1, 0))
    carry_ref[...] = p[tt - 1:tt]
    xs = p + (shifted - p) * mu_ref[...]
    r, k, v = xs[:, :w_], xs[:, w_:2 * w_], xs[:, 2 * w_:3 * w_]
    wa = xs[:, 3 * w_:3 * w_ + DECAY_LORA + AAA_LORA]
    gl = xs[:, 3 * w_ + DECAY_LORA + AAA_LORA:]
    hm = hm_ref[...]
    w = -_softplus(-(w0_ref[...] + _dot(jnp.tanh(wa).astype(BF16), wup_ref[...]))) - 0.5
    lw = -jnp.exp(w)
    a = jax.nn.sigmoid(a0_ref[...] + _dot(wa.astype(BF16), aup_ref[...]))
    g = _dot(jax.nn.sigmoid(gl).astype(BF16), gup_ref[...])
    kk = k * kk_ref[...]
    kk = kk / jnp.maximum(jnp.sqrt(_head_sum(kk * kk, hm)), 1e-12)
    k = k * (1.0 + (a - 1.0) * ka_ref[...])
    av, bv = -kk, kk * a
    if n_valid < tt:
        keep = row < n_valid
        lw, k, v, av, bv = (jnp.where(keep, t, 0.0) for t in (lw, k, v, av, bv))
    r_ref[...], k_ref[...], v_ref[...], lw_ref[...], a_ref[...], b_ref[...] = r, k, v, lw, av, bv

    ci = lax.broadcasted_iota(jnp.int32, (chunk, chunk), 0)
    cj = lax.broadcasted_iota(jnp.int32, (chunk, chunk), 1)
    strict, incl = ci > cj, ci >= cj
    dot = functools.partial(_bdot, _dot)
    dot_nt = functools.partial(_bdot, _dot_nt)
    dot_tn = functools.partial(_bdot, _dot_tn)
    group = min(CHUNK_GROUP, tt // chunk)

    def prepare(cg, carry):
        jobs, pre = [], []
        for ci in range(group):
            c = cg * group + ci
            rows = pl.ds(pl.multiple_of(c * chunk, chunk), chunk)
            lwc = lw_ref[rows, :]
            cs = _dot(tri_ref[...], lwc, precision=HIGHEST)
            e_in, e_out = jnp.exp(cs), jnp.exp(-cs)
            total = cs[chunk - 1:chunk]
            e_end = jnp.exp(total - cs)
            kc, bc = k_ref[rows, :], b_ref[rows, :]
            pre.append(dict(c=c, rows=rows, rt=r_ref[rows, :] * e_in, at=a_ref[rows, :] * jnp.exp(cs - lwc),
                            kt=kc * e_out, bt=bc * e_out, kh=kc * e_end, bh=bc * e_end, vc=v_ref[rows, :]))
            et_ref[c] = jnp.broadcast_to(jnp.exp(total), et_ref.shape[1:])
            jobs += [(ci, h, slice(h * hd, (h + 1) * hd)) for h in range(RWKV_HEADS)]
        nj = range(len(jobs))
        part = lambda name: [pre[ci][name][:, sl] for ci, _, sl in jobs]
        at, rt, bt, kt, bh, kh, vh = (part(n_) for n_ in ("at", "rt", "bt", "kt", "bh", "kh", "vc"))
        gm = [dot_nt(jnp.concatenate([at[j], rt[j]], axis=0), jnp.concatenate([bt[j], kt[j]], axis=0))
              for j in nj]
        lab = [jnp.where(strict, g_[:chunk, :chunk], 0.0) for g_ in gm]
        lak = [jnp.where(strict, g_[:chunk, chunk:], 0.0) for g_ in gm]
        mrb = [jnp.where(incl, g_[chunk:, :chunk], 0.0) for g_ in gm]
        mrk = [jnp.where(incl, g_[chunk:, chunk:], 0.0) for g_ in gm]
        wm = [jnp.concatenate([at[j], dot(lak[j], vh[j])], axis=1) for j in nj]
        x, n = lab, 1
        while 2 * n < chunk:
            xw = [dot(x[j], jnp.concatenate([wm[j], x[j]], axis=1)) for j in nj]
            x = [t[:, 2 * hd:] for t in xw]
            wm = [wm[j] + xw[j][:, :2 * hd] for j in nj]
            n *= 2
        wm = [wm[j] + dot(x[j], wm[j]) for j in nj]
        mw = [dot(mrb[j], wm[j]) for j in nj]
        y0 = [mw[j][:, hd:] + dot(mrk[j], vh[j]) for j in nj]
        wb = [dot_tn(wm[j], bh[j]) for j in nj]
        vk = [dot_tn(vh[j], kh[j]) for j in nj]
        for j, (ci, h, sl) in enumerate(jobs):
            c, rows = pre[ci]["c"], pre[ci]["rows"]
            rh_ref[rows, sl] = rt[j] + mw[j][:, :hd]
            y_ref[rows, sl] = y0[j]
            m_ref[c, h] = wb[j][:hd]
            n_ref[c, h] = wb[j][hd:] + vk[j]
        return carry

    def advance(c, carry):
        rows = pl.ds(pl.multiple_of(c * chunk, chunk), chunk)
        e_tot = et_ref[c]
        sls = [slice(h * hd, (h + 1) * hd) for h in range(RWKV_HEADS)]
        s0 = [state_ref[h] for h in range(RWKV_HEADS)]
        ys = [dot_nt(rh_ref[rows, sl], s0[h]) for h, sl in enumerate(sls)]
        sm = [dot(s0[h], m_ref[c, h]) for h in range(RWKV_HEADS)]
        for h, sl in enumerate(sls):
            y_ref[rows, sl] += ys[h]
            state_ref[h] = s0[h] * e_tot[0:1, sl] + sm[h] + n_ref[c, h]
        return carry

    lax.fori_loop(0, tt // chunk // group, prepare, 0)
    lax.fori_loop(0, tt // chunk, advance, 0, unroll=True)

    y = y_ref[...]
    inv_n = 1.0 / hd
    mean = _head_sum(y, hm) * inv_n
    d = y - mean
    var = _head_sum(d * d, hm) * inv_n
    y = d * lax.rsqrt(var + LNX_EPS) * lnw_ref[...] + lnb_ref[...]
    y = y + _head_sum(r_ref[...] * k_ref[...] * rk_ref[...], hm) * v_ref[...]
    out_ref[0] = (y * g).astype(out_ref.dtype)
    sout_ref[0] = state_ref[...]


def _rwkv(pr3, prev, s0, rw, chunk, tt, n_valid):
    b, t, _ = pr3.shape
    hm = jnp.asarray(np.kron(np.eye(RWKV_HEADS), np.ones((RWKV_HEAD, RWKV_HEAD))), dtype=BF16)
    tri = jnp.asarray(np.tril(np.ones((chunk, chunk))), dtype=F32)
    vec = lambda a: pl.BlockSpec(a.shape, lambda i, j: (0, 0))
    prev3 = prev[:, None, :]
    names = ("mu", "w0", "wup", "a0", "aup", "gup", "kk", "ka", "rk", "lnw", "lnb")
    kern = functools.partial(_rwkv_kernel, chunk, n_valid)
    per_chunk = (tt // chunk, RWKV_HEADS, RWKV_HEAD, RWKV_HEAD)
    return pl.pallas_call(
        kern, grid=(b, t // tt),
        in_specs=[pl.BlockSpec((1, tt, RWKV_PROJ), lambda i, j: (i, j, 0)),
                  pl.BlockSpec((1, 1, RWKV_PROJ), lambda i, j: (i, 0, 0)),
                  pl.BlockSpec((1, RWKV_HEADS, RWKV_HEAD, RWKV_HEAD), lambda i, j: (i, 0, 0, 0))] +
                 [vec(rw[n]) for n in names] + [vec(hm), vec(tri)],
        out_specs=[pl.BlockSpec((1, tt, RWKV_WIDTH), lambda i, j: (i, j, 0)),
                   pl.BlockSpec((1, RWKV_HEADS, RWKV_HEAD, RWKV_HEAD), lambda i, j: (i, 0, 0, 0))],
        out_shape=(jax.ShapeDtypeStruct((b, t, RWKV_WIDTH), BF16),
                   jax.ShapeDtypeStruct((b, RWKV_HEADS, RWKV_HEAD, RWKV_HEAD), F32)),
        scratch_shapes=[pltpu.VMEM((1, RWKV_PROJ), F32), pltpu.VMEM((RWKV_HEADS, RWKV_HEAD, RWKV_HEAD), F32)] +
                       [pltpu.VMEM((tt, RWKV_WIDTH), F32) for _ in range(8)] +
                       [pltpu.VMEM(per_chunk, F32), pltpu.VMEM(per_chunk, F32),
                        pltpu.VMEM((tt // chunk, 8, RWKV_WIDTH), F32)],
        compiler_params=_params("arbitrary", "arbitrary"), name="rwkv",
    )(pr3, prev3, s0, *[rw[n] for n in names], hm, tri)


def _ffn_head(x_ref, nsa_ref, rwkv_ref, wo_ref, gpost_ref, gpre_ref, x1_ref, h2_ref, acc_ref):
    n = nsa_ref.shape[1]
    mixed = _dot(nsa_ref[...], wo_ref[:n, :]) + _dot(rwkv_ref[...], wo_ref[n:, :])
    x1 = x_ref[...] + _rms(mixed, gpost_ref[...])
    x1_ref[...] = x1
    h2_ref[...] = _rms(x1, gpre_ref[...]).astype(BF16)
    acc_ref[...] = jnp.zeros(acc_ref.shape, F32)


def _ffn_tail(gate2, gate1, gate, val, cw_ref, cb_ref, wd_ref, acc_ref):
    conv = cb_ref[...] + (gate2 * cw_ref[0:1, :] + gate1 * cw_ref[1:2, :] + gate * cw_ref[2:3, :])
    acc_ref[...] += _dot((jax.nn.gelu(conv) * val).astype(BF16), wd_ref[...])


FFN_COLS = 256


def _ffn_seq_kernel(tiles_per_seq, x_ref, nsa_ref, rwkv_ref, wo_ref, gpost_ref, gpre_ref, gfpost_ref, wg_ref, wv_ref, cw_ref,
                    cb_ref, wd_ref, out_ref, tail_ref, x1_ref, h2_ref, acc_ref, carry_ref):
    i, j = pl.program_id(0), pl.program_id(1)
    tm = x_ref.shape[0]

    @pl.when(j == 0)
    def _():
        _ffn_head(x_ref, nsa_ref, rwkv_ref, wo_ref, gpost_ref, gpre_ref, x1_ref, h2_ref, acc_ref)

    @pl.when(i % tiles_per_seq == 0)
    def _():
        carry_ref[j] = jnp.zeros(carry_ref.shape[1:], F32)

    h2 = h2_ref[...]
    row = lax.broadcasted_iota(jnp.int32, (tm, 1), 0)
    tf = wg_ref.shape[1]
    cols = [slice(c0, min(c0 + FFN_COLS, tf)) for c0 in range(0, tf, FFN_COLS)]
    up = lambda c: (_dot(h2, wg_ref[:, c]), _dot(h2, wv_ref[:, c]))
    nxt = up(cols[0])
    for k, c in enumerate(cols):
        gate, val = nxt
        if k + 1 < len(cols):
            nxt = up(cols[k + 1])
        prev = carry_ref[j, :, c]
        gate1 = jnp.where(row == 0, prev[7:8], pltpu.roll(gate, 1, 0))
        gate2 = jnp.where(row == 0, prev[6:7], jnp.where(row == 1, prev[7:8], pltpu.roll(gate, 2, 0)))
        carry_ref[j, :, c] = gate[tm - 8:tm]
        tail_ref[0, :, c] = gate[tm - 8:tm]
        conv = cb_ref[:, c] + (gate2 * cw_ref[0:1, c] + gate1 * cw_ref[1:2, c] + gate * cw_ref[2:3, c])
        acc_ref[...] += _dot((jax.nn.gelu(conv) * val).astype(BF16), wd_ref[c, :])

    @pl.when(j == pl.num_programs(1) - 1)
    def _():
        out_ref[...] = x1_ref[...] + _rms(acc_ref[...], gfpost_ref[...])


def _ffn_step_kernel(x_ref, nsa_ref, rwkv_ref, wo_ref, gpost_ref, gpre_ref, gfpost_ref, wg_ref, wv_ref, cw_ref, cb_ref,
                     wd_ref, prev_ref, out_ref, gate_ref, x1_ref, h2_ref, acc_ref):
    j = pl.program_id(0)

    @pl.when(j == 0)
    def _():
        _ffn_head(x_ref, nsa_ref, rwkv_ref, wo_ref, gpost_ref, gpre_ref, x1_ref, h2_ref, acc_ref)

    h2 = h2_ref[...]
    gate, val = _dot(h2, wg_ref[...]), _dot(h2, wv_ref[...])
    gate_ref[...] = gate
    _ffn_tail(prev_ref[0], prev_ref[1], gate, val, cw_ref, cb_ref, wd_ref, acc_ref)

    @pl.when(j == pl.num_programs(0) - 1)
    def _():
        out_ref[...] = x1_ref[...] + _rms(acc_ref[...], gfpost_ref[...])


def _ffn_seq(x2, nsa2, rwkv2, fw, seq_len, tm, tf):
    m, d = x2.shape
    d_ff = fw["wd"].shape[0]
    nf = d_ff // tf
    row = lambda i, j: (i, 0)
    full = lambda i, j: (0, 0)
    tiles_per_seq = seq_len // tm
    scratch = [pltpu.VMEM((tm, d), F32), pltpu.VMEM((tm, d), BF16), pltpu.VMEM((tm, d), F32),
               pltpu.VMEM((nf, 8, tf), F32)]
    once = dict(pipeline_mode=pl.Buffered(1)) if nf == 1 else {}
    return pl.pallas_call(
        functools.partial(_ffn_seq_kernel, tiles_per_seq), grid=(m // tm, nf),
        in_specs=[pl.BlockSpec((tm, d), row), pl.BlockSpec((tm, nsa2.shape[1]), row),
                  pl.BlockSpec((tm, rwkv2.shape[1]), row), pl.BlockSpec((d, d), full, **once),
                  pl.BlockSpec((1, d), full), pl.BlockSpec((1, d), full), pl.BlockSpec((1, d), full),
                  pl.BlockSpec((d, tf), lambda i, j: (0, j), **once),
                  pl.BlockSpec((d, tf), lambda i, j: (0, nf + j), **once),
                  pl.BlockSpec((CONV_WIDTH, tf), lambda i, j: (0, j)), pl.BlockSpec((1, tf), lambda i, j: (0, j)),
                  pl.BlockSpec((tf, d), lambda i, j: (j, 0), **once)],
        out_specs=[pl.BlockSpec((tm, d), row), pl.BlockSpec((1, 8, tf), lambda i, j: (i, 0, j))],
        out_shape=(jax.ShapeDtypeStruct((m, d), F32), jax.ShapeDtypeStruct((m // tm, 8, d_ff), F32)),
        scratch_shapes=scratch, compiler_params=_params("arbitrary", "arbitrary"), name="ffn_seq",
    )(x2, nsa2, rwkv2, fw["wo"], fw["gpost"], fw["gpre"], fw["gfpost"], fw["wup"], fw["wup"], fw["cw"], fw["cb"],
      fw["wd"])


def _ffn_step(x2, nsa2, rwkv2, fw, prev, tf):
    m, d = x2.shape
    d_ff = fw["wd"].shape[0]
    nf = d_ff // tf
    full = lambda j: (0, 0)
    scratch = [pltpu.VMEM((m, d), F32), pltpu.VMEM((m, d), BF16), pltpu.VMEM((m, d), F32)]
    return pl.pallas_call(
        _ffn_step_kernel, grid=(nf,),
        in_specs=[pl.BlockSpec((m, d), full), pl.BlockSpec(nsa2.shape, full), pl.BlockSpec(rwkv2.shape, full),
                  pl.BlockSpec((d, d), full),
                  pl.BlockSpec((1, d), full), pl.BlockSpec((1, d), full), pl.BlockSpec((1, d), full),
                  pl.BlockSpec((d, tf), lambda j: (0, j)), pl.BlockSpec((d, tf), lambda j: (0, nf + j)),
                  pl.BlockSpec((CONV_WIDTH, tf), lambda j: (0, j)), pl.BlockSpec((1, tf), lambda j: (0, j)),
                  pl.BlockSpec((tf, d), lambda j: (j, 0)),
                  pl.BlockSpec((CONV_WIDTH - 1, m, tf), lambda j: (0, 0, j))],
        out_specs=[pl.BlockSpec((m, d), full), pl.BlockSpec((m, tf), lambda j: (0, j))],
        out_shape=(jax.ShapeDtypeStruct((m, d), F32), jax.ShapeDtypeStruct((m, d_ff), F32)),
        scratch_shapes=scratch, compiler_params=_params("arbitrary"), name="ffn_step",
    )(x2, nsa2, rwkv2, fw["wo"], fw["gpost"], fw["gpre"], fw["gfpost"], fw["wup"], fw["wup"], fw["cw"], fw["cb"],
      fw["wd"], prev)


def _layer_weights(l, p):
    w_in = p["w_in"][l]
    o = NSA_WIDTH + 6 * KV_WIDTH
    per_group = N_BRANCH * HEADS_PER_KV
    wg = jnp.pad(w_in[:, o:NSA_PROJ].reshape(-1, KV_HEADS, per_group), ((0, 0), (0, 0), (0, LANES - per_group)))
    wg = wg.reshape(-1, KV_HEADS * LANES)
    row = lambda a: a[l][None, :].astype(F32)
    zpad = jnp.zeros((AAA_LORA, RWKV_WIDTH), F32)
    rw = {"mu": row(p["rwkv_mu"]), "w0": row(p["rwkv_w0"]),
          "wup": jnp.concatenate([p["rwkv_w_up"][l], zpad], axis=0).astype(BF16),
          "a0": row(p["rwkv_a0"]), "aup": jnp.concatenate([zpad, p["rwkv_a_up"][l]], axis=0).astype(BF16),
          "gup": p["rwkv_g_up"][l].astype(BF16), "kk": row(p["rwkv_k_k"]), "ka": row(p["rwkv_k_a"]),
          "rk": p["rwkv_r_k"][l].reshape(1, RWKV_WIDTH), "lnw": row(p["rwkv_lnx_w"]), "lnb": row(p["rwkv_lnx_b"])}
    fw = {"wo": p["w_out"][l].astype(BF16), "gpost": row(p["norm_mix_post"]), "gpre": row(p["norm_ffn_pre"]),
          "gfpost": row(p["norm_ffn_post"]), "wup": p["ffn_w_up"][l].astype(BF16), "cw": p["ffn_conv_w"][l],
          "cb": row(p["ffn_conv_b"]), "wd": p["ffn_w_down"][l].astype(BF16)}
    return {"g_in": row(p["norm_mix_pre"]), "wq": w_in[:, :NSA_WIDTH].astype(BF16),
            "wkv": w_in[:, NSA_WIDTH:o].astype(BF16), "wg": wg.astype(BF16), "wr": w_in[:, NSA_PROJ:].astype(BF16),
            "cw": _compress_weights(p["cmp_pe"][l], p["cmp_w1"][l], p["cmp_w2"][l]), "rw": rw, "fw": fw}


def _prompt_layer(x, lw, tabs):
    b, t, d = x.shape
    m = b * t
    x2 = x.reshape(m, d)
    own_block = jnp.asarray(np.concatenate([np.zeros((t, HEAD_DIM)), NEG * _block_to_key(HEAD_DIM, t).T], axis=1),
                            dtype=BF16)
    kvw, gates, pr, qh, ksa, kw, va = _norm_proj(x2, lw["g_in"], lw["wq"], lw["wkv"], lw["wg"], lw["wr"], tabs,
                                                 tm=512, heads_of=(b, t, own_block))
    kvw3 = kvw.reshape(b, t, 6 * KV_WIDTH)
    kvc = _compress_prompt(kvw3, lw["cw"])
    cvh = kvc.astype(BF16).reshape(b, -1, 2, KV_HEADS, HEAD_DIM).transpose(0, 2, 3, 1, 4)
    o_nsa = _nsa_prompt(qh, ksa, kw, va, cvh, gates.reshape(b, t, KV_HEADS * LANES)).reshape(m, NSA_WIDTH)
    s0 = jnp.zeros((b, RWKV_HEADS, RWKV_HEAD, RWKV_HEAD), F32)
    pr3 = pr.reshape(b, t, RWKV_PROJ)
    o_rwkv, wkv = _rwkv(pr3, jnp.zeros((b, RWKV_PROJ), F32), s0, lw["rw"], chunk=64, tt=min(t, 512), n_valid=t)
    d_ff = lw["fw"]["wd"].shape[0]
    tm = min(t, 512)
    y, tail = _ffn_seq(x2, o_nsa, o_rwkv.reshape(m, RWKV_WIDTH), lw["fw"], t, tm=tm, tf=d_ff)
    tail = tail[t // tm - 1::t // tm]
    kv_rows = kvw3[:, :, :4 * KV_WIDTH].reshape(b, t, 4, KV_HEADS, HEAD_DIM)
    wn = min(WINDOW, t)
    win = kvw3[:, t - wn:, 4 * KV_WIDTH:].reshape(b, wn, 2, KV_HEADS, HEAD_DIM)
    return y.reshape(b, t, d), kv_rows, win, wkv, pr3[:, -1], tail[:, 8 - (CONV_WIDTH - 1):]


def _sample_layer(x, lw, tabs, l, cache_t, page_table, st_win, st_wkv, st_shift, st_conv):
    db, tn, d = x.shape
    past_len = page_table.shape[1] * cache_t.shape[3]
    x2 = x.reshape(db, d)
    q, kvw, gates, pr = _norm_proj(x2, lw["g_in"], lw["wq"], lw["wkv"], lw["wg"], lw["wr"], tabs, tm=db)
    qs = (q * (ATTN_SCALE * LOG2E)).astype(BF16).reshape(db, KV_HEADS, HEADS_PER_KV, HEAD_DIM)
    eye = jnp.eye(KV_HEADS, dtype=BF16)
    qz = jnp.einsum('bghd,gj->bghjd', qs, eye).reshape(db, KV_HEADS, HEADS_PER_KV, KV_WIDTH)
    qz = jnp.pad(qz, ((0, 0), (0, 0), (0, 8 - HEADS_PER_KV), (0, 0))).reshape(db, SROWS, KV_WIDTH)
    kvc = _compress_sample(cache_t, page_table, l, lw["cw"])
    n_cmp = past_len // CMP_STRIDE - 1
    n_sel = -(-(past_len + tn) // SEL_BLOCK)
    n_sel_pad = -(-n_sel // LANES) * LANES
    c2s = jnp.asarray(_cmp_to_sel(kvc.shape[1], n_cmp, n_sel_pad, n_sel))
    wb = st_win.shape[1]
    win = st_win.transpose(0, 2, 3, 4, 1).reshape(db, 2 * KV_WIDTH, wb)
    wnew = kvw[:, None, 4 * KV_WIDTH:]
    ocmp, owin, imp = _sample_cmp_win(past_len, n_cmp, qz, kvc, win, wnew, c2s)
    sel = _select(imp.reshape(db * 8, n_sel_pad), n_sel, past_len // SEL_BLOCK).reshape(db, 8, n_sel_pad)
    sel16 = jnp.broadcast_to(sel[:, :KV_HEADS, None, :], (db, KV_HEADS, 8, n_sel_pad)).reshape(db, SROWS, n_sel_pad)
    gz = gates.reshape(db, KV_HEADS, LANES)[:, :, :N_BRANCH * HEADS_PER_KV].reshape(db, KV_HEADS, HEADS_PER_KV, N_BRANCH)
    gz = jnp.pad(gz, ((0, 0), (0, 0), (0, 8 - HEADS_PER_KV), (0, 0))).reshape(db, SROWS, N_BRANCH)
    knew = kvw[:, None, 2 * KV_WIDTH:4 * KV_WIDTH]
    o16 = _sample_sel(cache_t, page_table, l, qz, sel16, knew, ocmp, owin, gz)
    o4 = o16.reshape(db, KV_HEADS, 8, KV_HEADS, HEAD_DIM)[:, :, :HEADS_PER_KV]
    o_nsa = jnp.stack([o4[:, g, :, g] for g in range(KV_HEADS)], axis=1).reshape(db, NSA_WIDTH)
    pad_t = 8
    pr3 = jnp.pad(pr[:, None, :], ((0, 0), (0, pad_t - tn), (0, 0)))
    o_rwkv, wkv = _rwkv(pr3, st_shift, st_wkv, lw["rw"], chunk=pad_t, tt=pad_t, n_valid=tn)
    prev = st_conv.transpose(1, 0, 2)
    d_ff = lw["fw"]["wd"].shape[0]
    y, gate = _ffn_step(x2, o_nsa.astype(BF16), o_rwkv[:, 0], lw["fw"], prev, tf=d_ff // 2)
    kv_rows = kvw[:, :4 * KV_WIDTH].reshape(db, tn, 4, KV_HEADS, HEAD_DIM)
    win_new = jnp.concatenate([st_win[:, tn:], kvw[:, 4 * KV_WIDTH:].reshape(db, tn, 2, KV_HEADS, HEAD_DIM)], axis=1)
    conv_new = jnp.concatenate([st_conv[:, 1:], gate[:, None, :]], axis=1)
    return y.reshape(db, tn, d), kv_rows, win_new, wkv, pr, conv_new


def kernel(x_prompt, x_sample, cache_kv, page_table, state_win, state_wkv, state_shift, state_conv, norm_mix_pre, norm_mix_post, norm_ffn_pre, norm_ffn_post, w_in, w_out, cmp_pe, cmp_w1, cmp_w2, rwkv_mu, rwkv_w0, rwkv_w_up, rwkv_a0, rwkv_a_up, rwkv_g_up, rwkv_k_k, rwkv_k_a, rwkv_r_k, rwkv_lnx_w, rwkv_lnx_b, ffn_w_up, ffn_conv_w, ffn_conv_b, ffn_w_down):
    p = dict(norm_mix_pre=norm_mix_pre, norm_mix_post=norm_mix_post, norm_ffn_pre=norm_ffn_pre,
             norm_ffn_post=norm_ffn_post, w_in=w_in, w_out=w_out, cmp_pe=cmp_pe, cmp_w1=cmp_w1, cmp_w2=cmp_w2,
             rwkv_mu=rwkv_mu, rwkv_w0=rwkv_w0, rwkv_w_up=rwkv_w_up, rwkv_a0=rwkv_a0, rwkv_a_up=rwkv_a_up,
             rwkv_g_up=rwkv_g_up, rwkv_k_k=rwkv_k_k, rwkv_k_a=rwkv_k_a, rwkv_r_k=rwkv_r_k,
             rwkv_lnx_w=rwkv_lnx_w, rwkv_lnx_b=rwkv_lnx_b, ffn_w_up=ffn_w_up, ffn_conv_w=ffn_conv_w,
             ffn_conv_b=ffn_conv_b, ffn_w_down=ffn_w_down)
    depth = w_in.shape[0]
    b, t, _ = x_prompt.shape
    db, tn, _ = x_sample.shape
    n_pool, page = cache_kv.shape[1], cache_kv.shape[2]
    past_len = page_table.shape[1] * page
    cache_t = cache_kv.transpose(0, 1, 3, 4, 5, 2).reshape(depth, n_pool, 2 * CMP_LANES, page)
    tabs_p = _rope_tables(jnp.arange(t, dtype=jnp.int32))
    tabs_s = _rope_tables(jnp.full((db,), past_len, jnp.int32))
    xp, xs = x_prompt, x_sample
    outs_p, outs_s = [], []
    for l in range(depth):
        lw = _layer_weights(l, p)
        xp, *st_p = _prompt_layer(xp, lw, tabs_p)
        xs, *st_s = _sample_layer(xs, lw, tabs_s, l, cache_t, page_table, state_win[l], state_wkv[l],
                                  state_shift[l], state_conv[l])
        outs_p.append(st_p)
        outs_s.append(st_s)
    stack = lambda outs, i: jnp.stack([o[i] for o in outs])
    return (xp, xs, stack(outs_p, 0), stack(outs_s, 0), stack(outs_p, 1), stack(outs_s, 1),
            stack(outs_p, 2), stack(outs_s, 2), stack(outs_p, 3), stack(outs_s, 3),
            stack(outs_p, 4), stack(outs_s, 4))
```
